```python
import math
import jax, jax.numpy as jnp
from jax import lax
import numpy as np


D_MODEL = 1024
BATCH = 32
SEQ = 2048
DEPTH = 4
DEC_BATCH = 4
DEC_SEQ = 8192
PAST_LEN = 128

HEAD_DIM = 64
WINDOWS = (128, 512, 2048)
DILATIONS = (1, 4, 16)
N_GROUPS = 3
HEADS_PER_GROUP = 8
N_ATT_HEADS = N_GROUPS * HEADS_PER_GROUP
ATT_WIDTH = N_ATT_HEADS * HEAD_DIM
ATT_OUT = HEADS_PER_GROUP * HEAD_DIM
CONV_A_CH = D_MODEL // 2
CONV_A_K = 31
EVEN_IN = 2 * CONV_A_CH + 3 * ATT_WIDTH
EVEN_MIX = CONV_A_CH + ATT_OUT
CONV_C_CH = D_MODEL
CONV_C_K = 3
FFN_HIDDEN = ((8 * D_MODEL // 3 + 255) // 256) * 256
N_BUCKETS = 32
MAX_DISTANCE = 1024
N_EVEN = (DEPTH + 1) // 2
N_ODD = DEPTH // 2
EPS = 1e-6
NEG_INF = -1e30

kernel_name = "hybrid_conformer_dilated_shortconv_encoder"


def rms_norm(x, g):
    xf = x.astype(jnp.float32)
    y = xf * lax.rsqrt(jnp.mean(xf * xf, axis=-1, keepdims=True) + EPS)
    return (y * g.astype(jnp.float32)).astype(x.dtype)


def layer_norm(x, g, b):
    xf = x.astype(jnp.float32)
    mu = jnp.mean(xf, axis=-1, keepdims=True)
    xc = xf - mu
    y = xc * lax.rsqrt(jnp.mean(xc * xc, axis=-1, keepdims=True) + EPS)
    return (y * g.astype(jnp.float32) + b.astype(jnp.float32)).astype(x.dtype)


def depthwise_conv(x, w, b=None):
    K, C = w.shape
    pad = (K - 1) // 2
    y = lax.conv_general_dilated(x, w[:, None, :].astype(x.dtype), window_strides=(1,),
                                 padding=[(pad, pad)], dimension_numbers=("NWC", "WIO", "NWC"),
                                 feature_group_count=C)
    if b is not None:
        y = y + b.astype(y.dtype)
    return y


def t5_bucket(rel):
    nb = N_BUCKETS // 2
    max_exact = nb // 2
    n = jnp.abs(rel)
    nf = jnp.maximum(n, 1).astype(jnp.float32)
    large = max_exact + (jnp.log(nf / max_exact) / math.log(MAX_DISTANCE / max_exact)
                         * (nb - max_exact)).astype(jnp.int32)
    large = jnp.minimum(large, nb - 1)
    return jnp.where(rel > 0, nb, 0) + jnp.where(n < max_exact, n, large)


def dilated_group_attention(q, k, v, table, dil, half):
    B, S, H, Dh = q.shape
    L = S // dil
    blk = half
    nb = -(-L // blk)
    Lp = nb * blk

    def classes(t):
        return t.reshape(B, L, dil, H, Dh).transpose(0, 2, 1, 3, 4)

    qb = jnp.pad(classes(q), ((0, 0), (0, 0), (0, Lp - L), (0, 0), (0, 0))).reshape(B, dil, nb, blk, H, Dh)

    def windows(t):
        tp = jnp.pad(classes(t), ((0, 0), (0, 0), (blk, Lp - L + blk), (0, 0), (0, 0)))
        tp = tp.reshape(B, dil, nb + 2, blk, H, Dh)
        return jnp.concatenate([tp[:, :, :-2], tp[:, :, 1:-1], tp[:, :, 2:]], axis=3)

    kw = windows(k)
    vw = windows(v)
    qi = jnp.arange(blk)[:, None]
    kj = jnp.arange(3 * blk)
    rel = kj[None, :] - blk - qi
    band = jnp.abs(rel) <= half
    bias = jnp.take(table, t5_bucket(rel * dil), axis=0).astype(jnp.float32).transpose(2, 0, 1)
    kpos = jnp.arange(nb)[:, None] * blk + kj[None, :] - blk
    kvalid = (kpos >= 0) & (kpos < L)
    mask = band[None, :, :] & kvalid[:, None, :]

    logits = jnp.einsum("bcnqhd,bcnkhd->bcnhqk", qb, kw) + bias[None, None, None]
    logits = jnp.where(mask[None, None, :, None], logits, NEG_INF)
    m = jnp.max(logits, axis=-1, keepdims=True)
    p = jnp.exp(logits - m)
    s = jnp.sum(p, axis=-1, keepdims=True)
    o = jnp.einsum("bcnhqk,bcnkhd->bcnqhd", p, vw) / s.transpose(0, 1, 2, 4, 3, 5)
    lse = (m + jnp.log(s))[..., 0].transpose(0, 1, 2, 4, 3)

    def from_classes(t):
        t = t.reshape((B, dil, Lp) + t.shape[4:])[:, :, :L]
        t = jnp.moveaxis(t, 1, 2)
        return t.reshape((B, S) + t.shape[3:])

    return from_classes(o), from_classes(lse)


def dilated_attention(q, k, v, table):
    outs, lses = [], []
    for g in range(N_GROUPS):
        sl = slice(g * HEADS_PER_GROUP, (g + 1) * HEADS_PER_GROUP)
        half = WINDOWS[g] // (2 * DILATIONS[g])
        o, l = dilated_group_attention(q[:, :, sl], k[:, :, sl], v[:, :, sl], table[:, sl], DILATIONS[g], half)
        outs.append(o)
        lses.append(l)
    alpha = jax.nn.softmax(jnp.stack(lses, axis=0), axis=0)
    return jnp.sum(alpha[..., None] * jnp.stack(outs, axis=0), axis=0)


def even_layer(x, table, norm_g, w_in, conv_w, conv_b, ln_g, ln_b, q_g, k_g, w_out):
    B, S, _ = x.shape
    h = rms_norm(x, norm_g)
    proj = h @ w_in
    a_val, a_gate, q, k, v = jnp.split(
        proj, [CONV_A_CH, 2 * CONV_A_CH, 2 * CONV_A_CH + ATT_WIDTH, 2 * CONV_A_CH + 2 * ATT_WIDTH], axis=-1)
    a = a_val * jax.nn.sigmoid(a_gate)
    a = depthwise_conv(a, conv_w, conv_b)
    a = jax.nn.silu(layer_norm(a, ln_g, ln_b))
    heads = lambda t: t.reshape(B, S, N_ATT_HEADS, HEAD_DIM).astype(jnp.float32)
    qh = rms_norm(heads(q), q_g) * (HEAD_DIM ** -0.5)
    kh = rms_norm(heads(k), k_g)
    o = dilated_attention(qh, kh, heads(v), table).reshape(B, S, ATT_OUT).astype(x.dtype)
    y = jnp.concatenate([a, o], axis=-1) @ w_out
    return x + y


def odd_layer(x, norm_g, w_in, conv_w, w_out):
    h = rms_norm(x, norm_g)
    b_gate, c_gate, u = jnp.split(h @ w_in, 3, axis=-1)
    y = b_gate * depthwise_conv(c_gate * u, conv_w)
    return x + y @ w_out


def ffn_layer(x, norm_g, w1, w3, w2):
    h = rms_norm(x, norm_g)
    return x + (jax.nn.silu(h @ w1) * (h @ w3)) @ w2


def trunk(x, rel_bias, even_norm, even_w_in, conv_a_w, conv_a_b, conv_a_ln_g, conv_a_ln_b,
          q_norm, k_norm, even_w_out, odd_norm, odd_w_in, conv_c_w, odd_w_out,
          ffn_norm, ffn_w1, ffn_w3, ffn_w2):
    for layer in range(DEPTH):
        i = layer // 2
        if layer % 2 == 0:
            x = even_layer(x, rel_bias, even_norm[i], even_w_in[i], conv_a_w[i], conv_a_b[i],
                           conv_a_ln_g[i], conv_a_ln_b[i], q_norm[i], k_norm[i], even_w_out[i])
        else:
            x = odd_layer(x, odd_norm[i], odd_w_in[i], conv_c_w[i], odd_w_out[i])
        x = ffn_layer(x, ffn_norm[layer], ffn_w1[layer], ffn_w3[layer], ffn_w2[layer])
    return x


def setup_inputs(seed: int = 0) -> dict:
    key = jax.random.key(seed)
    ks = jax.random.split(key, 24)
    nrm = lambda k, shape, scale: jax.random.normal(k, shape, jnp.float32) * scale
    res = (2 * DEPTH) ** -0.5
    return {
        "x_prompt": nrm(ks[0], (BATCH, SEQ, D_MODEL), 1.0),
        "x_sample": nrm(ks[1], (DEC_BATCH, DEC_SEQ, D_MODEL), 1.0),
        "rel_bias": nrm(ks[2], (N_BUCKETS, N_ATT_HEADS), 0.5),
        "even_norm": 1.0 + nrm(ks[3], (N_EVEN, D_MODEL), 0.02),
        "even_w_in": nrm(ks[4], (N_EVEN, D_MODEL, EVEN_IN), D_MODEL ** -0.5),
        "conv_a_w": nrm(ks[5], (N_EVEN, CONV_A_K, CONV_A_CH), CONV_A_K ** -0.5),
        "conv_a_b": nrm(ks[6], (N_EVEN, CONV_A_CH), 0.02),
        "conv_a_ln_g": 1.0 + nrm(ks[7], (N_EVEN, CONV_A_CH), 0.02),
        "conv_a_ln_b": nrm(ks[8], (N_EVEN, CONV_A_CH), 0.02),
        "q_norm": 1.0 + nrm(ks[9], (N_EVEN, HEAD_DIM), 0.02),
        "k_norm": 1.0 + nrm(ks[10], (N_EVEN, HEAD_DIM), 0.02),
        "even_w_out": nrm(ks[11], (N_EVEN, EVEN_MIX, D_MODEL), EVEN_MIX ** -0.5 * res),
        "odd_norm": 1.0 + nrm(ks[12], (N_ODD, D_MODEL), 0.02),
        "odd_w_in": nrm(ks[13], (N_ODD, D_MODEL, 3 * CONV_C_CH), D_MODEL ** -0.5),
        "conv_c_w": nrm(ks[14], (N_ODD, CONV_C_K, CONV_C_CH), CONV_C_K ** -0.5),
        "odd_w_out": nrm(ks[15], (N_ODD, CONV_C_CH, D_MODEL), CONV_C_CH ** -0.5 * res),
        "ffn_norm": 1.0 + nrm(ks[16], (DEPTH, D_MODEL), 0.02),
        "ffn_w1": nrm(ks[17], (DEPTH, D_MODEL, FFN_HIDDEN), D_MODEL ** -0.5),
        "ffn_w3": nrm(ks[18], (DEPTH, D_MODEL, FFN_HIDDEN), D_MODEL ** -0.5),
        "ffn_w2": nrm(ks[19], (DEPTH, FFN_HIDDEN, D_MODEL), FFN_HIDDEN ** -0.5 * res),
    }


def reference(x_prompt, x_sample, rel_bias, even_norm, even_w_in, conv_a_w, conv_a_b, conv_a_ln_g,
              conv_a_ln_b, q_norm, k_norm, even_w_out, odd_norm, odd_w_in, conv_c_w, odd_w_out,
              ffn_norm, ffn_w1, ffn_w3, ffn_w2):
    params = (rel_bias, even_norm, even_w_in, conv_a_w, conv_a_b, conv_a_ln_g, conv_a_ln_b,
              q_norm, k_norm, even_w_out, odd_norm, odd_w_in, conv_c_w, odd_w_out,
              ffn_norm, ffn_w1, ffn_w3, ffn_w2)
    y_prompt = trunk(x_prompt, *params)
    y_sample = trunk(x_sample, *params)
    return (y_prompt, y_sample)
```

```python
import functools
import math

import numpy as np
import jax
import jax.numpy as jnp
from jax import lax
from jax.experimental import pallas as pl
from jax.experimental.pallas import tpu as pltpu

F32 = jnp.float32
BF16 = jnp.bfloat16

D_MODEL = 1024
DEPTH = 4
HEAD_DIM = 64
WINDOWS = (128, 512, 2048)
DILATIONS = (1, 4, 16)
N_GROUPS = 3
HEADS_PER_GROUP = 8
GROUP_W = HEADS_PER_GROUP * HEAD_DIM
ATT_WIDTH = N_GROUPS * GROUP_W
QKV_W = 3 * ATT_WIDTH
CONV_A_CH = D_MODEL // 2
CONV_A_K = 31
EVEN_IN = 2 * CONV_A_CH + QKV_W
FFN_HIDDEN = 2816
N_BUCKETS = 32
MAX_DISTANCE = 1024
EPS = 1e-6
NEG_INF = -1e30
HALF = 64
assert all(w // (2 * d) == HALF for w, d in zip(WINDOWS, DILATIONS))

V7X_VMEM_BYTES = 64 * 1024 * 1024
VMEM_LIMIT_BYTES = V7X_VMEM_BYTES * 7 // 8

TM = 512
ATT_SUB = 128
ATT_KEYS = ATT_SUB + 2 * HALF
ATT_STEP = 512
CONV_A_TS = 256
CONV_A_RB = 32
CONV_A_HALO = 16
FFN_CHUNK = 1408


def _params(n_axes):
    return pltpu.CompilerParams(dimension_semantics=("parallel",) * n_axes,
                                vmem_limit_bytes=VMEM_LIMIT_BYTES)


def _resident(a):
    nd = a.ndim
    return pl.BlockSpec(a.shape, lambda *_: (0,) * nd)


def _rows(tm, width):
    return pl.BlockSpec((tm, width), lambda i: (i, 0))


def _rmsnorm_bf16(x, g):
    ms = jnp.mean(x * x, axis=-1, keepdims=True)
    return (x * lax.rsqrt(ms + EPS) * g).astype(BF16)


def _dot(a, b):
    return jnp.dot(a, b, preferred_element_type=F32)


def _dot_split(a, b):
    hi = a.astype(BF16)
    lo = (a - hi.astype(F32)).astype(BF16)
    return _dot(hi, b) + _dot(lo, b)


def _even_in_kernel(x_ref, g_ref, w_ref, seg_ref, qg_ref, kg_ref, a_ref, qkv_ref):
    h = _rmsnorm_bf16(x_ref[...], g_ref[...])

    def proj(c):
        return _dot(h, w_ref[:, c * GROUP_W:(c + 1) * GROUP_W])

    a_ref[...] = proj(0) * jax.nn.sigmoid(proj(1))
    seg = seg_ref[...]
    for c in range(2, 2 + 2 * N_GROUPS):
        p = proj(c)
        gain = qg_ref[...] if c < 2 + N_GROUPS else kg_ref[...]
        ms = _dot_split(p * p, seg) * (1.0 / HEAD_DIM)
        qkv_ref[:, (c - 2) * GROUP_W:(c - 1) * GROUP_W] = (p * lax.rsqrt(ms + EPS) * gain).astype(BF16)
    for c in range(2 + 2 * N_GROUPS, 2 + 3 * N_GROUPS):
        qkv_ref[:, (c - 2) * GROUP_W:(c - 1) * GROUP_W] = proj(c).astype(BF16)


def _even_in(x, g, w, seg, qg, kg):
    t = x.shape[0]
    return pl.pallas_call(
        _even_in_kernel,
        grid=(t // TM,),
        in_specs=[_rows(TM, D_MODEL), _resident(g), _resident(w), _resident(seg), _resident(qg), _resident(kg)],
        out_specs=[_rows(TM, CONV_A_CH), _rows(TM, QKV_W)],
        out_shape=[jax.ShapeDtypeStruct((t, CONV_A_CH), F32), jax.ShapeDtypeStruct((t, QKV_W), BF16)],
        compiler_params=_params(1),
        name="even_in",
    )(x, g, w, seg, qg, kg)


def _conv_a_kernel(prev_ref, a_ref, next_ref, w_ref, b_ref, lg_ref, lb_ref, o_ref, buf_ref, *, n_tiles):
    i = pl.program_id(1)
    ts = a_ref.shape[0]
    h = CONV_A_HALO
    buf_ref[0:h, :] = jnp.where(i > 0, prev_ref[...], 0.0)
    buf_ref[h:h + ts, :] = a_ref[...]
    buf_ref[h + ts:2 * h + ts, :] = jnp.where(i < n_tiles - 1, next_ref[...], 0.0)
    pad = (CONV_A_K - 1) // 2
    for r in range(0, ts, CONV_A_RB):
        acc = None
        for k in range(CONV_A_K):
            lo = h + r + k - pad
            term = buf_ref[lo:lo + CONV_A_RB, :] * w_ref[k:k + 1, :]
            acc = term if acc is None else acc + term
        acc = acc + b_ref[...]
        mu = jnp.mean(acc, axis=-1, keepdims=True)
        xc = acc - mu
        var = jnp.mean(xc * xc, axis=-1, keepdims=True)
        y = xc * lax.rsqrt(var + EPS) * lg_ref[...] + lb_ref[...]
        o_ref[r:r + CONV_A_RB, :] = (y * jax.nn.sigmoid(y)).astype(BF16)


def _conv_a(a, w, b, lg, lb, batch, seq):
    ts = CONV_A_TS
    n_tiles = seq // ts
    hb = ts // CONV_A_HALO
    a3 = a.reshape(batch, seq, CONV_A_CH)
    main = pl.BlockSpec((None, ts, CONV_A_CH), lambda bi, i: (bi, i, 0))
    prev = pl.BlockSpec((None, CONV_A_HALO, CONV_A_CH), lambda bi, i: (bi, jnp.maximum(i * hb - 1, 0), 0))
    nxt = pl.BlockSpec((None, CONV_A_HALO, CONV_A_CH),
                       lambda bi, i: (bi, jnp.minimum((i + 1) * hb, seq // CONV_A_HALO - 1), 0))
    out = pl.pallas_call(
        functools.partial(_conv_a_kernel, n_tiles=n_tiles),
        grid=(batch, n_tiles),
        in_specs=[prev, main, nxt, _resident(w), _resident(b), _resident(lg), _resident(lb)],
        out_specs=main,
        out_shape=jax.ShapeDtypeStruct((batch, seq, CONV_A_CH), BF16),
        scratch_shapes=[pltpu.VMEM((ts + 2 * CONV_A_HALO, CONV_A_CH), F32)],
        compiler_params=_params(2),
        name="conv_a",
    )(a3, a3, a3, w, b, lg, lb)
    return out.reshape(batch * seq, CONV_A_CH)


def _attn_kernel(q_ref, kp_ref, kc_ref, kn_ref, vp_ref, vc_ref, vn_ref, bm_ref, o_ref, l_ref,
                 kw_ref, vw_ref, *, n_tiles):
    i = pl.program_id(2)
    tq = q_ref.shape[0]
    for dst, (p, c, n) in ((kw_ref, (kp_ref, kc_ref, kn_ref)), (vw_ref, (vp_ref, vc_ref, vn_ref))):
        dst[0:HALF, :] = p[...]
        dst[HALF:HALF + tq, :] = c[...]
        dst[HALF + tq:2 * HALF + tq, :] = n[...]
    l_ref[...] = jnp.zeros(l_ref.shape, F32)
    col = lax.broadcasted_iota(jnp.int32, (1, ATT_KEYS), 1)
    n_sub = tq // ATT_SUB
    for s in range(n_sub):
        r0 = s * ATT_SUB
        lo = jnp.where(i == 0, HALF, 0) if s == 0 else None
        hi = jnp.where(i == n_tiles - 1, ATT_KEYS - HALF, ATT_KEYS) if s == n_sub - 1 else None
        edge = None
        if lo is not None or hi is not None:
            ok = col >= (lo if lo is not None else 0)
            if hi is not None:
                ok = ok & (col < hi)
            edge = jnp.where(ok, 0.0, NEG_INF)
        for j in range(HEADS_PER_GROUP):
            cs = slice(j * HEAD_DIM, (j + 1) * HEAD_DIM)
            q = q_ref[r0:r0 + ATT_SUB, cs]
            k = kw_ref[r0:r0 + ATT_KEYS, cs]
            v = vw_ref[r0:r0 + ATT_KEYS, cs]
            sc = lax.dot_general(q, k, (((1,), (1,)), ((), ())), preferred_element_type=F32) + bm_ref[j]
            if edge is not None:
                sc = sc + edge
            m = jnp.max(sc, axis=-1, keepdims=True)
            p = jnp.exp(sc - m)
            den = jnp.sum(p, axis=-1, keepdims=True)
            o = _dot(p.astype(BF16), v) / den
            o_ref[r0:r0 + ATT_SUB, cs] = o.astype(BF16)
            l_ref[r0:r0 + ATT_SUB, j:j + 1] = m + jnp.log(den)


def _attention_group(qkv, bm, g, batch, seq):
    d = DILATIONS[g]
    cls_len = seq // d
    tq = min(cls_len, ATT_STEP)
    n_tiles = cls_len // tq
    hb = tq // HALF
    n_col = QKV_W // GROUP_W
    view = qkv.reshape(batch, cls_len, d * QKV_W)

    def main(c):
        return pl.BlockSpec((None, tq, GROUP_W), lambda b, r, i: (b, i, r * n_col + c))

    def before(c):
        return pl.BlockSpec((None, HALF, GROUP_W), lambda b, r, i: (b, jnp.maximum(i * hb - 1, 0), r * n_col + c))

    def after(c):
        return pl.BlockSpec((None, HALF, GROUP_W),
                            lambda b, r, i: (b, jnp.minimum((i + 1) * hb, cls_len // HALF - 1), r * n_col + c))

    kc, vc = N_GROUPS + g, 2 * N_GROUPS + g
    o, l = pl.pallas_call(
        functools.partial(_attn_kernel, n_tiles=n_tiles),
        grid=(batch, d, n_tiles),
        in_specs=[main(g), before(kc), main(kc), after(kc), before(vc), main(vc), after(vc), _resident(bm)],
        out_specs=[pl.BlockSpec((None, tq, GROUP_W), lambda b, r, i: (b, i, r)),
                   pl.BlockSpec((None, tq, 128), lambda b, r, i: (b, i, r))],
        out_shape=[jax.ShapeDtypeStruct((batch, cls_len, d * GROUP_W), BF16),
                   jax.ShapeDtypeStruct((batch, cls_len, d * 128), F32)],
        scratch_shapes=[pltpu.VMEM((tq + 2 * HALF, GROUP_W), BF16), pltpu.VMEM((tq + 2 * HALF, GROUP_W), BF16)],
        compiler_params=_params(3),
        name=f"attn_g{g}",
    )(view, view, view, view, view, view, view, bm)
    return o.reshape(batch * seq, GROUP_W), l.reshape(batch * seq, 128)


def _even_out_kernel(x_ref, a_ref, o0_ref, o1_ref, o2_ref, l0_ref, l1_ref, l2_ref, ex_ref, w_ref, y_ref):
    lses = (l0_ref[...], l1_ref[...], l2_ref[...])
    m = jnp.maximum(jnp.maximum(lses[0], lses[1]), lses[2])
    es = [jnp.exp(l - m) for l in lses]
    inv = 1.0 / (es[0] + es[1] + es[2])
    ex = ex_ref[...]
    o = None
    for e, o_ref in zip(es, (o0_ref, o1_ref, o2_ref)):
        term = _dot_split(e * inv, ex) * o_ref[...].astype(F32)
        o = term if o is None else o + term
    y = _dot(a_ref[...], w_ref[0:CONV_A_CH, :]) + _dot(o.astype(BF16), w_ref[CONV_A_CH:, :])
    y_ref[...] = x_ref[...] + y


def _even_out(x, a, os, ls, ex, w):
    t = x.shape[0]
    return pl.pallas_call(
        _even_out_kernel,
        grid=(t // TM,),
        in_specs=[_rows(TM, D_MODEL), _rows(TM, CONV_A_CH)] + [_rows(TM, GROUP_W)] * 3 + [_rows(TM, 128)] * 3
                 + [_resident(ex), _resident(w)],
        out_specs=_rows(TM, D_MODEL),
        out_shape=jax.ShapeDtypeStruct((t, D_MODEL), F32),
        compiler_params=_params(1),
        name="even_out",
    )(x, a, *os, *ls, ex, w)


def _odd_in_kernel(x_ref, g_ref, w_ref, b_ref, cu_ref):
    h = _rmsnorm_bf16(x_ref[...], g_ref[...])
    b_ref[...] = _dot(h, w_ref[:, 0:D_MODEL])
    cu_ref[...] = _dot(h, w_ref[:, D_MODEL:2 * D_MODEL]) * _dot(h, w_ref[:, 2 * D_MODEL:3 * D_MODEL])


def _odd_in(x, g, w):
    t = x.shape[0]
    return pl.pallas_call(
        _odd_in_kernel,
        grid=(t // TM,),
        in_specs=[_rows(TM, D_MODEL), _resident(g), _resident(w)],
        out_specs=[_rows(TM, D_MODEL), _rows(TM, D_MODEL)],
        out_shape=[jax.ShapeDtypeStruct((t, D_MODEL), F32)] * 2,
        compiler_params=_params(1),
        name="odd_in",
    )(x, g, w)


def _odd_out_kernel(x_ref, b_ref, prev_ref, cu_ref, next_ref, cw_ref, w_ref, y_ref, buf_ref, *, n_tiles):
    i = pl.program_id(1)
    ts = cu_ref.shape[0]
    buf_ref[0:8, :] = jnp.where(i > 0, prev_ref[...], 0.0)
    buf_ref[8:8 + ts, :] = cu_ref[...]
    buf_ref[8 + ts:16 + ts, :] = jnp.where(i < n_tiles - 1, next_ref[...], 0.0)
    conv = (buf_ref[7:7 + ts, :] * cw_ref[0:1, :] + buf_ref[8:8 + ts, :] * cw_ref[1:2, :]
            + buf_ref[9:9 + ts, :] * cw_ref[2:3, :])
    y_ref[...] = x_ref[...] + _dot((b_ref[...] * conv).astype(BF16), w_ref[...])


def _odd_out(x, b, cu, cw, w, batch, seq):
    ts = TM
    n_tiles = seq // ts
    hb = ts // 8
    x3, b3, cu3 = (a.reshape(batch, seq, D_MODEL) for a in (x, b, cu))
    main = pl.BlockSpec((None, ts, D_MODEL), lambda bi, i: (bi, i, 0))
    prev = pl.BlockSpec((None, 8, D_MODEL), lambda bi, i: (bi, jnp.maximum(i * hb - 1, 0), 0))
    nxt = pl.BlockSpec((None, 8, D_MODEL), lambda bi, i: (bi, jnp.minimum((i + 1) * hb, seq // 8 - 1), 0))
    y = pl.pallas_call(
        functools.partial(_odd_out_kernel, n_tiles=n_tiles),
        grid=(batch, n_tiles),
        in_specs=[main, main, prev, main, nxt, _resident(cw), _resident(w)],
        out_specs=main,
        out_shape=jax.ShapeDtypeStruct((batch, seq, D_MODEL), F32),
        scratch_shapes=[pltpu.VMEM((ts + 16, D_MODEL), F32)],
        compiler_params=_params(2),
        name="odd_out",
    )(x3, b3, cu3, cu3, cu3, cw, w)
    return y.reshape(batch * seq, D_MODEL)


def _ffn_kernel(x_ref, g_ref, w1_ref, w3_ref, w2_ref, y_ref):
    x = x_ref[...]
    h = _rmsnorm_bf16(x, g_ref[...])
    y = x
    for c in range(0, FFN_HIDDEN, FFN_CHUNK):
        a = _dot(h, w1_ref[:, c:c + FFN_CHUNK])
        b = _dot(h, w3_ref[:, c:c + FFN_CHUNK])
        y = y + _dot((a * jax.nn.sigmoid(a) * b).astype(BF16), w2_ref[c:c + FFN_CHUNK, :])
    y_ref[...] = y


def _ffn(x, g, w1, w3, w2):
    t = x.shape[0]
    return pl.pallas_call(
        _ffn_kernel,
        grid=(t // TM,),
        in_specs=[_rows(TM, D_MODEL), _resident(g), _resident(w1), _resident(w3), _resident(w2)],
        out_specs=_rows(TM, D_MODEL),
        out_shape=jax.ShapeDtypeStruct((t, D_MODEL), F32),
        compiler_params=_params(1),
        name="ffn",
    )(x, g, w1, w3, w2)


def _t5_bucket_np(rel):
    nb = N_BUCKETS // 2
    max_exact = nb // 2
    n = np.abs(rel)
    nf = np.maximum(n, 1).astype(np.float32)
    large = max_exact + (np.log(nf / max_exact) / math.log(MAX_DISTANCE / max_exact) * (nb - max_exact)).astype(np.int32)
    large = np.minimum(large, nb - 1)
    return np.where(rel > 0, nb, 0) + np.where(n < max_exact, n, large)


def _bias_windows(rel_bias):
    off = np.arange(ATT_KEYS)[None, :] - HALF - np.arange(ATT_SUB)[:, None]
    band = np.abs(off) <= HALF
    out = []
    for g, d in enumerate(DILATIONS):
        bucket = _t5_bucket_np(off * d).reshape(-1)
        tab = rel_bias[:, g * HEADS_PER_GROUP:(g + 1) * HEADS_PER_GROUP].astype(F32)
        b = jnp.take(tab, jnp.asarray(bucket), axis=0).reshape(ATT_SUB, ATT_KEYS, HEADS_PER_GROUP)
        out.append(jnp.where(band[..., None], b, NEG_INF).transpose(2, 0, 1))
    return out


def _head_segments():
    h = np.arange(GROUP_W) // HEAD_DIM
    return jnp.asarray(h[:, None] == h[None, :], BF16)


def _head_expand():
    return jnp.asarray(np.arange(128)[:, None] == (np.arange(GROUP_W) // HEAD_DIM)[None, :], BF16)


def _trunk(x3, p):
    batch, seq, _ = x3.shape
    x = x3.reshape(batch * seq, D_MODEL)
    for layer in range(DEPTH):
        i = layer // 2
        if layer % 2 == 0:
            a, qkv = _even_in(x, p["even_norm"][i], p["even_w_in"][i], p["seg"], p["q_gain"][i], p["k_gain"][i])
            a = _conv_a(a, p["conv_a_w"][i], p["conv_a_b"][i], p["conv_a_ln_g"][i], p["conv_a_ln_b"][i], batch, seq)
            os, ls = zip(*[_attention_group(qkv, p["bias"][g], g, batch, seq) for g in range(N_GROUPS)])
            x = _even_out(x, a, os, ls, p["expand"], p["even_w_out"][i])
        else:
            b, cu = _odd_in(x, p["odd_norm"][i], p["odd_w_in"][i])
            x = _odd_out(x, b, cu, p["conv_c_w"][i], p["odd_w_out"][i], batch, seq)
        x = _ffn(x, p["ffn_norm"][layer], p["ffn_w1"][layer], p["ffn_w3"][layer], p["ffn_w2"][layer])
    return x.reshape(batch, seq, D_MODEL)


def _prepare(rel_bias, even_norm, even_w_in, conv_a_w, conv_a_b, conv_a_ln_g, conv_a_ln_b, q_norm, k_norm,
             even_w_out, odd_norm, odd_w_in, conv_c_w, odd_w_out, ffn_norm, ffn_w1, ffn_w3, ffn_w2):
    row = lambda a: a.astype(F32)[:, None, :]
    tile_heads = lambda a: jnp.tile(a.astype(F32), (1, HEADS_PER_GROUP))[:, None, :]
    return dict(
        bias=_bias_windows(rel_bias), seg=_head_segments(), expand=_head_expand(),
        even_norm=row(even_norm), even_w_in=even_w_in.astype(BF16),
        conv_a_w=conv_a_w.astype(F32), conv_a_b=row(conv_a_b), conv_a_ln_g=row(conv_a_ln_g), conv_a_ln_b=row(conv_a_ln_b),
        q_gain=tile_heads(q_norm) * (HEAD_DIM ** -0.5), k_gain=tile_heads(k_norm),
        even_w_out=even_w_out.astype(BF16),
        odd_norm=row(odd_norm), odd_w_in=odd_w_in.astype(BF16), conv_c_w=conv_c_w.astype(F32),
        odd_w_out=odd_w_out.astype(BF16),
        ffn_norm=row(ffn_norm), ffn_w1=ffn_w1.astype(BF16), ffn_w3=ffn_w3.astype(BF16), ffn_w2=ffn_w2.astype(BF16),
    )


def kernel(x_prompt, x_sample, rel_bias, even_norm, even_w_in, conv_a_w, conv_a_b, conv_a_ln_g, conv_a_ln_b, q_norm, k_norm, even_w_out, odd_norm, odd_w_in, conv_c_w, odd_w_out, ffn_norm, ffn_w1, ffn_w3, ffn_w2):
    p = _prepare(rel_bias, even_norm, even_w_in, conv_a_w, conv_a_b, conv_a_ln_g, conv_a_ln_b, q_norm, k_norm,
                 even_w_out, odd_norm, odd_w_in, conv_c_w, odd_w_out, ffn_norm, ffn_w1, ffn_w3, ffn_w2)
    return (_trunk(x_prompt, p), _trunk(x_sample, p))
```

```python
import functools
import math

import numpy as np
import jax
import jax.numpy as jnp
from jax import lax
from jax.experimental import pallas as pl
from jax.experimental.pallas import tpu as pltpu

F32 = jnp.float32
BF16 = jnp.bfloat16

D_MODEL = 1024
DEPTH = 4
HEAD_DIM = 64
WINDOWS = (128, 512, 2048)
DILATIONS = (1, 4, 16)
N_GROUPS = 3
HEADS_PER_GROUP = 8
GROUP_W = HEADS_PER_GROUP * HEAD_DIM
GROUP_QKV = 3 * GROUP_W
CONV_A_CH = D_MODEL // 2
CONV_A_K = 31
FFN_HIDDEN = 2816
N_BUCKETS = 32
MAX_DISTANCE = 1024
EPS = 1e-6
NEG_INF = -1e30
HALF = 64
assert all(w // (2 * d) == HALF for w, d in zip(WINDOWS, DILATIONS))

V7X_VMEM_BYTES = 64 * 1024 * 1024
VMEM_LIMIT_BYTES = V7X_VMEM_BYTES * 7 // 8
LANES = 128

TM = 512
PERM = 256
ATT_SUB = 128
ATT_KEYS = ATT_SUB + 2 * HALF
ATT_STEP = 512
N_PAIRS = GROUP_W // LANES
CONV_A_TS = 256
CONV_A_RB = 32
CONV_A_HALO = 16
FFN_CHUNK = 1408


def _params(n_axes):
    return pltpu.CompilerParams(dimension_semantics=("parallel",) * n_axes,
                                vmem_limit_bytes=VMEM_LIMIT_BYTES)


def _resident(a):
    nd = a.ndim
    return pl.BlockSpec(a.shape, lambda *_: (0,) * nd)


def _rows(tm, width):
    return pl.BlockSpec((None, tm, width), lambda b, i: (b, i, 0))


def _classes(d, tm, width):
    if d == 1:
        return pl.BlockSpec((None, None, tm, width), lambda b, i: (b, 0, i, 0))
    return pl.BlockSpec((None, d, tm // d, width), lambda b, i: (b, 0, i, 0))


def _rmsnorm_bf16(x, g):
    ms = jnp.mean(x * x, axis=-1, keepdims=True)
    return (x * lax.rsqrt(ms + EPS) * g).astype(BF16)


def _dot(a, b):
    return jnp.dot(a, b, preferred_element_type=F32)


def _split3(a):
    hi = a.astype(BF16)
    r = a - hi.astype(F32)
    mid = r.astype(BF16)
    return hi, mid, (r - mid.astype(F32)).astype(BF16)


def _dot_exact(a, b, left=False):
    parts = [(_dot(b, t) if left else _dot(t, b)) for t in _split3(a)]
    return parts[0] + parts[1] + parts[2]


def _even_in_kernel(x_ref, g_ref, w_ref, seg_ref, qg_ref, kg_ref, p4_ref, p16_ref,
                    a_ref, q0_ref, q1_ref, q2_ref):
    h = _rmsnorm_bf16(x_ref[...], g_ref[...])
    tm = h.shape[0]

    def proj(hh, c):
        return _dot(hh, w_ref[:, c * GROUP_W:(c + 1) * GROUP_W])

    a_ref[...] = proj(h, 0) * jax.nn.sigmoid(proj(h, 1))
    seg = seg_ref[...]
    for g, (d, perm_ref, out_ref) in enumerate(zip(DILATIONS, (None, p4_ref, p16_ref), (q0_ref, q1_ref, q2_ref))):
        if d == 1:
            hg = h
        else:
            hg = jnp.concatenate([_dot(perm_ref[...], h[s:s + PERM]).astype(BF16) for s in range(0, tm, PERM)], axis=0)
        for part in range(3):
            p = proj(hg, 2 + part * N_GROUPS + g)
            if part < 2:
                gain = qg_ref[...] if part == 0 else kg_ref[...]
                ms = _dot((p * p).astype(BF16), seg) * (1.0 / HEAD_DIM)
                p = p * lax.rsqrt(ms + EPS) * gain
            val = p.astype(BF16)
            cols = slice(part * GROUP_W, (part + 1) * GROUP_W)
            if d == 1:
                out_ref[:, cols] = val
            else:
                n = PERM // d
                for s in range(tm // PERM):
                    for r in range(d):
                        out_ref[r, s * n:(s + 1) * n, cols] = val[s * PERM + r * n:s * PERM + (r + 1) * n]


def _even_in(x, g, w, seg, qg, kg, p4, p16):
    batch, seq, _ = x.shape
    outs = [jax.ShapeDtypeStruct((batch, seq, CONV_A_CH), F32)]
    outs += [jax.ShapeDtypeStruct((batch, d, seq // d, GROUP_QKV), BF16) for d in DILATIONS]
    return pl.pallas_call(
        _even_in_kernel,
        grid=(batch, seq // TM),
        in_specs=[_rows(TM, D_MODEL)] + [_resident(a) for a in (g, w, seg, qg, kg, p4, p16)],
        out_specs=[_rows(TM, CONV_A_CH)] + [_classes(d, TM, GROUP_QKV) for d in DILATIONS],
        out_shape=outs,
        compiler_params=_params(2),
        name="even_in",
    )(x, g, w, seg, qg, kg, p4, p16)


def _conv_a_kernel(prev_ref, a_ref, next_ref, w_ref, b_ref, lg_ref, lb_ref, o_ref, buf_ref, *, n_tiles):
    i = pl.program_id(1)
    ts = a_ref.shape[0]
    h = CONV_A_HALO
    buf_ref[0:h, :] = jnp.where(i > 0, prev_ref[...], 0.0)
    buf_ref[h:h + ts, :] = a_ref[...]
    buf_ref[h + ts:2 * h + ts, :] = jnp.where(i < n_tiles - 1, next_ref[...], 0.0)
    pad = (CONV_A_K - 1) // 2
    for r in range(0, ts, CONV_A_RB):
        acc = None
        for k in range(CONV_A_K):
            lo = h + r + k - pad
            term = buf_ref[lo:lo + CONV_A_RB, :] * w_ref[k:k + 1, :]
            acc = term if acc is None else acc + term
        acc = acc + b_ref[...]
        mu = jnp.mean(acc, axis=-1, keepdims=True)
        xc = acc - mu
        var = jnp.mean(xc * xc, axis=-1, keepdims=True)
        y = xc * lax.rsqrt(var + EPS) * lg_ref[...] + lb_ref[...]
        o_ref[r:r + CONV_A_RB, :] = (y * jax.nn.sigmoid(y)).astype(BF16)


def _conv_a(a, w, b, lg, lb):
    batch, seq, _ = a.shape
    ts = CONV_A_TS
    n_tiles = seq // ts
    hb = ts // CONV_A_HALO
    main = _rows(ts, CONV_A_CH)
    prev = pl.BlockSpec((None, CONV_A_HALO, CONV_A_CH), lambda bi, i: (bi, jnp.maximum(i * hb - 1, 0), 0))
    nxt = pl.BlockSpec((None, CONV_A_HALO, CONV_A_CH),
                       lambda bi, i: (bi, jnp.minimum((i + 1) * hb, seq // CONV_A_HALO - 1), 0))
    return pl.pallas_call(
        functools.partial(_conv_a_kernel, n_tiles=n_tiles),
        grid=(batch, n_tiles),
        in_specs=[prev, main, nxt, _resident(w), _resident(b), _resident(lg), _resident(lb)],
        out_specs=main,
        out_shape=jax.ShapeDtypeStruct((batch, seq, CONV_A_CH), BF16),
        scratch_shapes=[pltpu.VMEM((ts + 2 * CONV_A_HALO, CONV_A_CH), F32)],
        compiler_params=_params(2),
        name="conv_a",
    )(a, a, a, w, b, lg, lb)


def _attn_kernel(q_ref, kp_ref, kc_ref, kn_ref, vp_ref, vc_ref, vn_ref, bm_ref, o_ref, l_ref,
                 kw_ref, vw_ref, *, n_tiles):
    i = pl.program_id(2)
    tq = q_ref.shape[0]
    for dst, (p, c, n) in ((kw_ref, (kp_ref, kc_ref, kn_ref)), (vw_ref, (vp_ref, vc_ref, vn_ref))):
        dst[0:HALF, :] = p[...]
        dst[HALF:HALF + tq, :] = c[...]
        dst[HALF + tq:2 * HALF + tq, :] = n[...]
    lane = lax.broadcasted_iota(jnp.int32, (1, LANES), 1)
    first_head = lane < HEAD_DIM
    keep_first = first_head.astype(BF16)
    keep_second = 1.0 - keep_first
    col = lax.broadcasted_iota(jnp.int32, (1, ATT_KEYS), 1)
    n_sub = tq // ATT_SUB
    for s in range(n_sub):
        r0 = s * ATT_SUB
        lo = jnp.where(i == 0, HALF, 0) if s == 0 else None
        hi = jnp.where(i == n_tiles - 1, ATT_KEYS - HALF, ATT_KEYS) if s == n_sub - 1 else None
        edge = None
        if lo is not None or hi is not None:
            ok = col >= (lo if lo is not None else 0)
            if hi is not None:
                ok = ok & (col < hi)
            edge = jnp.where(ok, 0.0, NEG_INF)
        lanes = [slice(pr * LANES, (pr + 1) * LANES) for pr in range(N_PAIRS)]
        scs = []
        for cs in lanes:
            q2 = q_ref[r0:r0 + ATT_SUB, cs]
            qs = jnp.concatenate([q2 * keep_first, q2 * keep_second], axis=0)
            sc = lax.dot_general(qs, kw_ref[r0:r0 + ATT_KEYS, cs], (((1,), (1,)), ((), ())),
                                 preferred_element_type=F32) + bm_ref[len(scs)]
            scs.append(sc if edge is None else sc + edge)
        ms = [jnp.max(sc, axis=-1, keepdims=True) for sc in scs]
        ps = [jnp.exp(sc - m) for sc, m in zip(scs, ms)]
        dens = [jnp.sum(p, axis=-1, keepdims=True) for p in ps]
        pvs = [_dot(p.astype(BF16), vw_ref[r0:r0 + ATT_KEYS, cs]) for p, cs in zip(ps, lanes)]
        lse_rows = jnp.zeros((ATT_SUB, LANES), F32)
        for pr, (cs, pv, m, den) in enumerate(zip(lanes, pvs, ms, dens)):
            o = pv / den
            o_ref[r0:r0 + ATT_SUB, cs] = jnp.where(first_head, o[0:ATT_SUB], o[ATT_SUB:]).astype(BF16)
            lse = m + jnp.log(den)
            lse_rows = jnp.where(lane == 2 * pr, lse[0:ATT_SUB], lse_rows)
            lse_rows = jnp.where(lane == 2 * pr + 1, lse[ATT_SUB:], lse_rows)
        l_ref[r0:r0 + ATT_SUB, :] = lse_rows


def _attention_group(qkv, bm):
    batch, d, cls_len, _ = qkv.shape
    tq = min(cls_len, ATT_STEP)
    n_tiles = cls_len // tq
    hb = tq // HALF

    def main(c, width=GROUP_W):
        return pl.BlockSpec((None, None, tq, width), lambda b, r, i: (b, r, i, c))

    def before(c):
        return pl.BlockSpec((None, None, HALF, GROUP_W), lambda b, r, i: (b, r, jnp.maximum(i * hb - 1, 0), c))

    def after(c):
        return pl.BlockSpec((None, None, HALF, GROUP_W),
                            lambda b, r, i: (b, r, jnp.minimum((i + 1) * hb, cls_len // HALF - 1), c))

    return pl.pallas_call(
        functools.partial(_attn_kernel, n_tiles=n_tiles),
        grid=(batch, d, n_tiles),
        in_specs=[main(0), before(1), main(1), after(1), before(2), main(2), after(2), _resident(bm)],
        out_specs=[main(0), main(0, LANES)],
        out_shape=[jax.ShapeDtypeStruct((batch, d, cls_len, GROUP_W), BF16),
                   jax.ShapeDtypeStruct((batch, d, cls_len, LANES), F32)],
        scratch_shapes=[pltpu.VMEM((tq + 2 * HALF, GROUP_W), BF16), pltpu.VMEM((tq + 2 * HALF, GROUP_W), BF16)],
        compiler_params=_params(3),
        name=f"attn_d{d}",
    )(qkv, qkv, qkv, qkv, qkv, qkv, qkv, bm)


def _token_order(ref, d, perm_t, exact):
    if d == 1:
        return ref[...].astype(F32)
    n = PERM // d
    blocks = []
    for s in range(ref.shape[1] // n):
        cm = jnp.concatenate([ref[r, s * n:(s + 1) * n, :] for r in range(d)], axis=0)
        blocks.append(_dot_exact(cm, perm_t, left=True) if exact else _dot(perm_t, cm))
    return jnp.concatenate(blocks, axis=0)


def _even_out_kernel(x_ref, a_ref, o0_ref, o1_ref, o2_ref, l0_ref, l1_ref, l2_ref, ex_ref, p4t_ref, p16t_ref,
                     w_ref, y_ref):
    perms = (None, p4t_ref[...], p16t_ref[...])
    lses = [_token_order(r, d, pt, True) for r, d, pt in zip((l0_ref, l1_ref, l2_ref), DILATIONS, perms)]
    m = jnp.maximum(jnp.maximum(lses[0], lses[1]), lses[2])
    es = [jnp.exp(l - m) for l in lses]
    inv = 1.0 / (es[0] + es[1] + es[2])
    ex = ex_ref[...]
    o = None
    for e, o_ref, d, pt in zip(es, (o0_ref, o1_ref, o2_ref), DILATIONS, perms):
        term = _dot_exact(e * inv, ex) * _token_order(o_ref, d, pt, False)
        o = term if o is None else o + term
    y = _dot(a_ref[...], w_ref[0:CONV_A_CH, :]) + _dot(o.astype(BF16), w_ref[CONV_A_CH:, :])
    y_ref[...] = x_ref[...] + y


def _even_out(x, a, os, ls, ex, p4t, p16t, w):
    batch, seq, _ = x.shape
    return pl.pallas_call(
        _even_out_kernel,
        grid=(batch, seq // TM),
        in_specs=[_rows(TM, D_MODEL), _rows(TM, CONV_A_CH)]
                 + [_classes(d, TM, GROUP_W) for d in DILATIONS] + [_classes(d, TM, LANES) for d in DILATIONS]
                 + [_resident(ex), _resident(p4t), _resident(p16t), _resident(w)],
        out_specs=_rows(TM, D_MODEL),
        out_shape=jax.ShapeDtypeStruct((batch, seq, D_MODEL), F32),
        compiler_params=_params(2),
        name="even_out",
    )(x, a, *os, *ls, ex, p4t, p16t, w)


def _odd_in_kernel(x_ref, g_ref, w_ref, b_ref, cu_ref):
    h = _rmsnorm_bf16(x_ref[...], g_ref[...])
    b_ref[...] = _dot(h, w_ref[:, 0:D_MODEL])
    cu_ref[...] = _dot(h, w_ref[:, D_MODEL:2 * D_MODEL]) * _dot(h, w_ref[:, 2 * D_MODEL:3 * D_MODEL])


def _odd_in(x, g, w):
    batch, seq, _ = x.shape
    return pl.pallas_call(
        _odd_in_kernel,
        grid=(batch, seq // TM),
        in_specs=[_rows(TM, D_MODEL), _resident(g), _resident(w)],
        out_specs=[_rows(TM, D_MODEL), _rows(TM, D_MODEL)],
        out_shape=[jax.ShapeDtypeStruct((batch, seq, D_MODEL), F32)] * 2,
        compiler_params=_params(2),
        name="odd_in",
    )(x, g, w)


def _odd_out_kernel(x_ref, b_ref, prev_ref, cu_ref, next_ref, cw_ref, w_ref, y_ref, buf_ref, *, n_tiles):
    i = pl.program_id(1)
    ts = cu_ref.shape[0]
    buf_ref[0:8, :] = jnp.where(i > 0, prev_ref[...], 0.0)
    buf_ref[8:8 + ts, :] = cu_ref[...]
    buf_ref[8 + ts:16 + ts, :] = jnp.where(i < n_tiles - 1, next_ref[...], 0.0)
    conv = (buf_ref[7:7 + ts, :] * cw_ref[0:1, :] + buf_ref[8:8 + ts, :] * cw_ref[1:2, :]
            + buf_ref[9:9 + ts, :] * cw_ref[2:3, :])
    y_ref[...] = x_ref[...] + _dot((b_ref[...] * conv).astype(BF16), w_ref[...])


def _odd_out(x, b, cu, cw, w):
    batch, seq, _ = x.shape
    ts = TM
    n_tiles = seq // ts
    hb = ts // 8
    main = _rows(ts, D_MODEL)
    prev = pl.BlockSpec((None, 8, D_MODEL), lambda bi, i: (bi, jnp.maximum(i * hb - 1, 0), 0))
    nxt = pl.BlockSpec((None, 8, D_MODEL), lambda bi, i: (bi, jnp.minimum((i + 1) * hb, seq // 8 - 1), 0))
    return pl.pallas_call(
        functools.partial(_odd_out_kernel, n_tiles=n_tiles),
        grid=(batch, n_tiles),
        in_specs=[main, main, prev, main, nxt, _resident(cw), _resident(w)],
        out_specs=main,
        out_shape=jax.ShapeDtypeStruct((batch, seq, D_MODEL), F32),
        scratch_shapes=[pltpu.VMEM((ts + 16, D_MODEL), F32)],
        compiler_params=_params(2),
        name="odd_out",
    )(x, b, cu, cu, cu, cw, w)


def _ffn_kernel(x_ref, g_ref, w1_ref, w3_ref, w2_ref, y_ref):
    x = x_ref[...]
    h = _rmsnorm_bf16(x, g_ref[...])
    y = x
    for c in range(0, FFN_HIDDEN, FFN_CHUNK):
        a = _dot(h, w1_ref[:, c:c + FFN_CHUNK])
        b = _dot(h, w3_ref[:, c:c + FFN_CHUNK])
        y = y + _dot((a * jax.nn.sigmoid(a) * b).astype(BF16), w2_ref[c:c + FFN_CHUNK, :])
    y_ref[...] = y


def _ffn(x, g, w1, w3, w2):
    batch, seq, _ = x.shape
    return pl.pallas_call(
        _ffn_kernel,
        grid=(batch, seq // TM),
        in_specs=[_rows(TM, D_MODEL), _resident(g), _resident(w1), _resident(w3), _resident(w2)],
        out_specs=_rows(TM, D_MODEL),
        out_shape=jax.ShapeDtypeStruct((batch, seq, D_MODEL), F32),
        compiler_params=_params(2),
        name="ffn",
    )(x, g, w1, w3, w2)


def _t5_bucket_np(rel):
    nb = N_BUCKETS // 2
    max_exact = nb // 2
    n = np.abs(rel)
    nf = np.maximum(n, 1).astype(np.float32)
    large = max_exact + (np.log(nf / max_exact) / math.log(MAX_DISTANCE / max_exact) * (nb - max_exact)).astype(np.int32)
    large = np.minimum(large, nb - 1)
    return np.where(rel > 0, nb, 0) + np.where(n < max_exact, n, large)


def _bias_windows(rel_bias):
    n_off = ATT_SUB + ATT_KEYS - 1
    off = np.arange(n_off) - (ATT_SUB - 1) - HALF
    band = np.abs(off) <= HALF
    out = []
    for g, d in enumerate(DILATIONS):
        tab = rel_bias[:, g * HEADS_PER_GROUP:(g + 1) * HEADS_PER_GROUP].astype(F32)
        vec = jnp.take(tab, jnp.asarray(_t5_bucket_np(off * d)), axis=0)
        vec = jnp.where(band[:, None], vec, NEG_INF).T
        flat = jnp.tile(jnp.pad(vec, ((0, 0), (0, 1))), (1, ATT_SUB))[:, :ATT_SUB * n_off]
        skew = flat.reshape(HEADS_PER_GROUP, ATT_SUB, n_off)
        win = skew[:, :, ATT_SUB - 1:ATT_SUB - 1 + ATT_KEYS]
        out.append(win.reshape(N_PAIRS, 2 * ATT_SUB, ATT_KEYS))
    return out


def _head_segments():
    h = np.arange(GROUP_W) // HEAD_DIM
    return jnp.asarray(h[:, None] == h[None, :], BF16)


def _head_expand():
    return jnp.asarray(np.arange(LANES)[:, None] == (np.arange(GROUP_W) // HEAD_DIM)[None, :], BF16)


def _class_perm(d):
    n = PERM // d
    src = (np.arange(PERM) % n) * d + np.arange(PERM) // n
    return np.arange(PERM)[None, :] == src[:, None]


def _trunk(x, p):
    for layer in range(DEPTH):
        i = layer // 2
        if layer % 2 == 0:
            a, *qkvs = _even_in(x, p["even_norm"][i], p["even_w_in"][i], p["seg"], p["q_gain"][i], p["k_gain"][i],
                                p["perm"][1], p["perm"][2])
            a = _conv_a(a, p["conv_a_w"][i], p["conv_a_b"][i], p["conv_a_ln_g"][i], p["conv_a_ln_b"][i])
            os, ls = zip(*[_attention_group(qkv, bm) for qkv, bm in zip(qkvs, p["bias"])])
            x = _even_out(x, a, os, ls, p["expand"], p["perm_t"][1], p["perm_t"][2], p["even_w_out"][i])
        else:
            b, cu = _odd_in(x, p["odd_norm"][i], p["odd_w_in"][i])
            x = _odd_out(x, b, cu, p["conv_c_w"][i], p["odd_w_out"][i])
        x = _ffn(x, p["ffn_norm"][layer], p["ffn_w1"][layer], p["ffn_w3"][layer], p["ffn_w2"][layer])
    return x


def _prepare(rel_bias, even_norm, even_w_in, conv_a_w, conv_a_b, conv_a_ln_g, conv_a_ln_b, q_norm, k_norm,
             even_w_out, odd_norm, odd_w_in, conv_c_w, odd_w_out, ffn_norm, ffn_w1, ffn_w3, ffn_w2):
    row = lambda a: a.astype(F32)[:, None, :]
    tile_heads = lambda a: jnp.tile(a.astype(F32), (1, HEADS_PER_GROUP))[:, None, :]
    perms = [None if d == 1 else _class_perm(d) for d in DILATIONS]
    return dict(
        bias=_bias_windows(rel_bias), seg=_head_segments(), expand=_head_expand(),
        perm=[None if m is None else jnp.asarray(m, BF16) for m in perms],
        perm_t=[None if m is None else jnp.asarray(m.T, BF16) for m in perms],
        even_norm=row(even_norm), even_w_in=even_w_in.astype(BF16),
        conv_a_w=conv_a_w.astype(F32), conv_a_b=row(conv_a_b), conv_a_ln_g=row(conv_a_ln_g), conv_a_ln_b=row(conv_a_ln_b),
        q_gain=tile_heads(q_norm) * (HEAD_DIM ** -0.5), k_gain=tile_heads(k_norm),
        even_w_out=even_w_out.astype(BF16),
        odd_norm=row(odd_norm), odd_w_in=odd_w_in.astype(BF16), conv_c_w=conv_c_w.astype(F32),
        odd_w_out=odd_w_out.astype(BF16),
        ffn_norm=row(ffn_norm), ffn_w1=ffn_w1.astype(BF16), ffn_w3=ffn_w3.astype(BF16), ffn_w2=ffn_w2.astype(BF16),
    )


def kernel(x_prompt, x_sample, rel_bias, even_norm, even_w_in, conv_a_w, conv_a_b, conv_a_ln_g, conv_a_ln_b, q_norm, k_norm, even_w_out, odd_norm, odd_w_in, conv_c_w, odd_w_out, ffn_norm, ffn_w1, ffn_w3, ffn_w2):
    p = _prepare(rel_bias, even_norm, even_w_in, conv_a_w, conv_a_b, conv_a_ln_g, conv_a_ln_b, q_norm, k_norm,
                 even_w_out, odd_norm, odd_w_in, conv_c_w, odd_w_out, ffn_norm, ffn_w1, ffn_w3, ffn_w2)
    return (_trunk(x_prompt, p), _trunk(x_sample, p))
```

```python
import functools
import math

import numpy as np
import jax
import jax.numpy as jnp
from jax import lax
from jax.experimental import pallas as pl
from jax.experimental.pallas import tpu as pltpu

F32 = jnp.float32
BF16 = jnp.bfloat16

D_MODEL = 1024
DEPTH = 4
HEAD_DIM = 64
WINDOWS = (128, 512, 2048)
DILATIONS = (1, 4, 16)
N_GROUPS = 3
HEADS_PER_GROUP = 8
GROUP_W = HEADS_PER_GROUP * HEAD_DIM
GROUP_QKV = 3 * GROUP_W
CONV_A_CH = D_MODEL // 2
CONV_A_K = 31
FFN_HIDDEN = 2816
N_BUCKETS = 32
MAX_DISTANCE = 1024
EPS = 1e-6
NEG_INF = -1e30
HALF = 64
assert all(w // (2 * d) == HALF for w, d in zip(WINDOWS, DILATIONS))

V7X_VMEM_BYTES = 64 * 1024 * 1024
VMEM_LIMIT_BYTES = V7X_VMEM_BYTES * 7 // 8
LANES = 128
SUBLANES = 8
MXU_DIM = 256

TM = 512
PERM = MXU_DIM
SEG_W = MXU_DIM
LSE_PARTS = 3
ATT_SUB = 128
ATT_KEYS = ATT_SUB + 2 * HALF
ATT_STEP = 512
N_PAIRS = GROUP_W // LANES
CONV_A_TS = 256
CONV_A_RB = 32
CONV_A_HALO = 16
FFN_CHUNKS = ((0, 1536), (1536, 2816))


def _params(n_axes):
    return pltpu.CompilerParams(dimension_semantics=("parallel",) * n_axes,
                                vmem_limit_bytes=VMEM_LIMIT_BYTES)


def _resident(a):
    nd = a.ndim
    return pl.BlockSpec(a.shape, lambda *_: (0,) * nd, pipeline_mode=pl.Buffered(1))


def _rows(tm, width):
    return pl.BlockSpec((None, tm, width), lambda b, i: (b, i, 0))


def _classes(d, tm, width):
    if d == 1:
        return pl.BlockSpec((None, None, tm, width), lambda b, i: (b, 0, i, 0))
    return pl.BlockSpec((None, d, tm // d, width), lambda b, i: (b, 0, i, 0))


def _rmsnorm_bf16(x, g):
    ms = jnp.mean(x * x, axis=-1, keepdims=True)
    return (x * lax.rsqrt(ms + EPS) * g).astype(BF16)


def _dot(a, b):
    return jnp.dot(a, b, preferred_element_type=F32)


def _even_in_kernel(x_ref, g_ref, w_ref, seg_ref, qg_ref, kg_ref, p4_ref, p16_ref,
                    a_ref, q0_ref, q1_ref, q2_ref):
    h = _rmsnorm_bf16(x_ref[...], g_ref[...])
    tm = h.shape[0]

    def proj(hh, c):
        return _dot(hh, w_ref[:, c * GROUP_W:(c + 1) * GROUP_W])

    a_ref[...] = proj(h, 0) * jax.nn.sigmoid(proj(h, 1))
    seg = seg_ref[...]
    for g, (d, perm_ref, out_ref) in enumerate(zip(DILATIONS, (None, p4_ref, p16_ref), (q0_ref, q1_ref, q2_ref))):
        if d == 1:
            hg = h
        else:
            hg = jnp.concatenate([_dot(perm_ref[...], h[s:s + PERM]).astype(BF16) for s in range(0, tm, PERM)], axis=0)
        for part in range(3):
            p = proj(hg, 2 + part * N_GROUPS + g)
            if part < 2:
                gain = qg_ref[...] if part == 0 else kg_ref[...]
                sq = (p * p).astype(BF16)
                ms = jnp.concatenate([_dot(sq[:, c:c + SEG_W], seg) for c in range(0, GROUP_W, SEG_W)], axis=1)
                ms = ms * (1.0 / HEAD_DIM)
                p = p * lax.rsqrt(ms + EPS) * gain
            val = p.astype(BF16)
            cols = slice(part * GROUP_W, (part + 1) * GROUP_W)
            if d == 1:
                out_ref[:, cols] = val
            else:
                n = PERM // d
                for s in range(tm // PERM):
                    for r in range(d):
                        out_ref[r, s * n:(s + 1) * n, cols] = val[s * PERM + r * n:s * PERM + (r + 1) * n]


def _even_in(x, g, w, seg, qg, kg, p4, p16):
    batch, seq, _ = x.shape
    outs = [jax.ShapeDtypeStruct((batch, seq, CONV_A_CH), F32)]
    outs += [jax.ShapeDtypeStruct((batch, d, seq // d, GROUP_QKV), BF16) for d in DILATIONS]
    return pl.pallas_call(
        _even_in_kernel,
        grid=(batch, seq // TM),
        in_specs=[_rows(TM, D_MODEL)] + [_resident(a) for a in (g, w, seg, qg, kg, p4, p16)],
        out_specs=[_rows(TM, CONV_A_CH)] + [_classes(d, TM, GROUP_QKV) for d in DILATIONS],
        out_shape=outs,
        compiler_params=_params(2),
        name="even_in",
    )(x, g, w, seg, qg, kg, p4, p16)


def _conv_a_kernel(prev_ref, a_ref, next_ref, wb_ref, b_ref, lg_ref, lb_ref, o_ref, buf_ref, sh_ref, *, n_tiles):
    i = pl.program_id(1)
    ts = a_ref.shape[0]
    h = CONV_A_HALO
    rows = ts + 2 * h
    buf_ref[0:h, :] = jnp.where(i > 0, prev_ref[...], 0.0)
    buf_ref[h:h + ts, :] = a_ref[...]
    buf_ref[h + ts:rows, :] = jnp.where(i < n_tiles - 1, next_ref[...], 0.0)
    n_sh = rows - SUBLANES
    for v in range(SUBLANES):
        sh_ref[v] = buf_ref[v:v + n_sh, :].reshape(n_sh // SUBLANES, SUBLANES, CONV_A_CH)
    off = h - (CONV_A_K - 1) // 2
    nb = CONV_A_RB // SUBLANES
    for r in range(0, ts, CONV_A_RB):
        t0 = r // SUBLANES
        acc = None
        for v in range(SUBLANES):
            us = [u for u in range((CONV_A_K + off) // SUBLANES + 1) if 0 <= SUBLANES * u + v - off < CONV_A_K]
            x = sh_ref[v, t0 + us[0]:t0 + us[-1] + nb]
            for u in us:
                term = x[u - us[0]:u - us[0] + nb] * wb_ref[SUBLANES * u + v - off]
                acc = term if acc is None else acc + term
        acc = acc.reshape(CONV_A_RB, CONV_A_CH) + b_ref[...]
        mu = jnp.mean(acc, axis=-1, keepdims=True)
        xc = acc - mu
        var = jnp.mean(xc * xc, axis=-1, keepdims=True)
        y = xc * lax.rsqrt(var + EPS) * lg_ref[...] + lb_ref[...]
        o_ref[r:r + CONV_A_RB, :] = (y * jax.nn.sigmoid(y)).astype(BF16)


def _conv_a(a, wb, b, lg, lb):
    batch, seq, _ = a.shape
    ts = CONV_A_TS
    n_tiles = seq // ts
    hb = ts // CONV_A_HALO
    rows = ts + 2 * CONV_A_HALO
    main = _rows(ts, CONV_A_CH)
    prev = pl.BlockSpec((None, CONV_A_HALO, CONV_A_CH), lambda bi, i: (bi, jnp.maximum(i * hb - 1, 0), 0))
    nxt = pl.BlockSpec((None, CONV_A_HALO, CONV_A_CH),
                       lambda bi, i: (bi, jnp.minimum((i + 1) * hb, seq // CONV_A_HALO - 1), 0))
    return pl.pallas_call(
        functools.partial(_conv_a_kernel, n_tiles=n_tiles),
        grid=(batch, n_tiles),
        in_specs=[prev, main, nxt, _resident(wb), _resident(b), _resident(lg), _resident(lb)],
        out_specs=main,
        out_shape=jax.ShapeDtypeStruct((batch, seq, CONV_A_CH), BF16),
        scratch_shapes=[pltpu.VMEM((rows, CONV_A_CH), F32),
                        pltpu.VMEM((SUBLANES, rows // SUBLANES - 1, SUBLANES, CONV_A_CH), F32)],
        compiler_params=_params(2),
        name="conv_a",
    )(a, a, a, wb, b, lg, lb)


def _attn_kernel(*refs, n_tiles, halo):
    if halo:
        q_ref, kp_ref, kc_ref, kn_ref, vp_ref, vc_ref, vn_ref, bm_ref, o_ref, l_ref, kw_ref, vw_ref = refs
        edges = ((kw_ref, kp_ref, kn_ref), (vw_ref, vp_ref, vn_ref))
    else:
        q_ref, kc_ref, vc_ref, bm_ref, o_ref, l_ref, kw_ref, vw_ref = refs
        edges = ((kw_ref, None, None), (vw_ref, None, None))
    i = pl.program_id(2)
    n_cls, tq = q_ref.shape[0], q_ref.shape[1]
    lane = lax.broadcasted_iota(jnp.int32, (1, LANES), 1)
    first_head = lane < HEAD_DIM
    keep_first = first_head.astype(BF16)
    keep_second = 1.0 - keep_first
    col = lax.broadcasted_iota(jnp.int32, (1, ATT_KEYS), 1)
    n_sub = tq // ATT_SUB
    lanes = [slice(pr * LANES, (pr + 1) * LANES) for pr in range(N_PAIRS)]
    for c in range(n_cls):
        for (dst, before, after), cur in zip(edges, (kc_ref, vc_ref)):
            pad = jnp.zeros((HALF, GROUP_W), BF16)
            dst[c, 0:HALF, :] = pad if before is None else before[c]
            dst[c, HALF:HALF + tq, :] = cur[c]
            dst[c, HALF + tq:2 * HALF + tq, :] = pad if after is None else after[c]
        for s in range(n_sub):
            r0 = s * ATT_SUB
            lo = jnp.where(i == 0, HALF, 0) if s == 0 else None
            hi = jnp.where(i == n_tiles - 1, ATT_KEYS - HALF, ATT_KEYS) if s == n_sub - 1 else None
            edge = None
            if lo is not None or hi is not None:
                ok = col >= (lo if lo is not None else 0)
                if hi is not None:
                    ok = ok & (col < hi)
                edge = jnp.where(ok, 0.0, NEG_INF)
            scs = []
            for pr, cs in enumerate(lanes):
                q2 = q_ref[c, r0:r0 + ATT_SUB, cs]
                qs = jnp.concatenate([q2 * keep_first, q2 * keep_second], axis=0)
                sc = lax.dot_general(qs, kw_ref[c, r0:r0 + ATT_KEYS, cs], (((1,), (1,)), ((), ())),
                                     preferred_element_type=F32) + bm_ref[pr]
                scs.append(sc if edge is None else sc + edge)
            ms = [jnp.max(sc, axis=-1, keepdims=True) for sc in scs]
            ps = [jnp.exp(sc - m) for sc, m in zip(scs, ms)]
            dens = [jnp.sum(p, axis=-1, keepdims=True) for p in ps]
            pvs = [_dot(p.astype(BF16), vw_ref[c, r0:r0 + ATT_KEYS, cs]) for p, cs in zip(ps, lanes)]
            lse_rows = jnp.zeros((ATT_SUB, LANES), F32)
            for pr, (cs, pv, m, den) in enumerate(zip(lanes, pvs, ms, dens)):
                o = pv / den
                o_ref[c, r0:r0 + ATT_SUB, cs] = jnp.where(first_head, o[0:ATT_SUB], o[ATT_SUB:]).astype(BF16)
                lse = m + jnp.log(den)
                lse_rows = jnp.where(lane == 2 * pr, lse[0:ATT_SUB], lse_rows)
                lse_rows = jnp.where(lane == 2 * pr + 1, lse[ATT_SUB:], lse_rows)
            packed, rest = None, lse_rows
            for t in range(LSE_PARTS):
                part = rest.astype(BF16).astype(F32)
                rest = rest - part
                part = part if t == 0 else pltpu.roll(part, t * HEADS_PER_GROUP, 1)
                packed = part if packed is None else packed + part
            l_ref[c, r0:r0 + ATT_SUB, :] = packed.astype(BF16)


def _attention_group(qkv, bm):
    batch, d, cls_len, _ = qkv.shape
    tq = min(cls_len, ATT_STEP)
    n_tiles = cls_len // tq
    halo = n_tiles > 1
    n_cls = 1 if halo else min(d, ATT_STEP // tq)
    hb = tq // HALF

    def main(c, width=GROUP_W):
        return pl.BlockSpec((None, n_cls, tq, width), lambda b, r, i: (b, r, i, c))

    def before(c):
        return pl.BlockSpec((None, 1, HALF, GROUP_W), lambda b, r, i: (b, r, jnp.maximum(i * hb - 1, 0), c))

    def after(c):
        return pl.BlockSpec((None, 1, HALF, GROUP_W),
                            lambda b, r, i: (b, r, jnp.minimum((i + 1) * hb, cls_len // HALF - 1), c))

    if halo:
        in_specs = [main(0), before(1), main(1), after(1), before(2), main(2), after(2), _resident(bm)]
    else:
        in_specs = [main(0), main(1), main(2), _resident(bm)]
    window = pltpu.VMEM((n_cls, tq + 2 * HALF, GROUP_W), BF16)
    return pl.pallas_call(
        functools.partial(_attn_kernel, n_tiles=n_tiles, halo=halo),
        grid=(batch, d // n_cls, n_tiles),
        in_specs=in_specs,
        out_specs=[main(0), main(0, LANES)],
        out_shape=[jax.ShapeDtypeStruct((batch, d, cls_len, GROUP_W), BF16),
                   jax.ShapeDtypeStruct((batch, d, cls_len, LANES), BF16)],
        scratch_shapes=[window, window],
        compiler_params=_params(3),
        name=f"attn_d{d}",
    )(*([qkv] * (len(in_specs) - 1)), bm)


def _token_order(ref, d, perm_t):
    if d == 1:
        return ref[...].astype(F32)
    n = PERM // d
    blocks = []
    for s in range(ref.shape[1] // n):
        cm = jnp.concatenate([ref[r, s * n:(s + 1) * n, :] for r in range(d)], axis=0)
        blocks.append(_dot(perm_t, cm))
    return jnp.concatenate(blocks, axis=0)


def _even_out_kernel(x_ref, a_ref, o0_ref, o1_ref, o2_ref, l0_ref, l1_ref, l2_ref, p4t_ref, p16t_ref,
                     w_ref, y_ref):
    perms = (None, p4t_ref[...], p16t_ref[...])
    lses = []
    for l_ref, d, pt in zip((l0_ref, l1_ref, l2_ref), DILATIONS, perms):
        parts = _token_order(l_ref, d, pt)
        lse = parts
        for t in range(1, LSE_PARTS):
            lse = lse + pltpu.roll(parts, LANES - t * HEADS_PER_GROUP, 1)
        lses.append(lse)
    m = jnp.maximum(jnp.maximum(lses[0], lses[1]), lses[2])
    es = [jnp.exp(l - m) for l in lses]
    inv = 1.0 / (es[0] + es[1] + es[2])
    alphas = [e * inv for e in es]
    os = [_token_order(r, d, pt) for r, d, pt in zip((o0_ref, o1_ref, o2_ref), DILATIONS, perms)]
    first_head = lax.broadcasted_iota(jnp.int32, (1, LANES), 1) < HEAD_DIM
    blocks = []
    for pr in range(N_PAIRS):
        cs = slice(pr * LANES, (pr + 1) * LANES)
        acc = None
        for al, og in zip(alphas, os):
            wgt = jnp.where(first_head, al[:, 2 * pr:2 * pr + 1], al[:, 2 * pr + 1:2 * pr + 2])
            acc = wgt * og[:, cs] if acc is None else acc + wgt * og[:, cs]
        blocks.append(acc.astype(BF16))
    o = jnp.concatenate(blocks, axis=1)
    y = _dot(a_ref[...], w_ref[0:CONV_A_CH, :]) + _dot(o, w_ref[CONV_A_CH:, :])
    y_ref[...] = x_ref[...] + y


def _even_out(x, a, os, ls, p4t, p16t, w):
    batch, seq, _ = x.shape
    return pl.pallas_call(
        _even_out_kernel,
        grid=(batch, seq // TM),
        in_specs=[_rows(TM, D_MODEL), _rows(TM, CONV_A_CH)]
                 + [_classes(d, TM, GROUP_W) for d in DILATIONS] + [_classes(d, TM, LANES) for d in DILATIONS]
                 + [_resident(p4t), _resident(p16t), _resident(w)],
        out_specs=_rows(TM, D_MODEL),
        out_shape=jax.ShapeDtypeStruct((batch, seq, D_MODEL), F32),
        compiler_params=_params(2),
        name="even_out",
    )(x, a, *os, *ls, p4t, p16t, w)


def _odd_in_kernel(x_ref, g_ref, w_ref, b_ref, cu_ref):
    h = _rmsnorm_bf16(x_ref[...], g_ref[...])
    b_ref[...] = _dot(h, w_ref[:, 0:D_MODEL])
    cu_ref[...] = _dot(h, w_ref[:, D_MODEL:2 * D_MODEL]) * _dot(h, w_ref[:, 2 * D_MODEL:3 * D_MODEL])


def _odd_in(x, g, w):
    batch, seq, _ = x.shape
    return pl.pallas_call(
        _odd_in_kernel,
        grid=(batch, seq // TM),
        in_specs=[_rows(TM, D_MODEL), _resident(g), _resident(w)],
        out_specs=[_rows(TM, D_MODEL), _rows(TM, D_MODEL)],
        out_shape=[jax.ShapeDtypeStruct((batch, seq, D_MODEL), F32)] * 2,
        compiler_params=_params(2),
        name="odd_in",
    )(x, g, w)


def _odd_out_kernel(x_ref, b_ref, prev_ref, cu_ref, next_ref, cw_ref, w_ref, y_ref, buf_ref, *, n_tiles):
    i = pl.program_id(1)
    ts = cu_ref.shape[0]
    buf_ref[0:8, :] = jnp.where(i > 0, prev_ref[...], 0.0)
    buf_ref[8:8 + ts, :] = cu_ref[...]
    buf_ref[8 + ts:16 + ts, :] = jnp.where(i < n_tiles - 1, next_ref[...], 0.0)
    conv = (buf_ref[7:7 + ts, :] * cw_ref[0:1, :] + buf_ref[8:8 + ts, :] * cw_ref[1:2, :]
            + buf_ref[9:9 + ts, :] * cw_ref[2:3, :])
    y_ref[...] = x_ref[...] + _dot((b_ref[...] * conv).astype(BF16), w_ref[...])


def _odd_out(x, b, cu, cw, w):
    batch, seq, _ = x.shape
    ts = TM
    n_tiles = seq // ts
    hb = ts // 8
    main = _rows(ts, D_MODEL)
    prev = pl.BlockSpec((None, 8, D_MODEL), lambda bi, i: (bi, jnp.maximum(i * hb - 1, 0), 0))
    nxt = pl.BlockSpec((None, 8, D_MODEL), lambda bi, i: (bi, jnp.minimum((i + 1) * hb, seq // 8 - 1), 0))
    return pl.pallas_call(
        functools.partial(_odd_out_kernel, n_tiles=n_tiles),
        grid=(batch, n_tiles),
        in_specs=[main, main, prev, main, nxt, _resident(cw), _resident(w)],
        out_specs=main,
        out_shape=jax.ShapeDtypeStruct((batch, seq, D_MODEL), F32),
        scratch_shapes=[pltpu.VMEM((ts + 16, D_MODEL), F32)],
        compiler_params=_params(2),
        name="odd_out",
    )(x, b, cu, cu, cu, cw, w)


def _ffn_kernel(x_ref, g_ref, w1_ref, w3_ref, w2_ref, y_ref):
    x = x_ref[...]
    h = _rmsnorm_bf16(x, g_ref[...])
    y = x
    for lo, hi in FFN_CHUNKS:
        a = _dot(h, w1_ref[:, lo:hi])
        b = _dot(h, w3_ref[:, lo:hi])
        y = y + _dot((a * jax.nn.sigmoid(a) * b).astype(BF16), w2_ref[lo:hi, :])
    y_ref[...] = y


def _ffn(x, g, w1, w3, w2):
    batch, seq, _ = x.shape
    return pl.pallas_call(
        _ffn_kernel,
        grid=(batch, seq // TM),
        in_specs=[_rows(TM, D_MODEL), _resident(g), _resident(w1), _resident(w3), _resident(w2)],
        out_specs=_rows(TM, D_MODEL),
        out_shape=jax.ShapeDtypeStruct((batch, seq, D_MODEL), F32),
        compiler_params=_params(2),
        name="ffn",
    )(x, g, w1, w3, w2)


def _t5_bucket_np(rel):
    nb = N_BUCKETS // 2
    max_exact = nb // 2
    n = np.abs(rel)
    nf = np.maximum(n, 1).astype(np.float32)
    large = max_exact + (np.log(nf / max_exact) / math.log(MAX_DISTANCE / max_exact) * (nb - max_exact)).astype(np.int32)
    large = np.minimum(large, nb - 1)
    return np.where(rel > 0, nb, 0) + np.where(n < max_exact, n, large)


def _bias_windows(rel_bias):
    n_off = ATT_SUB + ATT_KEYS - 1
    off = np.arange(n_off) - (ATT_SUB - 1) - HALF
    band = np.abs(off) <= HALF
    out = []
    for g, d in enumerate(DILATIONS):
        tab = rel_bias[:, g * HEADS_PER_GROUP:(g + 1) * HEADS_PER_GROUP].astype(F32)
        vec = jnp.take(tab, jnp.asarray(_t5_bucket_np(off * d)), axis=0)
        vec = jnp.where(band[:, None], vec, NEG_INF).T
        flat = jnp.tile(jnp.pad(vec, ((0, 0), (0, 1))), (1, ATT_SUB))[:, :ATT_SUB * n_off]
        skew = flat.reshape(HEADS_PER_GROUP, ATT_SUB, n_off)
        win = skew[:, :, ATT_SUB - 1:ATT_SUB - 1 + ATT_KEYS]
        out.append(win.reshape(N_PAIRS, 2 * ATT_SUB, ATT_KEYS))
    return out


def _head_segments():
    h = np.arange(SEG_W) // HEAD_DIM
    return jnp.asarray(h[:, None] == h[None, :], BF16)


def _class_perm(d):
    n = PERM // d
    src = (np.arange(PERM) % n) * d + np.arange(PERM) // n
    return np.arange(PERM)[None, :] == src[:, None]


def _trunk(x, p):
    for layer in range(DEPTH):
        i = layer // 2
        if layer % 2 == 0:
            a, *qkvs = _even_in(x, p["even_norm"][i], p["even_w_in"][i], p["seg"], p["q_gain"][i], p["k_gain"][i],
                                p["perm"][1], p["perm"][2])
            a = _conv_a(a, p["conv_a_w"][i], p["conv_a_b"][i], p["conv_a_ln_g"][i], p["conv_a_ln_b"][i])
            os, ls = zip(*[_attention_group(qkv, bm) for qkv, bm in zip(qkvs, p["bias"])])
            x = _even_out(x, a, os, ls, p["perm_t"][1], p["perm_t"][2], p["even_w_out"][i])
        else:
            b, cu = _odd_in(x, p["odd_norm"][i], p["odd_w_in"][i])
            x = _odd_out(x, b, cu, p["conv_c_w"][i], p["odd_w_out"][i])
        x = _ffn(x, p["ffn_norm"][layer], p["ffn_w1"][layer], p["ffn_w3"][layer], p["ffn_w2"][layer])
    return x


def _prepare(rel_bias, even_norm, even_w_in, conv_a_w, conv_a_b, conv_a_ln_g, conv_a_ln_b, q_norm, k_norm,
             even_w_out, odd_norm, odd_w_in, conv_c_w, odd_w_out, ffn_norm, ffn_w1, ffn_w3, ffn_w2):
    row = lambda a: a.astype(F32)[:, None, :]
    tile_heads = lambda a: jnp.tile(a.astype(F32), (1, HEADS_PER_GROUP))[:, None, :]
    perms = [None if d == 1 else _class_perm(d) for d in DILATIONS]
    return dict(
        bias=_bias_windows(rel_bias), seg=_head_segments(),
        perm=[None if m is None else jnp.asarray(m, BF16) for m in perms],
        perm_t=[None if m is None else jnp.asarray(m.T, BF16) for m in perms],
        even_norm=row(even_norm), even_w_in=even_w_in.astype(BF16),
        conv_a_w=jnp.broadcast_to(conv_a_w.astype(F32)[:, :, None, :], conv_a_w.shape[:2] + (SUBLANES, CONV_A_CH)),
        conv_a_b=row(conv_a_b), conv_a_ln_g=row(conv_a_ln_g), conv_a_ln_b=row(conv_a_ln_b),
        q_gain=tile_heads(q_norm) * (HEAD_DIM ** -0.5), k_gain=tile_heads(k_norm),
        even_w_out=even_w_out.astype(BF16),
        odd_norm=row(odd_norm), odd_w_in=odd_w_in.astype(BF16), conv_c_w=conv_c_w.astype(F32),
        odd_w_out=odd_w_out.astype(BF16),
        ffn_norm=row(ffn_norm), ffn_w1=ffn_w1.astype(BF16), ffn_w3=ffn_w3.astype(BF16), ffn_w2=ffn_w2.astype(BF16),
    )


def kernel(x_prompt, x_sample, rel_bias, even_norm, even_w_in, conv_a_w, conv_a_b, conv_a_ln_g, conv_a_ln_b, q_norm, k_norm, even_w_out, odd_norm, odd_w_in, conv_c_w, odd_w_out, ffn_norm, ffn_w1, ffn_w3, ffn_w2):
    p = _prepare(rel_bias, even_norm, even_w_in, conv_a_w, conv_a_b, conv_a_ln_g, conv_a_ln_b, q_norm, k_norm,
                 even_w_out, odd_norm, odd_w_in, conv_c_w, odd_w_out, ffn_norm, ffn_w1, ffn_w3, ffn_w2)
    return (_trunk(x_prompt, p), _trunk(x_sample, p))
```

```python
import functools
import math

import numpy as np
import jax
import jax.numpy as jnp
from jax import lax
from jax.experimental import pallas as pl
from jax.experimental.pallas import tpu as pltpu

F32 = jnp.float32
BF16 = jnp.bfloat16

D_MODEL = 1024
DEPTH = 4
HEAD_DIM = 64
WINDOWS = (128, 512, 2048)
DILATIONS = (1, 4, 16)
N_GROUPS = 3
HEADS_PER_GROUP = 8
GROUP_W = HEADS_PER_GROUP * HEAD_DIM
GROUP_QKV = 3 * GROUP_W
CONV_A_CH = D_MODEL // 2
CONV_A_K = 31
FFN_HIDDEN = 2816
N_BUCKETS = 32
MAX_DISTANCE = 1024
EPS = 1e-6
NEG_INF = -1e30
LOG2E = math.log2(math.e)
LN2 = math.log(2.0)
HALF = 64
assert all(w // (2 * d) == HALF for w, d in zip(WINDOWS, DILATIONS))

V7X_VMEM_BYTES = 64 * 1024 * 1024
VMEM_LIMIT_BYTES = V7X_VMEM_BYTES * 7 // 8
LANES = 128
SUBLANES = 8
MXU_DIM = 256

TM = 512
PERM = MXU_DIM
SEG_W = MXU_DIM
LSE_PARTS = 3
ATT_SUB = 128
ATT_KEYS = ATT_SUB + 2 * HALF
ATT_STEP = 512
N_PAIRS = GROUP_W // LANES
CONV_A_TS = 256
CONV_A_RB = 32
CONV_A_HALO = 16
FFN_CHUNKS = ((0, 1536), (1536, 2816))
PACK_W = N_GROUPS * HEADS_PER_GROUP


def _params(n_axes):
    return pltpu.CompilerParams(dimension_semantics=("parallel",) * n_axes,
                                vmem_limit_bytes=VMEM_LIMIT_BYTES)


def _resident(a):
    nd = a.ndim
    return pl.BlockSpec(a.shape, lambda *_: (0,) * nd, pipeline_mode=pl.Buffered(1))


def _rows(tm, width):
    return pl.BlockSpec((None, tm, width), lambda b, i: (b, i, 0))


def _classes(d, tm, width):
    if d == 1:
        return pl.BlockSpec((None, None, tm, width), lambda b, i: (b, 0, i, 0))
    return pl.BlockSpec((None, d, tm // d, width), lambda b, i: (b, 0, i, 0))


def _rmsnorm_bf16(x, g):
    ms = jnp.mean(x * x, axis=-1, keepdims=True)
    return (x * lax.rsqrt(ms + EPS) * g).astype(BF16)


def _dot(a, b):
    return jnp.dot(a, b, preferred_element_type=F32)


def _even_in_kernel(x_ref, g_ref, w_ref, seg_ref, qg_ref, kg_ref, p4_ref, p16_ref,
                    a_ref, q0_ref, q1_ref, q2_ref):
    h = _rmsnorm_bf16(x_ref[...], g_ref[...])
    tm = h.shape[0]

    def proj(hh, c):
        return _dot(hh, w_ref[:, c * GROUP_W:(c + 1) * GROUP_W])

    a_ref[...] = proj(h, 0) * jax.nn.sigmoid(proj(h, 1))
    seg = seg_ref[...]
    for g, (d, perm_ref, out_ref) in enumerate(zip(DILATIONS, (None, p4_ref, p16_ref), (q0_ref, q1_ref, q2_ref))):
        if d == 1:
            hg = h
        else:
            hg = jnp.concatenate([_dot(perm_ref[...], h[s:s + PERM]).astype(BF16) for s in range(0, tm, PERM)], axis=0)
        for part in range(3):
            p = proj(hg, 2 + part * N_GROUPS + g)
            if part < 2:
                gain = qg_ref[...] if part == 0 else kg_ref[...]
                sq = (p * p).astype(BF16)
                ms = jnp.concatenate([_dot(sq[:, c:c + SEG_W], seg) for c in range(0, GROUP_W, SEG_W)], axis=1)
                ms = ms * (1.0 / HEAD_DIM)
                p = p * lax.rsqrt(ms + EPS) * gain
            val = p.astype(BF16)
            cols = slice(part * GROUP_W, (part + 1) * GROUP_W)
            if d == 1:
                out_ref[:, cols] = val
            else:
                n = PERM // d
                for s in range(tm // PERM):
                    for r in range(d):
                        out_ref[r, s * n:(s + 1) * n, cols] = val[s * PERM + r * n:s * PERM + (r + 1) * n]


def _even_in(x, g, w, seg, qg, kg, p4, p16):
    batch, seq, _ = x.shape
    outs = [jax.ShapeDtypeStruct((batch, seq, CONV_A_CH), F32)]
    outs += [jax.ShapeDtypeStruct((batch, d, seq // d, GROUP_QKV), BF16) for d in DILATIONS]
    return pl.pallas_call(
        _even_in_kernel,
        grid=(batch, seq // TM),
        in_specs=[_rows(TM, D_MODEL)] + [_resident(a) for a in (g, w, seg, qg, kg, p4, p16)],
        out_specs=[_rows(TM, CONV_A_CH)] + [_classes(d, TM, GROUP_QKV) for d in DILATIONS],
        out_shape=outs,
        compiler_params=_params(2),
        name="even_in",
    )(x, g, w, seg, qg, kg, p4, p16)


def _conv_a_kernel(prev_ref, a_ref, next_ref, wb_ref, b_ref, lg_ref, lb_ref, o_ref, buf_ref, sh_ref, *, n_tiles):
    i = pl.program_id(1)
    ts = a_ref.shape[0]
    h = CONV_A_HALO
    rows = ts + 2 * h
    buf_ref[0:h, :] = jnp.where(i > 0, prev_ref[...], 0.0)
    buf_ref[h:h + ts, :] = a_ref[...]
    buf_ref[h + ts:rows, :] = jnp.where(i < n_tiles - 1, next_ref[...], 0.0)
    n_sh = rows - SUBLANES
    for v in range(SUBLANES):
        sh_ref[v] = buf_ref[v:v + n_sh, :].reshape(n_sh // SUBLANES, SUBLANES, CONV_A_CH)
    off = h - (CONV_A_K - 1) // 2
    nb = CONV_A_RB // SUBLANES
    for r in range(0, ts, CONV_A_RB):
        t0 = r // SUBLANES
        acc = None
        for v in range(SUBLANES):
            us = [u for u in range((CONV_A_K + off) // SUBLANES + 1) if 0 <= SUBLANES * u + v - off < CONV_A_K]
            x = sh_ref[v, t0 + us[0]:t0 + us[-1] + nb]
            for u in us:
                term = x[u - us[0]:u - us[0] + nb] * wb_ref[SUBLANES * u + v - off]
                acc = term if acc is None else acc + term
        acc = acc.reshape(CONV_A_RB, CONV_A_CH) + b_ref[...]
        mu = jnp.mean(acc, axis=-1, keepdims=True)
        xc = acc - mu
        var = jnp.mean(xc * xc, axis=-1, keepdims=True)
        y = xc * lax.rsqrt(var + EPS) * lg_ref[...] + lb_ref[...]
        o_ref[r:r + CONV_A_RB, :] = (y * jax.nn.sigmoid(y)).astype(BF16)


def _conv_a(a, wb, b, lg, lb):
    batch, seq, _ = a.shape
    ts = CONV_A_TS
    n_tiles = seq // ts
    hb = ts // CONV_A_HALO
    rows = ts + 2 * CONV_A_HALO
    main = _rows(ts, CONV_A_CH)
    prev = pl.BlockSpec((None, CONV_A_HALO, CONV_A_CH), lambda bi, i: (bi, jnp.maximum(i * hb - 1, 0), 0))
    nxt = pl.BlockSpec((None, CONV_A_HALO, CONV_A_CH),
                       lambda bi, i: (bi, jnp.minimum((i + 1) * hb, seq // CONV_A_HALO - 1), 0))
    return pl.pallas_call(
        functools.partial(_conv_a_kernel, n_tiles=n_tiles),
        grid=(batch, n_tiles),
        in_specs=[prev, main, nxt, _resident(wb), _resident(b), _resident(lg), _resident(lb)],
        out_specs=main,
        out_shape=jax.ShapeDtypeStruct((batch, seq, CONV_A_CH), BF16),
        scratch_shapes=[pltpu.VMEM((rows, CONV_A_CH), F32),
                        pltpu.VMEM((SUBLANES, rows // SUBLANES - 1, SUBLANES, CONV_A_CH), F32)],
        compiler_params=_params(2),
        name="conv_a",
    )(a, a, a, wb, b, lg, lb)


def _attn_kernel(*refs, n_tiles, halo):
    if halo:
        q_ref, kp_ref, kc_ref, kn_ref, vp_ref, vc_ref, vn_ref, bm_ref, o_ref, l_ref, kw_ref, vw_ref = refs
        edges = ((kw_ref, kp_ref, kn_ref), (vw_ref, vp_ref, vn_ref))
    else:
        q_ref, kc_ref, vc_ref, bm_ref, o_ref, l_ref, kw_ref, vw_ref = refs
        edges = ((kw_ref, None, None), (vw_ref, None, None))
    i = pl.program_id(2)
    n_cls, tq = q_ref.shape[0], q_ref.shape[1]
    lane = lax.broadcasted_iota(jnp.int32, (1, LANES), 1)
    first_head = lane < HEAD_DIM
    keep_first = first_head.astype(BF16)
    keep_second = 1.0 - keep_first
    col = lax.broadcasted_iota(jnp.int32, (1, ATT_KEYS), 1)
    n_sub = tq // ATT_SUB
    lanes = [slice(pr * LANES, (pr + 1) * LANES) for pr in range(N_PAIRS)]
    for c in range(n_cls):
        for (dst, before, after), cur in zip(edges, (kc_ref, vc_ref)):
            pad = jnp.zeros((HALF, GROUP_W), BF16)
            dst[c, 0:HALF, :] = pad if before is None else before[c]
            dst[c, HALF:HALF + tq, :] = cur[c]
            dst[c, HALF + tq:2 * HALF + tq, :] = pad if after is None else after[c]
        for s in range(n_sub):
            r0 = s * ATT_SUB
            lo = jnp.where(i == 0, HALF, 0) if s == 0 else None
            hi = jnp.where(i == n_tiles - 1, ATT_KEYS - HALF, ATT_KEYS) if s == n_sub - 1 else None
            edge = None
            if lo is not None or hi is not None:
                ok = col >= (lo if lo is not None else 0)
                if hi is not None:
                    ok = ok & (col < hi)
                edge = jnp.where(ok, 0.0, NEG_INF)
            scs = []
            for pr, cs in enumerate(lanes):
                q2 = q_ref[c, r0:r0 + ATT_SUB, cs]
                qs = jnp.concatenate([q2 * keep_first, q2 * keep_second], axis=0)
                sc = lax.dot_general(qs, kw_ref[c, r0:r0 + ATT_KEYS, cs], (((1,), (1,)), ((), ())),
                                     preferred_element_type=F32) + bm_ref[pr]
                scs.append(sc if edge is None else sc + edge)
            ms = [jnp.max(sc, axis=-1, keepdims=True) for sc in scs]
            ps = [jnp.exp2(sc - m) for sc, m in zip(scs, ms)]
            dens = [jnp.sum(p, axis=-1, keepdims=True) for p in ps]
            pvs = [_dot(p.astype(BF16), vw_ref[c, r0:r0 + ATT_KEYS, cs]) for p, cs in zip(ps, lanes)]
            lse_rows = jnp.zeros((ATT_SUB, LANES), F32)
            for pr, (cs, pv, m, den) in enumerate(zip(lanes, pvs, ms, dens)):
                o = pv / den
                o_ref[c, r0:r0 + ATT_SUB, cs] = jnp.where(first_head, o[0:ATT_SUB], o[ATT_SUB:]).astype(BF16)
                lse = m * LN2 + jnp.log(den)
                lse_rows = jnp.where(lane == 2 * pr, lse[0:ATT_SUB], lse_rows)
                lse_rows = jnp.where(lane == 2 * pr + 1, lse[ATT_SUB:], lse_rows)
            packed, rest = None, lse_rows
            for t in range(LSE_PARTS):
                part = rest.astype(BF16).astype(F32)
                rest = rest - part
                part = part if t == 0 else pltpu.roll(part, t * HEADS_PER_GROUP, 1)
                packed = part if packed is None else packed + part
            l_ref[c, r0:r0 + ATT_SUB, :] = packed.astype(BF16)


def _attention_group(qkv, bm):
    batch, d, cls_len, _ = qkv.shape
    tq = min(cls_len, ATT_STEP)
    n_tiles = cls_len // tq
    halo = n_tiles > 1
    n_cls = 1 if halo else min(d, ATT_STEP // tq)
    hb = tq // HALF

    def main(c, width=GROUP_W):
        return pl.BlockSpec((None, n_cls, tq, width), lambda b, r, i: (b, r, i, c))

    def before(c):
        return pl.BlockSpec((None, 1, HALF, GROUP_W), lambda b, r, i: (b, r, jnp.maximum(i * hb - 1, 0), c))

    def after(c):
        return pl.BlockSpec((None, 1, HALF, GROUP_W),
                            lambda b, r, i: (b, r, jnp.minimum((i + 1) * hb, cls_len // HALF - 1), c))

    if halo:
        in_specs = [main(0), before(1), main(1), after(1), before(2), main(2), after(2), _resident(bm)]
    else:
        in_specs = [main(0), main(1), main(2), _resident(bm)]
    window = pltpu.VMEM((n_cls, tq + 2 * HALF, GROUP_W), BF16)
    return pl.pallas_call(
        functools.partial(_attn_kernel, n_tiles=n_tiles, halo=halo),
        grid=(batch, d // n_cls, n_tiles),
        in_specs=in_specs,
        out_specs=[main(0), main(0, LANES)],
        out_shape=[jax.ShapeDtypeStruct((batch, d, cls_len, GROUP_W), BF16),
                   jax.ShapeDtypeStruct((batch, d, cls_len, LANES), BF16)],
        scratch_shapes=[window, window],
        compiler_params=_params(3),
        name=f"attn_d{d}",
    )(*([qkv] * (len(in_specs) - 1)), bm)


def _token_order(ref, d, perm_t):
    if d == 1:
        return ref[...].astype(F32)
    n = PERM // d
    blocks = []
    for s in range(ref.shape[1] // n):
        cm = jnp.concatenate([ref[r, s * n:(s + 1) * n, :] for r in range(d)], axis=0)
        blocks.append(_dot(perm_t, cm))
    return jnp.concatenate(blocks, axis=0)


def _ffn_tail(x, g_ref, w1_ref, w3_ref, w2_ref):
    h = _rmsnorm_bf16(x, g_ref[...])
    y = x
    for lo, hi in FFN_CHUNKS:
        a = _dot(h, w1_ref[:, lo:hi])
        b = _dot(h, w3_ref[:, lo:hi])
        y = y + _dot((a * jax.nn.sigmoid(a) * b).astype(BF16), w2_ref[lo:hi, :])
    return y


def _even_out_kernel(x_ref, a_ref, o0_ref, o1_ref, o2_ref, l0_ref, l1_ref, l2_ref, p4t_ref, p16t_ref, ex_ref,
                     w_ref, fg_ref, w1_ref, w3_ref, w2_ref, y_ref):
    perms = (None, p4t_ref[...], p16t_ref[...])
    lane = lax.broadcasted_iota(jnp.int32, (1, LANES), 1)
    lses = []
    for l_ref, d, pt in zip((l0_ref, l1_ref, l2_ref), DILATIONS, perms):
        parts = _token_order(l_ref, d, pt)
        lse = parts
        for t in range(1, LSE_PARTS):
            lse = lse + pltpu.roll(parts, LANES - t * HEADS_PER_GROUP, 1)
        lses.append(lse)
    m = jnp.maximum(jnp.maximum(lses[0], lses[1]), lses[2])
    es = [jnp.exp(l - m) for l in lses]
    inv = 1.0 / (es[0] + es[1] + es[2])
    packed = None
    for g, e in enumerate(es):
        al = jnp.where(lane < HEADS_PER_GROUP, e * inv, 0.0)
        al = al if g == 0 else pltpu.roll(al, g * HEADS_PER_GROUP, 1)
        packed = al if packed is None else packed + al
    terms, rest = None, packed
    for t in range(LSE_PARTS):
        part = rest.astype(BF16).astype(F32)
        rest = rest - part
        part = part if t == 0 else pltpu.roll(part, t * PACK_W, 1)
        terms = part if terms is None else terms + part
    wgt = _dot(terms.astype(BF16), ex_ref[...])
    o = None
    for g, (o_ref, d, pt) in enumerate(zip((o0_ref, o1_ref, o2_ref), DILATIONS, perms)):
        term = wgt[:, g * GROUP_W:(g + 1) * GROUP_W] * _token_order(o_ref, d, pt)
        o = term if o is None else o + term
    y_ref[...] = x_ref[...] + _dot(a_ref[...], w_ref[0:CONV_A_CH, :]) + _dot(o.astype(BF16), w_ref[CONV_A_CH:, :])
    y_ref[...] = _ffn_tail(y_ref[...], fg_ref, w1_ref, w3_ref, w2_ref)


def _even_out(x, a, os, ls, p4t, p16t, ex, w, ffn):
    batch, seq, _ = x.shape
    return pl.pallas_call(
        _even_out_kernel,
        grid=(batch, seq // TM),
        in_specs=[_rows(TM, D_MODEL), _rows(TM, CONV_A_CH)]
                 + [_classes(d, TM, GROUP_W) for d in DILATIONS] + [_classes(d, TM, LANES) for d in DILATIONS]
                 + [_resident(t) for t in (p4t, p16t, ex, w, *ffn)],
        out_specs=_rows(TM, D_MODEL),
        out_shape=jax.ShapeDtypeStruct((batch, seq, D_MODEL), F32),
        compiler_params=_params(2),
        name="even_out_ffn",
    )(x, a, *os, *ls, p4t, p16t, ex, w, *ffn)


def _odd_kernel(prev_ref, x_ref, next_ref, g_ref, win_ref, cw_ref, wout_ref, fg_ref, w1_ref, w3_ref, w2_ref,
                y_ref, buf_ref, *, n_tiles):
    i = pl.program_id(1)
    x = x_ref[...]
    tm = x.shape[0]
    hb = SUBLANES
    halo = jnp.concatenate([prev_ref[...], next_ref[...]], axis=0)
    h = jnp.concatenate([_rmsnorm_bf16(x, g_ref[...]), _rmsnorm_bf16(halo, g_ref[...])], axis=0)
    gate = _dot(h[0:tm], win_ref[:, 0:D_MODEL])
    cu = _dot(h, win_ref[:, D_MODEL:2 * D_MODEL]) * _dot(h, win_ref[:, 2 * D_MODEL:3 * D_MODEL])
    buf_ref[0:hb, :] = jnp.where(i > 0, cu[tm:tm + hb], 0.0)
    buf_ref[hb:hb + tm, :] = cu[0:tm]
    buf_ref[hb + tm:2 * hb + tm, :] = jnp.where(i < n_tiles - 1, cu[tm + hb:], 0.0)
    conv = (buf_ref[hb - 1:hb - 1 + tm, :] * cw_ref[0:1, :] + buf_ref[hb:hb + tm, :] * cw_ref[1:2, :]
            + buf_ref[hb + 1:hb + 1 + tm, :] * cw_ref[2:3, :])
    y_ref[...] = x + _dot((gate * conv).astype(BF16), wout_ref[...])
    y_ref[...] = _ffn_tail(y_ref[...], fg_ref, w1_ref, w3_ref, w2_ref)


def _odd(x, g, win, cw, wout, ffn):
    batch, seq, _ = x.shape
    n_tiles = seq // TM
    per = TM // SUBLANES
    main = _rows(TM, D_MODEL)
    prev = pl.BlockSpec((None, SUBLANES, D_MODEL), lambda bi, i: (bi, jnp.maximum(i * per - 1, 0), 0))
    nxt = pl.BlockSpec((None, SUBLANES, D_MODEL), lambda bi, i: (bi, jnp.minimum((i + 1) * per, seq // SUBLANES - 1), 0))
    return pl.pallas_call(
        functools.partial(_odd_kernel, n_tiles=n_tiles),
        grid=(batch, n_tiles),
        in_specs=[prev, main, nxt] + [_resident(t) for t in (g, win, cw, wout, *ffn)],
        out_specs=main,
        out_shape=jax.ShapeDtypeStruct((batch, seq, D_MODEL), F32),
        scratch_shapes=[pltpu.VMEM((TM + 2 * SUBLANES, D_MODEL), F32)],
        compiler_params=_params(2),
        name="odd_ffn",
    )(x, x, x, g, win, cw, wout, *ffn)


def _t5_bucket_np(rel):
    nb = N_BUCKETS // 2
    max_exact = nb // 2
    n = np.abs(rel)
    nf = np.maximum(n, 1).astype(np.float32)
    large = max_exact + (np.log(nf / max_exact) / math.log(MAX_DISTANCE / max_exact) * (nb - max_exact)).astype(np.int32)
    large = np.minimum(large, nb - 1)
    return np.where(rel > 0, nb, 0) + np.where(n < max_exact, n, large)


def _bias_windows(rel_bias):
    n_off = ATT_SUB + ATT_KEYS - 1
    off = np.arange(n_off) - (ATT_SUB - 1) - HALF
    band = np.abs(off) <= HALF
    out = []
    for g, d in enumerate(DILATIONS):
        tab = rel_bias[:, g * HEADS_PER_GROUP:(g + 1) * HEADS_PER_GROUP].astype(F32)
        vec = jnp.take(tab, jnp.asarray(_t5_bucket_np(off * d)), axis=0)
        vec = jnp.where(band[:, None], vec * LOG2E, NEG_INF).T
        flat = jnp.tile(jnp.pad(vec, ((0, 0), (0, 1))), (1, ATT_SUB))[:, :ATT_SUB * n_off]
        skew = flat.reshape(HEADS_PER_GROUP, ATT_SUB, n_off)
        win = skew[:, :, ATT_SUB - 1:ATT_SUB - 1 + ATT_KEYS]
        out.append(win.reshape(N_PAIRS, 2 * ATT_SUB, ATT_KEYS))
    return out


def _head_segments():
    h = np.arange(SEG_W) // HEAD_DIM
    return jnp.asarray(h[:, None] == h[None, :], BF16)


def _head_expand():
    src = np.arange(LANES)
    col = np.arange(N_GROUPS * GROUP_W)
    hit = (src[:, None] % PACK_W == (col // GROUP_W) * HEADS_PER_GROUP + (col % GROUP_W) // HEAD_DIM)[:, :]
    return jnp.asarray(hit & (src[:, None] < LSE_PARTS * PACK_W), BF16)


def _class_perm(d):
    n = PERM // d
    src = (np.arange(PERM) % n) * d + np.arange(PERM) // n
    return np.arange(PERM)[None, :] == src[:, None]


def _trunk(x, p):
    for layer in range(DEPTH):
        i = layer // 2
        ffn = (p["ffn_norm"][layer], p["ffn_w1"][layer], p["ffn_w3"][layer], p["ffn_w2"][layer])
        if layer % 2 == 0:
            a, *qkvs = _even_in(x, p["even_norm"][i], p["even_w_in"][i], p["seg"], p["q_gain"][i], p["k_gain"][i],
                                p["perm"][1], p["perm"][2])
            a = _conv_a(a, p["conv_a_w"][i], p["conv_a_b"][i], p["conv_a_ln_g"][i], p["conv_a_ln_b"][i])
            os, ls = zip(*[_attention_group(qkv, bm) for qkv, bm in zip(qkvs, p["bias"])])
            x = _even_out(x, a, os, ls, p["perm_t"][1], p["perm_t"][2], p["expand"], p["even_w_out"][i], ffn)
        else:
            x = _odd(x, p["odd_norm"][i], p["odd_w_in"][i], p["conv_c_w"][i], p["odd_w_out"][i], ffn)
    return x


def _prepare(rel_bias, even_norm, even_w_in, conv_a_w, conv_a_b, conv_a_ln_g, conv_a_ln_b, q_norm, k_norm,
             even_w_out, odd_norm, odd_w_in, conv_c_w, odd_w_out, ffn_norm, ffn_w1, ffn_w3, ffn_w2):
    row = lambda a: a.astype(F32)[:, None, :]
    tile_heads = lambda a: jnp.tile(a.astype(F32), (1, HEADS_PER_GROUP))[:, None, :]
    perms = [None if d == 1 else _class_perm(d) for d in DILATIONS]
    return dict(
        bias=_bias_windows(rel_bias), seg=_head_segments(), expand=_head_expand(),
        perm=[None if m is None else jnp.asarray(m, BF16) for m in perms],
        perm_t=[None if m is None else jnp.asarray(m.T, BF16) for m in perms],
        even_norm=row(even_norm), even_w_in=even_w_in.astype(BF16),
        conv_a_w=jnp.broadcast_to(conv_a_w.astype(F32)[:, :, None, :], conv_a_w.shape[:2] + (SUBLANES, CONV_A_CH)),
        conv_a_b=row(conv_a_b), conv_a_ln_g=row(conv_a_ln_g), conv_a_ln_b=row(conv_a_ln_b),
        q_gain=tile_heads(q_norm) * (HEAD_DIM ** -0.5 * LOG2E), k_gain=tile_heads(k_norm),
        even_w_out=even_w_out.astype(BF16),
        odd_norm=row(odd_norm), odd_w_in=odd_w_in.astype(BF16), conv_c_w=conv_c_w.astype(F32),
        odd_w_out=odd_w_out.astype(BF16),
        ffn_norm=row(ffn_norm), ffn_w1=ffn_w1.astype(BF16), ffn_w3=ffn_w3.astype(BF16), ffn_w2=ffn_w2.astype(BF16),
    )


def kernel(x_prompt, x_sample, rel_bias, even_norm, even_w_in, conv_a_w, conv_a_b, conv_a_ln_g, conv_a_ln_b, q_norm, k_norm, even_w_out, odd_norm, odd_w_in, conv_c_w, odd_w_out, ffn_norm, ffn_w1, ffn_w3, ffn_w2):
    p = _prepare(rel_bias, even_norm, even_w_in, conv_a_w, conv_a_b, conv_a_ln_g, conv_a_ln_b, q_norm, k_norm,
                 even_w_out, odd_norm, odd_w_in, conv_c_w, odd_w_out, ffn_norm, ffn_w1, ffn_w3, ffn_w2)
    return (_trunk(x_prompt, p), _trunk(x_sample, p))
```

```python
import functools
import math

import numpy as np
import jax
import jax.numpy as jnp
from jax import lax
from jax.experimental import pallas as pl
from jax.experimental.pallas import tpu as pltpu

F32 = jnp.float32
BF16 = jnp.bfloat16

D_MODEL = 1024
DEPTH = 4
HEAD_DIM = 64
WINDOWS = (128, 512, 2048)
DILATIONS = (1, 4, 16)
N_GROUPS = 3
HEADS_PER_GROUP = 8
GROUP_W = HEADS_PER_GROUP * HEAD_DIM
GROUP_QKV = 3 * GROUP_W
CONV_A_CH = D_MODEL // 2
CONV_A_K = 31
FFN_HIDDEN = 2816
N_BUCKETS = 32
MAX_DISTANCE = 1024
EPS = 1e-6
NEG_INF = -1e30
LOG2E = math.log2(math.e)
LN2 = math.log(2.0)
HALF = 64
assert all(w // (2 * d) == HALF for w, d in zip(WINDOWS, DILATIONS))

V7X_VMEM_BYTES = 64 * 1024 * 1024
VMEM_LIMIT_BYTES = V7X_VMEM_BYTES * 7 // 8
LANES = 128
SUBLANES = 8
MXU_DIM = 256

TM = 512
PERM = MXU_DIM
SEG_W = MXU_DIM
LSE_PARTS = 3
ATT_SUB = 128
ATT_KEYS = ATT_SUB + 2 * HALF
ATT_STEP = 512
N_PAIRS = GROUP_W // LANES
CONV_A_RB = 32
CONV_A_HALO = 16
FFN_CHUNKS = ((0, 1536), (1536, 2816))
PACK_W = N_GROUPS * HEADS_PER_GROUP


def _params(n_axes):
    return pltpu.CompilerParams(dimension_semantics=("parallel",) * n_axes,
                                vmem_limit_bytes=VMEM_LIMIT_BYTES)


def _resident(a):
    nd = a.ndim
    return pl.BlockSpec(a.shape, lambda *_: (0,) * nd, pipeline_mode=pl.Buffered(1))


def _rows(tm, width):
    return pl.BlockSpec((None, tm, width), lambda b, i: (b, i, 0))


def _classes(d, tm, width):
    if d == 1:
        return pl.BlockSpec((None, None, tm, width), lambda b, i: (b, 0, i, 0))
    return pl.BlockSpec((None, d, tm // d, width), lambda b, i: (b, 0, i, 0))


def _rmsnorm_bf16(x, g):
    ms = jnp.mean(x * x, axis=-1, keepdims=True)
    return (x * lax.rsqrt(ms + EPS) * g).astype(BF16)


def _dot(a, b):
    return jnp.dot(a, b, preferred_element_type=F32)


def _zero_after(x):
    bits = lax.bitcast_convert_type(x, jnp.uint32)
    return lax.bitcast_convert_type((bits >> 16) >> 16, F32)


def _conv_a_rows(sh_ref, wb_ref, b_ref, lg_ref, lb_ref, r, start):
    off = CONV_A_HALO - (CONV_A_K - 1) // 2
    nb = CONV_A_RB // SUBLANES
    t0 = r // SUBLANES
    accs = []
    for c in range(0, CONV_A_CH, LANES):
        acc = start[:, c:c + LANES]
        for v in range(SUBLANES):
            us = [u for u in range((CONV_A_K + off) // SUBLANES + 1) if 0 <= SUBLANES * u + v - off < CONV_A_K]
            x = sh_ref[v, t0 + us[0]:t0 + us[-1] + nb, :, c:c + LANES]
            for u in us:
                acc = acc + x[u - us[0]:u - us[0] + nb] * wb_ref[SUBLANES * u + v - off, :, c:c + LANES]
        accs.append(acc.reshape(CONV_A_RB, LANES))
    acc = jnp.concatenate(accs, axis=1) + b_ref[...]
    mu = jnp.mean(acc, axis=-1, keepdims=True)
    xc = acc - mu
    var = jnp.mean(xc * xc, axis=-1, keepdims=True)
    y = xc * lax.rsqrt(var + EPS) * lg_ref[...] + lb_ref[...]
    return (y * jax.nn.sigmoid(y)).astype(BF16)


def _even_in_kernel(prev_ref, x_ref, next_ref, g_ref, w_ref, seg_ref, qg_ref, kg_ref, p4_ref, p16_ref,
                    cw_ref, cb_ref, lg_ref, lb_ref, a_ref, q0_ref, q1_ref, q2_ref, buf_ref, sh_ref, *, n_tiles):
    i = pl.program_id(1)
    h = _rmsnorm_bf16(x_ref[...], g_ref[...])
    tm = h.shape[0]
    hh = CONV_A_HALO

    def proj(hh_, c):
        return _dot(hh_, w_ref[:, c * GROUP_W:(c + 1) * GROUP_W])

    halo = jnp.concatenate([prev_ref[...], next_ref[...]], axis=0)
    h_ext = jnp.concatenate([h, _rmsnorm_bf16(halo, g_ref[...])], axis=0)
    glu = proj(h_ext, 0) * jax.nn.sigmoid(proj(h_ext, 1))
    buf_ref[0:hh, :] = jnp.where(i > 0, glu[tm:tm + hh], 0.0)
    buf_ref[hh:hh + tm, :] = glu[0:tm]
    buf_ref[hh + tm:2 * hh + tm, :] = jnp.where(i < n_tiles - 1, glu[tm + hh:], 0.0)
    n_sh = tm + 2 * hh - SUBLANES
    for v in range(SUBLANES):
        sh_ref[v] = buf_ref[v:v + n_sh, :].reshape(n_sh // SUBLANES, SUBLANES, CONV_A_CH)
    conv_rows = list(range(0, tm, CONV_A_RB))
    per_proj = -(-len(conv_rows) // (3 * N_GROUPS))

    def conv_some(after):
        start = _zero_after(after[0:SUBLANES, :])
        for r in conv_rows[:per_proj]:
            a_ref[r:r + CONV_A_RB, :] = _conv_a_rows(sh_ref, cw_ref, cb_ref, lg_ref, lb_ref, r, start)
        del conv_rows[:per_proj]

    seg = seg_ref[...]
    for g, (d, perm_ref, out_ref) in enumerate(zip(DILATIONS, (None, p4_ref, p16_ref), (q0_ref, q1_ref, q2_ref))):
        if d == 1:
            hg = h
        else:
            hg = jnp.concatenate([_dot(perm_ref[...], h[s:s + PERM]).astype(BF16) for s in range(0, tm, PERM)], axis=0)
        for part in range(3):
            p = proj(hg, 2 + part * N_GROUPS + g)
            conv_some(p)
            if part < 2:
                gain = qg_ref[...] if part == 0 else kg_ref[...]
                sq = (p * p).astype(BF16)
                ms = jnp.concatenate([_dot(sq[:, c:c + SEG_W], seg) for c in range(0, GROUP_W, SEG_W)], axis=1)
                ms = ms * (1.0 / HEAD_DIM)
                p = p * lax.rsqrt(ms + EPS) * gain
            val = p.astype(BF16)
            cols = slice(part * GROUP_W, (part + 1) * GROUP_W)
            if d == 1:
                out_ref[:, cols] = val
            else:
                n = PERM // d
                for s in range(tm // PERM):
                    for r in range(d):
                        out_ref[r, s * n:(s + 1) * n, cols] = val[s * PERM + r * n:s * PERM + (r + 1) * n]


def _even_in(x, g, w, seg, qg, kg, p4, p16, cw, cb, lg, lb):
    batch, seq, _ = x.shape
    n_tiles = seq // TM
    per = TM // CONV_A_HALO
    rows = TM + 2 * CONV_A_HALO
    prev = pl.BlockSpec((None, CONV_A_HALO, D_MODEL), lambda bi, i: (bi, jnp.maximum(i * per - 1, 0), 0))
    nxt = pl.BlockSpec((None, CONV_A_HALO, D_MODEL),
                       lambda bi, i: (bi, jnp.minimum((i + 1) * per, seq // CONV_A_HALO - 1), 0))
    outs = [jax.ShapeDtypeStruct((batch, seq, CONV_A_CH), BF16)]
    outs += [jax.ShapeDtypeStruct((batch, d, seq // d, GROUP_QKV), BF16) for d in DILATIONS]
    return pl.pallas_call(
        functools.partial(_even_in_kernel, n_tiles=n_tiles),
        grid=(batch, n_tiles),
        in_specs=[prev, _rows(TM, D_MODEL), nxt] + [_resident(t) for t in (g, w, seg, qg, kg, p4, p16, cw, cb, lg, lb)],
        out_specs=[_rows(TM, CONV_A_CH)] + [_classes(d, TM, GROUP_QKV) for d in DILATIONS],
        out_shape=outs,
        scratch_shapes=[pltpu.VMEM((rows, CONV_A_CH), F32),
                        pltpu.VMEM((SUBLANES, rows // SUBLANES - 1, SUBLANES, CONV_A_CH), F32)],
        compiler_params=_params(2),
        name="even_in",
    )(x, x, x, g, w, seg, qg, kg, p4, p16, cw, cb, lg, lb)


def _attn_kernel(*refs, n_tiles, halo):
    if halo:
        q_ref, kp_ref, kc_ref, kn_ref, vp_ref, vc_ref, vn_ref, bm_ref, o_ref, l_ref, kw_ref, vw_ref = refs
        edges = ((kw_ref, kp_ref, kn_ref), (vw_ref, vp_ref, vn_ref))
    else:
        q_ref, kc_ref, vc_ref, bm_ref, o_ref, l_ref, kw_ref, vw_ref = refs
        edges = ((kw_ref, None, None), (vw_ref, None, None))
    i = pl.program_id(2)
    n_cls, tq = q_ref.shape[0], q_ref.shape[1]
    lane = lax.broadcasted_iota(jnp.int32, (1, LANES), 1)
    first_head = lane < HEAD_DIM
    keep_first = first_head.astype(BF16)
    keep_second = 1.0 - keep_first
    col = lax.broadcasted_iota(jnp.int32, (1, ATT_KEYS), 1)
    n_sub = tq // ATT_SUB
    lanes = [slice(pr * LANES, (pr + 1) * LANES) for pr in range(N_PAIRS)]
    for c in range(n_cls):
        for (dst, before, after), cur in zip(edges, (kc_ref, vc_ref)):
            pad = jnp.zeros((HALF, GROUP_W), BF16)
            dst[c, 0:HALF, :] = pad if before is None else before[c]
            dst[c, HALF:HALF + tq, :] = cur[c]
            dst[c, HALF + tq:2 * HALF + tq, :] = pad if after is None else after[c]
        for s in range(n_sub):
            r0 = s * ATT_SUB
            lo = jnp.where(i == 0, HALF, 0) if s == 0 else None
            hi = jnp.where(i == n_tiles - 1, ATT_KEYS - HALF, ATT_KEYS) if s == n_sub - 1 else None
            edge = None
            if lo is not None or hi is not None:
                ok = col >= (lo if lo is not None else 0)
                if hi is not None:
                    ok = ok & (col < hi)
                edge = jnp.where(ok, 0.0, NEG_INF)
            scs = []
            for pr, cs in enumerate(lanes):
                q2 = q_ref[c, r0:r0 + ATT_SUB, cs]
                qs = jnp.concatenate([q2 * keep_first, q2 * keep_second], axis=0)
                sc = lax.dot_general(qs, kw_ref[c, r0:r0 + ATT_KEYS, cs], (((1,), (1,)), ((), ())),
                                     preferred_element_type=F32) + bm_ref[pr]
                scs.append(sc if edge is None else sc + edge)
            ms = [jnp.max(sc, axis=-1, keepdims=True) for sc in scs]
            ps = [jnp.exp2(sc - m) for sc, m in zip(scs, ms)]
            dens = [jnp.sum(p, axis=-1, keepdims=True) for p in ps]
            pvs = [_dot(p.astype(BF16), vw_ref[c, r0:r0 + ATT_KEYS, cs]) for p, cs in zip(ps, lanes)]
            lse_rows = jnp.zeros((ATT_SUB, LANES), F32)
            for pr, (cs, pv, m, den) in enumerate(zip(lanes, pvs, ms, dens)):
                o = pv / den
                o_ref[c, r0:r0 + ATT_SUB, cs] = jnp.where(first_head, o[0:ATT_SUB], o[ATT_SUB:]).astype(BF16)
                lse = m * LN2 + jnp.log(den)
                lse_rows = jnp.where(lane == 2 * pr, lse[0:ATT_SUB], lse_rows)
                lse_rows = jnp.where(lane == 2 * pr + 1, lse[ATT_SUB:], lse_rows)
            packed, rest = None, lse_rows
            for t in range(LSE_PARTS):
                part = rest.astype(BF16).astype(F32)
                rest = rest - part
                part = part if t == 0 else pltpu.roll(part, t * HEADS_PER_GROUP, 1)
                packed = part if packed is None else packed + part
            l_ref[c, r0:r0 + ATT_SUB, :] = packed.astype(BF16)


def _attention_group(qkv, bm):
    batch, d, cls_len, _ = qkv.shape
    tq = min(cls_len, ATT_STEP)
    n_tiles = cls_len // tq
    halo = n_tiles > 1
    n_cls = 1 if halo else min(d, ATT_STEP // tq)
    hb = tq // HALF

    def main(c, width=GROUP_W):
        return pl.BlockSpec((None, n_cls, tq, width), lambda b, r, i: (b, r, i, c))

    def before(c):
        return pl.BlockSpec((None, 1, HALF, GROUP_W), lambda b, r, i: (b, r, jnp.maximum(i * hb - 1, 0), c))

    def after(c):
        return pl.BlockSpec((None, 1, HALF, GROUP_W),
                            lambda b, r, i: (b, r, jnp.minimum((i + 1) * hb, cls_len // HALF - 1), c))

    if halo:
        in_specs = [main(0), before(1), main(1), after(1), before(2), main(2), after(2), _resident(bm)]
    else:
        in_specs = [main(0), main(1), main(2), _resident(bm)]
    window = pltpu.VMEM((n_cls, tq + 2 * HALF, GROUP_W), BF16)
    return pl.pallas_call(
        functools.partial(_attn_kernel, n_tiles=n_tiles, halo=halo),
        grid=(batch, d // n_cls, n_tiles),
        in_specs=in_specs,
        out_specs=[main(0), main(0, LANES)],
        out_shape=[jax.ShapeDtypeStruct((batch, d, cls_len, GROUP_W), BF16),
                   jax.ShapeDtypeStruct((batch, d, cls_len, LANES), BF16)],
        scratch_shapes=[window, window],
        compiler_params=_params(3),
        name=f"attn_d{d}",
    )(*([qkv] * (len(in_specs) - 1)), bm)


def _token_order(ref, d, perm_t):
    if d == 1:
        return ref[...].astype(F32)
    n = PERM // d
    blocks = []
    for s in range(ref.shape[1] // n):
        cm = jnp.concatenate([ref[r, s * n:(s + 1) * n, :] for r in range(d)], axis=0)
        blocks.append(_dot(perm_t, cm))
    return jnp.concatenate(blocks, axis=0)


def _ffn_tail(x, g_ref, w1_ref, w3_ref, w2_ref):
    h = _rmsnorm_bf16(x, g_ref[...])
    y = x
    for lo, hi in FFN_CHUNKS:
        a = _dot(h, w1_ref[:, lo:hi])
        b = _dot(h, w3_ref[:, lo:hi])
        y = y + _dot((a * jax.nn.sigmoid(a) * b).astype(BF16), w2_ref[lo:hi, :])
    return y


def _even_out_kernel(x_ref, a_ref, o0_ref, o1_ref, o2_ref, l0_ref, l1_ref, l2_ref, p4t_ref, p16t_ref, ex_ref,
                     w_ref, fg_ref, w1_ref, w3_ref, w2_ref, y_ref):
    perms = (None, p4t_ref[...], p16t_ref[...])
    lane = lax.broadcasted_iota(jnp.int32, (1, LANES), 1)
    lses = []
    for l_ref, d, pt in zip((l0_ref, l1_ref, l2_ref), DILATIONS, perms):
        parts = _token_order(l_ref, d, pt)
        lse = parts
        for t in range(1, LSE_PARTS):
            lse = lse + pltpu.roll(parts, LANES - t * HEADS_PER_GROUP, 1)
        lses.append(lse)
    m = jnp.maximum(jnp.maximum(lses[0], lses[1]), lses[2])
    es = [jnp.exp(l - m) for l in lses]
    inv = 1.0 / (es[0] + es[1] + es[2])
    packed = None
    for g, e in enumerate(es):
        al = jnp.where(lane < HEADS_PER_GROUP, e * inv, 0.0)
        al = al if g == 0 else pltpu.roll(al, g * HEADS_PER_GROUP, 1)
        packed = al if packed is None else packed + al
    terms, rest = None, packed
    for t in range(LSE_PARTS):
        part = rest.astype(BF16).astype(F32)
        rest = rest - part
        part = part if t == 0 else pltpu.roll(part, t * PACK_W, 1)
        terms = part if terms is None else terms + part
    wgt = _dot(terms.astype(BF16), ex_ref[...])
    o = None
    for g, (o_ref, d, pt) in enumerate(zip((o0_ref, o1_ref, o2_ref), DILATIONS, perms)):
        term = wgt[:, g * GROUP_W:(g + 1) * GROUP_W] * _token_order(o_ref, d, pt)
        o = term if o is None else o + term
    y_ref[...] = x_ref[...] + _dot(a_ref[...], w_ref[0:CONV_A_CH, :]) + _dot(o.astype(BF16), w_ref[CONV_A_CH:, :])
    y_ref[...] = _ffn_tail(y_ref[...], fg_ref, w1_ref, w3_ref, w2_ref)


def _even_out(x, a, os, ls, p4t, p16t, ex, w, ffn):
    batch, seq, _ = x.shape
    return pl.pallas_call(
        _even_out_kernel,
        grid=(batch, seq // TM),
        in_specs=[_rows(TM, D_MODEL), _rows(TM, CONV_A_CH)]
                 + [_classes(d, TM, GROUP_W) for d in DILATIONS] + [_classes(d, TM, LANES) for d in DILATIONS]
                 + [_resident(t) for t in (p4t, p16t, ex, w, *ffn)],
        out_specs=_rows(TM, D_MODEL),
        out_shape=jax.ShapeDtypeStruct((batch, seq, D_MODEL), F32),
        compiler_params=_params(2),
        name="even_out_ffn",
    )(x, a, *os, *ls, p4t, p16t, ex, w, *ffn)


def _odd_kernel(prev_ref, x_ref, next_ref, g_ref, win_ref, cw_ref, wout_ref, fg_ref, w1_ref, w3_ref, w2_ref,
                y_ref, buf_ref, *, n_tiles):
    i = pl.program_id(1)
    x = x_ref[...]
    tm = x.shape[0]
    hb = SUBLANES
    halo = jnp.concatenate([prev_ref[...], next_ref[...]], axis=0)
    h = jnp.concatenate([_rmsnorm_bf16(x, g_ref[...]), _rmsnorm_bf16(halo, g_ref[...])], axis=0)
    gate = _dot(h[0:tm], win_ref[:, 0:D_MODEL])
    cu = _dot(h, win_ref[:, D_MODEL:2 * D_MODEL]) * _dot(h, win_ref[:, 2 * D_MODEL:3 * D_MODEL])
    buf_ref[0:hb, :] = jnp.where(i > 0, cu[tm:tm + hb], 0.0)
    buf_ref[hb:hb + tm, :] = cu[0:tm]
    buf_ref[hb + tm:2 * hb + tm, :] = jnp.where(i < n_tiles - 1, cu[tm + hb:], 0.0)
    conv = (buf_ref[hb - 1:hb - 1 + tm, :] * cw_ref[0:1, :] + buf_ref[hb:hb + tm, :] * cw_ref[1:2, :]
            + buf_ref[hb + 1:hb + 1 + tm, :] * cw_ref[2:3, :])
    y_ref[...] = x + _dot((gate * conv).astype(BF16), wout_ref[...])
    y_ref[...] = _ffn_tail(y_ref[...], fg_ref, w1_ref, w3_ref, w2_ref)


def _odd(x, g, win, cw, wout, ffn):
    batch, seq, _ = x.shape
    n_tiles = seq // TM
    per = TM // SUBLANES
    main = _rows(TM, D_MODEL)
    prev = pl.BlockSpec((None, SUBLANES, D_MODEL), lambda bi, i: (bi, jnp.maximum(i * per - 1, 0), 0))
    nxt = pl.BlockSpec((None, SUBLANES, D_MODEL), lambda bi, i: (bi, jnp.minimum((i + 1) * per, seq // SUBLANES - 1), 0))
    return pl.pallas_call(
        functools.partial(_odd_kernel, n_tiles=n_tiles),
        grid=(batch, n_tiles),
        in_specs=[prev, main, nxt] + [_resident(t) for t in (g, win, cw, wout, *ffn)],
        out_specs=main,
        out_shape=jax.ShapeDtypeStruct((batch, seq, D_MODEL), F32),
        scratch_shapes=[pltpu.VMEM((TM + 2 * SUBLANES, D_MODEL), F32)],
        compiler_params=_params(2),
        name="odd_ffn",
    )(x, x, x, g, win, cw, wout, *ffn)


def _t5_bucket_np(rel):
    nb = N_BUCKETS // 2
    max_exact = nb // 2
    n = np.abs(rel)
    nf = np.maximum(n, 1).astype(np.float32)
    large = max_exact + (np.log(nf / max_exact) / math.log(MAX_DISTANCE / max_exact) * (nb - max_exact)).astype(np.int32)
    large = np.minimum(large, nb - 1)
    return np.where(rel > 0, nb, 0) + np.where(n < max_exact, n, large)


def _bias_windows(rel_bias):
    n_off = ATT_SUB + ATT_KEYS - 1
    off = np.arange(n_off) - (ATT_SUB - 1) - HALF
    band = np.abs(off) <= HALF
    out = []
    for g, d in enumerate(DILATIONS):
        tab = rel_bias[:, g * HEADS_PER_GROUP:(g + 1) * HEADS_PER_GROUP].astype(F32)
        vec = jnp.take(tab, jnp.asarray(_t5_bucket_np(off * d)), axis=0)
        vec = jnp.where(band[:, None], vec * LOG2E, NEG_INF).T
        flat = jnp.tile(jnp.pad(vec, ((0, 0), (0, 1))), (1, ATT_SUB))[:, :ATT_SUB * n_off]
        skew = flat.reshape(HEADS_PER_GROUP, ATT_SUB, n_off)
        win = skew[:, :, ATT_SUB - 1:ATT_SUB - 1 + ATT_KEYS]
        out.append(win.reshape(N_PAIRS, 2 * ATT_SUB, ATT_KEYS))
    return out


def _head_segments():
    h = np.arange(SEG_W) // HEAD_DIM
    return jnp.asarray(h[:, None] == h[None, :], BF16)


def _head_expand():
    src = np.arange(LANES)
    col = np.arange(N_GROUPS * GROUP_W)
    hit = (src[:, None] % PACK_W == (col // GROUP_W) * HEADS_PER_GROUP + (col % GROUP_W) // HEAD_DIM)[:, :]
    return jnp.asarray(hit & (src[:, None] < LSE_PARTS * PACK_W), BF16)


def _class_perm(d):
    n = PERM // d
    src = (np.arange(PERM) % n) * d + np.arange(PERM) // n
    return np.arange(PERM)[None, :] == src[:, None]


def _trunk(x, p):
    for layer in range(DEPTH):
        i = layer // 2
        ffn = (p["ffn_norm"][layer], p["ffn_w1"][layer], p["ffn_w3"][layer], p["ffn_w2"][layer])
        if layer % 2 == 0:
            a, *qkvs = _even_in(x, p["even_norm"][i], p["even_w_in"][i], p["seg"], p["q_gain"][i], p["k_gain"][i],
                                p["perm"][1], p["perm"][2],
                                p["conv_a_w"][i], p["conv_a_b"][i], p["conv_a_ln_g"][i], p["conv_a_ln_b"][i])
            os, ls = zip(*[_attention_group(qkv, bm) for qkv, bm in zip(qkvs, p["bias"])])
            x = _even_out(x, a, os, ls, p["perm_t"][1], p["perm_t"][2], p["expand"], p["even_w_out"][i], ffn)
        else:
            x = _odd(x, p["odd_norm"][i], p["odd_w_in"][i], p["conv_c_w"][i], p["odd_w_out"][i], ffn)
    return x


def _prepare(rel_bias, even_norm, even_w_in, conv_a_w, conv_a_b, conv_a_ln_g, conv_a_ln_b, q_norm, k_norm,
             even_w_out, odd_norm, odd_w_in, conv_c_w, odd_w_out, ffn_norm, ffn_w1, ffn_w3, ffn_w2):
    row = lambda a: a.astype(F32)[:, None, :]
    tile_heads = lambda a: jnp.tile(a.astype(F32), (1, HEADS_PER_GROUP))[:, None, :]
    perms = [None if d == 1 else _class_perm(d) for d in DILATIONS]
    return dict(
        bias=_bias_windows(rel_bias), seg=_head_segments(), expand=_head_expand(),
        perm=[None if m is None else jnp.asarray(m, BF16) for m in perms],
        perm_t=[None if m is None else jnp.asarray(m.T, BF16) for m in perms],
        even_norm=row(even_norm), even_w_in=even_w_in.astype(BF16),
        conv_a_w=jnp.broadcast_to(conv_a_w.astype(F32)[:, :, None, :], conv_a_w.shape[:2] + (SUBLANES, CONV_A_CH)),
        conv_a_b=row(conv_a_b), conv_a_ln_g=row(conv_a_ln_g), conv_a_ln_b=row(conv_a_ln_b),
        q_gain=tile_heads(q_norm) * (HEAD_DIM ** -0.5 * LOG2E), k_gain=tile_heads(k_norm),
        even_w_out=even_w_out.astype(BF16),
        odd_norm=row(odd_norm), odd_w_in=odd_w_in.astype(BF16), conv_c_w=conv_c_w.astype(F32),
        odd_w_out=odd_w_out.astype(BF16),
        ffn_norm=row(ffn_norm), ffn_w1=ffn_w1.astype(BF16), ffn_w3=ffn_w3.astype(BF16), ffn_w2=ffn_w2.astype(BF16),
    )


def kernel(x_prompt, x_sample, rel_bias, even_norm, even_w_in, conv_a_w, conv_a_b, conv_a_ln_g, conv_a_ln_b, q_norm, k_norm, even_w_out, odd_norm, odd_w_in, conv_c_w, odd_w_out, ffn_norm, ffn_w1, ffn_w3, ffn_w2):
    p = _prepare(rel_bias, even_norm, even_w_in, conv_a_w, conv_a_b, conv_a_ln_g, conv_a_ln_b, q_norm, k_norm,
                 even_w_out, odd_norm, odd_w_in, conv_c_w, odd_w_out, ffn_norm, ffn_w1, ffn_w3, ffn_w2)
    return (_trunk(x_prompt, p), _trunk(x_sample, p))
```

```python
import functools
import math

import numpy as np
import jax
import jax.numpy as jnp
from jax import lax
from jax.experimental import pallas as pl
from jax.experimental.pallas import tpu as pltpu

F32 = jnp.float32
BF16 = jnp.bfloat16

D_MODEL = 1024
DEPTH = 4
HEAD_DIM = 64
WINDOWS = (128, 512, 2048)
DILATIONS = (1, 4, 16)
N_GROUPS = 3
HEADS_PER_GROUP = 8
GROUP_W = HEADS_PER_GROUP * HEAD_DIM
GROUP_QKV = 3 * GROUP_W
CONV_A_CH = D_MODEL // 2
CONV_A_K = 31
FFN_HIDDEN = 2816
N_BUCKETS = 32
MAX_DISTANCE = 1024
EPS = 1e-6
NEG_INF = -1e30
LOG2E = math.log2(math.e)
HALF = 64
assert all(w // (2 * d) == HALF for w, d in zip(WINDOWS, DILATIONS))

V7X_VMEM_BYTES = 64 * 1024 * 1024
VMEM_LIMIT_BYTES = V7X_VMEM_BYTES * 7 // 8
LANES = 128
SUBLANES = 8
MXU_DIM = 256

TM = 512
PERM = MXU_DIM
SEG_W = MXU_DIM
STAT_W = 2 * HEADS_PER_GROUP
STAT_PARTS = 3
ATT_SUB = 128
ATT_KEYS = ATT_SUB + 2 * HALF
ATT_STEP = 1024
assert ATT_KEYS == 2 * LANES
N_PAIRS = GROUP_W // LANES
CONV_A_TS = 256
CONV_A_RB = 32
CONV_A_HALO = 16
FFN_CHUNKS = ((0, 1536), (1536, 2816))
PACK_W = N_GROUPS * HEADS_PER_GROUP


def _params(n_axes):
    return pltpu.CompilerParams(dimension_semantics=("parallel",) * n_axes,
                                vmem_limit_bytes=VMEM_LIMIT_BYTES)


def _resident(a):
    nd = a.ndim
    return pl.BlockSpec(a.shape, lambda *_: (0,) * nd, pipeline_mode=pl.Buffered(1))


def _rows(tm, width):
    return pl.BlockSpec((None, tm, width), lambda b, i: (b, i, 0))


def _classes(d, tm, width):
    if d == 1:
        return pl.BlockSpec((None, None, tm, width), lambda b, i: (b, 0, i, 0))
    return pl.BlockSpec((None, d, tm // d, width), lambda b, i: (b, 0, i, 0))


def _rmsnorm_bf16(x, g):
    ms = jnp.mean(x * x, axis=-1, keepdims=True)
    return (x * lax.rsqrt(ms + EPS) * g).astype(BF16)


def _dot(a, b):
    return jnp.dot(a, b, preferred_element_type=F32)


def _even_in_kernel(x_ref, g_ref, w_ref, seg_ref, qg_ref, kg_ref, p4_ref, p16_ref,
                    a_ref, q0_ref, q1_ref, q2_ref):
    h = _rmsnorm_bf16(x_ref[...], g_ref[...])
    tm = h.shape[0]

    def proj(hh, c):
        return _dot(hh, w_ref[:, c * GROUP_W:(c + 1) * GROUP_W])

    a_ref[...] = proj(h, 0) * jax.nn.sigmoid(proj(h, 1))
    seg = seg_ref[...]
    for g, (d, perm_ref, out_ref) in enumerate(zip(DILATIONS, (None, p4_ref, p16_ref), (q0_ref, q1_ref, q2_ref))):
        if d == 1:
            hg = h
        else:
            hg = jnp.concatenate([_dot(perm_ref[...], h[s:s + PERM]).astype(BF16) for s in range(0, tm, PERM)], axis=0)
        for part in range(3):
            p = proj(hg, 2 + part * N_GROUPS + g)
            if part < 2:
                gain = qg_ref[...] if part == 0 else kg_ref[...]
                sq = (p * p).astype(BF16)
                ms = jnp.concatenate([_dot(sq[:, c:c + SEG_W], seg) for c in range(0, GROUP_W, SEG_W)], axis=1)
                ms = ms * (1.0 / HEAD_DIM)
                p = p * lax.rsqrt(ms + EPS) * gain
            val = p.astype(BF16)
            cols = slice(part * GROUP_W, (part + 1) * GROUP_W)
            if d == 1:
                out_ref[:, cols] = val
            else:
                n = PERM // d
                for s in range(tm // PERM):
                    for r in range(d):
                        out_ref[r, s * n:(s + 1) * n, cols] = val[s * PERM + r * n:s * PERM + (r + 1) * n]


def _even_in(x, g, w, seg, qg, kg, p4, p16):
    batch, seq, _ = x.shape
    outs = [jax.ShapeDtypeStruct((batch, seq, CONV_A_CH), F32)]
    outs += [jax.ShapeDtypeStruct((batch, d, seq // d, GROUP_QKV), BF16) for d in DILATIONS]
    return pl.pallas_call(
        _even_in_kernel,
        grid=(batch, seq // TM),
        in_specs=[_rows(TM, D_MODEL)] + [_resident(a) for a in (g, w, seg, qg, kg, p4, p16)],
        out_specs=[_rows(TM, CONV_A_CH)] + [_classes(d, TM, GROUP_QKV) for d in DILATIONS],
        out_shape=outs,
        compiler_params=_params(2),
        name="even_in",
    )(x, g, w, seg, qg, kg, p4, p16)


def _conv_a_kernel(prev_ref, a_ref, next_ref, wb_ref, b_ref, lg_ref, lb_ref, o_ref, buf_ref, sh_ref, *, n_tiles):
    i = pl.program_id(1)
    ts = a_ref.shape[0]
    h = CONV_A_HALO
    rows = ts + 2 * h
    buf_ref[0:h, :] = jnp.where(i > 0, prev_ref[...], 0.0)
    buf_ref[h:h + ts, :] = a_ref[...]
    buf_ref[h + ts:rows, :] = jnp.where(i < n_tiles - 1, next_ref[...], 0.0)
    n_sh = rows - SUBLANES
    for v in range(SUBLANES):
        sh_ref[v] = buf_ref[v:v + n_sh, :].reshape(n_sh // SUBLANES, SUBLANES, CONV_A_CH)
    off = h - (CONV_A_K - 1) // 2
    nb = CONV_A_RB // SUBLANES
    for r in range(0, ts, CONV_A_RB):
        t0 = r // SUBLANES
        acc = None
        for v in range(SUBLANES):
            us = [u for u in range((CONV_A_K + off) // SUBLANES + 1) if 0 <= SUBLANES * u + v - off < CONV_A_K]
            x = sh_ref[v, t0 + us[0]:t0 + us[-1] + nb]
            for u in us:
                term = x[u - us[0]:u - us[0] + nb] * wb_ref[SUBLANES * u + v - off]
                acc = term if acc is None else acc + term
        acc = acc.reshape(CONV_A_RB, CONV_A_CH) + b_ref[...]
        mu = jnp.mean(acc, axis=-1, keepdims=True)
        xc = acc - mu
        var = jnp.mean(xc * xc, axis=-1, keepdims=True)
        y = xc * lax.rsqrt(var + EPS) * lg_ref[...] + lb_ref[...]
        o_ref[r:r + CONV_A_RB, :] = (y * jax.nn.sigmoid(y)).astype(BF16)


def _conv_a(a, wb, b, lg, lb):
    batch, seq, _ = a.shape
    ts = CONV_A_TS
    n_tiles = seq // ts
    hb = ts // CONV_A_HALO
    rows = ts + 2 * CONV_A_HALO
    main = _rows(ts, CONV_A_CH)
    prev = pl.BlockSpec((None, CONV_A_HALO, CONV_A_CH), lambda bi, i: (bi, jnp.maximum(i * hb - 1, 0), 0))
    nxt = pl.BlockSpec((None, CONV_A_HALO, CONV_A_CH),
                       lambda bi, i: (bi, jnp.minimum((i + 1) * hb, seq // CONV_A_HALO - 1), 0))
    return pl.pallas_call(
        functools.partial(_conv_a_kernel, n_tiles=n_tiles),
        grid=(batch, n_tiles),
        in_specs=[prev, main, nxt, _resident(wb), _resident(b), _resident(lg), _resident(lb)],
        out_specs=main,
        out_shape=jax.ShapeDtypeStruct((batch, seq, CONV_A_CH), BF16),
        scratch_shapes=[pltpu.VMEM((rows, CONV_A_CH), F32),
                        pltpu.VMEM((SUBLANES, rows // SUBLANES - 1, SUBLANES, CONV_A_CH), F32)],
        compiler_params=_params(2),
        name="conv_a",
    )(a, a, a, wb, b, lg, lb)


def _attn_kernel(*refs, n_tiles, halo):
    if halo:
        q_ref, kp_ref, kc_ref, kn_ref, vp_ref, vc_ref, vn_ref, bm_ref, o_ref, l_ref, kw_ref, vw_ref = refs
        edges = ((kw_ref, kp_ref, kn_ref), (vw_ref, vp_ref, vn_ref))
    else:
        q_ref, kc_ref, vc_ref, bm_ref, o_ref, l_ref, kw_ref, vw_ref = refs
        edges = ((kw_ref, None, None), (vw_ref, None, None))
    i = pl.program_id(2)
    n_cls, tq = q_ref.shape[0], q_ref.shape[1]
    lane = lax.broadcasted_iota(jnp.int32, (1, LANES), 1)
    first_head = lane < HEAD_DIM
    keep_first = first_head.astype(BF16)
    keep_second = 1.0 - keep_first
    n_sub = tq // ATT_SUB
    lanes = [slice(pr * LANES, (pr + 1) * LANES) for pr in range(N_PAIRS)]
    for c in range(n_cls):
        for (dst, before, after), cur in zip(edges, (kc_ref, vc_ref)):
            pad = jnp.zeros((HALF, GROUP_W), BF16)
            dst[c, 0:HALF, :] = pad if before is None else before[c]
            dst[c, HALF:HALF + tq, :] = cur[c]
            dst[c, HALF + tq:2 * HALF + tq, :] = pad if after is None else after[c]
        for s in range(n_sub):
            r0 = s * ATT_SUB
            edge_lo = edge_hi = None
            if s == 0:
                edge_lo = jnp.where(lane >= jnp.where(i == 0, HALF, 0), 0.0, NEG_INF)
            if s == n_sub - 1:
                edge_hi = jnp.where(lane < jnp.where(i == n_tiles - 1, LANES - HALF, LANES), 0.0, NEG_INF)
            scs = []
            for pr, cs in enumerate(lanes):
                q2 = q_ref[c, r0:r0 + ATT_SUB, cs]
                qs = jnp.concatenate([q2 * keep_first, q2 * keep_second], axis=0)
                sc = lax.dot_general(qs, kw_ref[c, r0:r0 + ATT_KEYS, cs], (((1,), (1,)), ((), ())),
                                     preferred_element_type=F32) + bm_ref[pr]
                if edge_lo is not None or edge_hi is not None:
                    left, right = sc[:, :LANES], sc[:, LANES:]
                    left = left if edge_lo is None else left + edge_lo
                    right = right if edge_hi is None else right + edge_hi
                    sc = jnp.concatenate([left, right], axis=1)
                scs.append(sc)
            ms = [jnp.max(sc, axis=-1, keepdims=True) for sc in scs]
            ps = [jnp.exp2(sc - m) for sc, m in zip(scs, ms)]
            dens = [jnp.sum(p, axis=-1, keepdims=True) for p in ps]
            pvs = [_dot(p.astype(BF16), vw_ref[c, r0:r0 + ATT_KEYS, cs]) for p, cs in zip(ps, lanes)]
            stat_rows = jnp.zeros((ATT_SUB, LANES), F32)
            for pr, (cs, pv, m, den) in enumerate(zip(lanes, pvs, ms, dens)):
                o_ref[c, r0:r0 + ATT_SUB, cs] = jnp.where(first_head, pv[0:ATT_SUB], pv[ATT_SUB:]).astype(BF16)
                for hd, rows in ((2 * pr, slice(0, ATT_SUB)), (2 * pr + 1, slice(ATT_SUB, 2 * ATT_SUB))):
                    both = jnp.where(lane < HEADS_PER_GROUP, m[rows], den[rows])
                    stat_rows = jnp.where((lane == hd) | (lane == HEADS_PER_GROUP + hd), both, stat_rows)
            packed, rest = None, stat_rows
            for t in range(STAT_PARTS):
                part = rest.astype(BF16).astype(F32)
                rest = rest - part
                part = part if t == 0 else pltpu.roll(part, t * STAT_W, 1)
                packed = part if packed is None else packed + part
            l_ref[c, r0:r0 + ATT_SUB, :] = packed.astype(BF16)


def _attention_group(qkv, bm):
    batch, d, cls_len, _ = qkv.shape
    tq = min(cls_len, ATT_STEP)
    n_tiles = cls_len // tq
    halo = n_tiles > 1
    n_cls = 1 if halo else min(d, ATT_STEP // tq)
    hb = tq // HALF

    def main(c, width=GROUP_W):
        return pl.BlockSpec((None, n_cls, tq, width), lambda b, r, i: (b, r, i, c))

    def before(c):
        return pl.BlockSpec((None, 1, HALF, GROUP_W), lambda b, r, i: (b, r, jnp.maximum(i * hb - 1, 0), c))

    def after(c):
        return pl.BlockSpec((None, 1, HALF, GROUP_W),
                            lambda b, r, i: (b, r, jnp.minimum((i + 1) * hb, cls_len // HALF - 1), c))

    if halo:
        in_specs = [main(0), before(1), main(1), after(1), before(2), main(2), after(2), _resident(bm)]
    else:
        in_specs = [main(0), main(1), main(2), _resident(bm)]
    window = pltpu.VMEM((n_cls, tq + 2 * HALF, GROUP_W), BF16)
    return pl.pallas_call(
        functools.partial(_attn_kernel, n_tiles=n_tiles, halo=halo),
        grid=(batch, d // n_cls, n_tiles),
        in_specs=in_specs,
        out_specs=[main(0), main(0, LANES)],
        out_shape=[jax.ShapeDtypeStruct((batch, d, cls_len, GROUP_W), BF16),
                   jax.ShapeDtypeStruct((batch, d, cls_len, LANES), BF16)],
        scratch_shapes=[window, window],
        compiler_params=_params(3),
        name=f"attn_d{d}",
    )(*([qkv] * (len(in_specs) - 1)), bm)


def _token_order(ref, d, perm_t):
    if d == 1:
        return ref[...].astype(F32)
    n = PERM // d
    blocks = []
    for s in range(ref.shape[1] // n):
        cm = jnp.concatenate([ref[r, s * n:(s + 1) * n, :] for r in range(d)], axis=0)
        blocks.append(_dot(perm_t, cm))
    return jnp.concatenate(blocks, axis=0)


def _ffn_tail(x, g_ref, w1_ref, w3_ref, w2_ref):
    h = _rmsnorm_bf16(x, g_ref[...])
    y = x
    for lo, hi in FFN_CHUNKS:
        a = _dot(h, w1_ref[:, lo:hi])
        b = _dot(h, w3_ref[:, lo:hi])
        y = y + _dot((a * jax.nn.sigmoid(a) * b).astype(BF16), w2_ref[lo:hi, :])
    return y


def _even_out_kernel(x_ref, a_ref, o0_ref, o1_ref, o2_ref, l0_ref, l1_ref, l2_ref, p4t_ref, p16t_ref, ex_ref,
                     w_ref, fg_ref, w1_ref, w3_ref, w2_ref, y_ref):
    perms = (None, p4t_ref[...], p16t_ref[...])
    lane = lax.broadcasted_iota(jnp.int32, (1, LANES), 1)
    stats = []
    for l_ref, d, pt in zip((l0_ref, l1_ref, l2_ref), DILATIONS, perms):
        parts = _token_order(l_ref, d, pt)
        st = parts
        for t in range(1, STAT_PARTS):
            st = st + pltpu.roll(parts, LANES - t * STAT_W, 1)
        stats.append(st)
    m = jnp.maximum(jnp.maximum(stats[0], stats[1]), stats[2])
    es = [jnp.exp2(st - m) for st in stats]
    dens = [pltpu.roll(st, LANES - HEADS_PER_GROUP, 1) for st in stats]
    inv = 1.0 / (es[0] * dens[0] + es[1] * dens[1] + es[2] * dens[2])
    packed = None
    for g, e in enumerate(es):
        al = jnp.where(lane < HEADS_PER_GROUP, e * inv, 0.0)
        al = al if g == 0 else pltpu.roll(al, g * HEADS_PER_GROUP, 1)
        packed = al if packed is None else packed + al
    terms, rest = None, packed
    for t in range(STAT_PARTS):
        part = rest.astype(BF16).astype(F32)
        rest = rest - part
        part = part if t == 0 else pltpu.roll(part, t * PACK_W, 1)
        terms = part if terms is None else terms + part
    wgt = _dot(terms.astype(BF16), ex_ref[...])
    o = None
    for g, (o_ref, d, pt) in enumerate(zip((o0_ref, o1_ref, o2_ref), DILATIONS, perms)):
        term = wgt[:, g * GROUP_W:(g + 1) * GROUP_W] * _token_order(o_ref, d, pt)
        o = term if o is None else o + term
    y_ref[...] = x_ref[...] + _dot(a_ref[...], w_ref[0:CONV_A_CH, :]) + _dot(o.astype(BF16), w_ref[CONV_A_CH:, :])
    y_ref[...] = _ffn_tail(y_ref[...], fg_ref, w1_ref, w3_ref, w2_ref)


def _even_out(x, a, os, ls, p4t, p16t, ex, w, ffn):
    batch, seq, _ = x.shape
    return pl.pallas_call(
        _even_out_kernel,
        grid=(batch, seq // TM),
        in_specs=[_rows(TM, D_MODEL), _rows(TM, CONV_A_CH)]
                 + [_classes(d, TM, GROUP_W) for d in DILATIONS] + [_classes(d, TM, LANES) for d in DILATIONS]
                 + [_resident(t) for t in (p4t, p16t, ex, w, *ffn)],
        out_specs=_rows(TM, D_MODEL),
        out_shape=jax.ShapeDtypeStruct((batch, seq, D_MODEL), F32),
        compiler_params=_params(2),
        name="even_out_ffn",
    )(x, a, *os, *ls, p4t, p16t, ex, w, *ffn)


def _odd_kernel(prev_ref, x_ref, next_ref, g_ref, win_ref, cw_ref, wout_ref, fg_ref, w1_ref, w3_ref, w2_ref,
                y_ref, buf_ref, *, n_tiles):
    i = pl.program_id(1)
    x = x_ref[...]
    tm = x.shape[0]
    hb = SUBLANES
    halo = jnp.concatenate([prev_ref[...], next_ref[...]], axis=0)
    h = jnp.concatenate([_rmsnorm_bf16(x, g_ref[...]), _rmsnorm_bf16(halo, g_ref[...])], axis=0)
    gate = _dot(h[0:tm], win_ref[:, 0:D_MODEL])
    cu = _dot(h, win_ref[:, D_MODEL:2 * D_MODEL]) * _dot(h, win_ref[:, 2 * D_MODEL:3 * D_MODEL])
    buf_ref[0:hb, :] = jnp.where(i > 0, cu[tm:tm + hb], 0.0)
    buf_ref[hb:hb + tm, :] = cu[0:tm]
    buf_ref[hb + tm:2 * hb + tm, :] = jnp.where(i < n_tiles - 1, cu[tm + hb:], 0.0)
    conv = (buf_ref[hb - 1:hb - 1 + tm, :] * cw_ref[0:1, :] + buf_ref[hb:hb + tm, :] * cw_ref[1:2, :]
            + buf_ref[hb + 1:hb + 1 + tm, :] * cw_ref[2:3, :])
    y_ref[...] = x + _dot((gate * conv).astype(BF16), wout_ref[...])
    y_ref[...] = _ffn_tail(y_ref[...], fg_ref, w1_ref, w3_ref, w2_ref)


def _odd(x, g, win, cw, wout, ffn):
    batch, seq, _ = x.shape
    n_tiles = seq // TM
    per = TM // SUBLANES
    main = _rows(TM, D_MODEL)
    prev = pl.BlockSpec((None, SUBLANES, D_MODEL), lambda bi, i: (bi, jnp.maximum(i * per - 1, 0), 0))
    nxt = pl.BlockSpec((None, SUBLANES, D_MODEL), lambda bi, i: (bi, jnp.minimum((i + 1) * per, seq // SUBLANES - 1), 0))
    return pl.pallas_call(
        functools.partial(_odd_kernel, n_tiles=n_tiles),
        grid=(batch, n_tiles),
        in_specs=[prev, main, nxt] + [_resident(t) for t in (g, win, cw, wout, *ffn)],
        out_specs=main,
        out_shape=jax.ShapeDtypeStruct((batch, seq, D_MODEL), F32),
        scratch_shapes=[pltpu.VMEM((TM + 2 * SUBLANES, D_MODEL), F32)],
        compiler_params=_params(2),
        name="odd_ffn",
    )(x, x, x, g, win, cw, wout, *ffn)


def _t5_bucket_np(rel):
    nb = N_BUCKETS // 2
    max_exact = nb // 2
    n = np.abs(rel)
    nf = np.maximum(n, 1).astype(np.float32)
    large = max_exact + (np.log(nf / max_exact) / math.log(MAX_DISTANCE / max_exact) * (nb - max_exact)).astype(np.int32)
    large = np.minimum(large, nb - 1)
    return np.where(rel > 0, nb, 0) + np.where(n < max_exact, n, large)


def _bias_windows(rel_bias):
    n_off = ATT_SUB + ATT_KEYS - 1
    off = np.arange(n_off) - (ATT_SUB - 1) - HALF
    band = np.abs(off) <= HALF
    out = []
    for g, d in enumerate(DILATIONS):
        tab = rel_bias[:, g * HEADS_PER_GROUP:(g + 1) * HEADS_PER_GROUP].astype(F32)
        vec = jnp.take(tab, jnp.asarray(_t5_bucket_np(off * d)), axis=0)
        vec = jnp.where(band[:, None], vec * LOG2E, NEG_INF).T
        flat = jnp.tile(jnp.pad(vec, ((0, 0), (0, 1))), (1, ATT_SUB))[:, :ATT_SUB * n_off]
        skew = flat.reshape(HEADS_PER_GROUP, ATT_SUB, n_off)
        win = skew[:, :, ATT_SUB - 1:ATT_SUB - 1 + ATT_KEYS]
        out.append(win.reshape(N_PAIRS, 2 * ATT_SUB, ATT_KEYS))
    return out


def _head_segments():
    h = np.arange(SEG_W) // HEAD_DIM
    return jnp.asarray(h[:, None] == h[None, :], BF16)


def _head_expand():
    src = np.arange(LANES)
    col = np.arange(N_GROUPS * GROUP_W)
    hit = (src[:, None] % PACK_W == (col // GROUP_W) * HEADS_PER_GROUP + (col % GROUP_W) // HEAD_DIM)[:, :]
    return jnp.asarray(hit & (src[:, None] < STAT_PARTS * PACK_W), BF16)


def _class_perm(d):
    n = PERM // d
    src = (np.arange(PERM) % n) * d + np.arange(PERM) // n
    return np.arange(PERM)[None, :] == src[:, None]


def _trunk(x, p):
    for layer in range(DEPTH):
        i = layer // 2
        ffn = (p["ffn_norm"][layer], p["ffn_w1"][layer], p["ffn_w3"][layer], p["ffn_w2"][layer])
        if layer % 2 == 0:
            a, *qkvs = _even_in(x, p["even_norm"][i], p["even_w_in"][i], p["seg"], p["q_gain"][i], p["k_gain"][i],
                                p["perm"][1], p["perm"][2])
            a = _conv_a(a, p["conv_a_w"][i], p["conv_a_b"][i], p["conv_a_ln_g"][i], p["conv_a_ln_b"][i])
            os, ls = zip(*[_attention_group(qkv, bm) for qkv, bm in zip(qkvs, p["bias"])])
            x = _even_out(x, a, os, ls, p["perm_t"][1], p["perm_t"][2], p["expand"], p["even_w_out"][i], ffn)
        else:
            x = _odd(x, p["odd_norm"][i], p["odd_w_in"][i], p["conv_c_w"][i], p["odd_w_out"][i], ffn)
    return x


def _prepare(rel_bias, even_norm, even_w_in, conv_a_w, conv_a_b, conv_a_ln_g, conv_a_ln_b, q_norm, k_norm,
             even_w_out, odd_norm, odd_w_in, conv_c_w, odd_w_out, ffn_norm, ffn_w1, ffn_w3, ffn_w2):
    row = lambda a: a.astype(F32)[:, None, :]
    tile_heads = lambda a: jnp.tile(a.astype(F32), (1, HEADS_PER_GROUP))[:, None, :]
    perms = [None if d == 1 else _class_perm(d) for d in DILATIONS]
    return dict(
        bias=_bias_windows(rel_bias), seg=_head_segments(), expand=_head_expand(),
        perm=[None if m is None else jnp.asarray(m, BF16) for m in perms],
        perm_t=[None if m is None else jnp.asarray(m.T, BF16) for m in perms],
        even_norm=row(even_norm), even_w_in=even_w_in.astype(BF16),
        conv_a_w=jnp.broadcast_to(conv_a_w.astype(F32)[:, :, None, :], conv_a_w.shape[:2] + (SUBLANES, CONV_A_CH)),
        conv_a_b=row(conv_a_b), conv_a_ln_g=row(conv_a_ln_g), conv_a_ln_b=row(conv_a_ln_b),
        q_gain=tile_heads(q_norm) * (HEAD_DIM ** -0.5 * LOG2E), k_gain=tile_heads(k_norm),
        even_w_out=even_w_out.astype(BF16),
        odd_norm=row(odd_norm), odd_w_in=odd_w_in.astype(BF16), conv_c_w=conv_c_w.astype(F32),
        odd_w_out=odd_w_out.astype(BF16),
        ffn_norm=row(ffn_norm), ffn_w1=ffn_w1.astype(BF16), ffn_w3=ffn_w3.astype(BF16), ffn_w2=ffn_w2.astype(BF16),
    )


def kernel(x_prompt, x_sample, rel_bias, even_norm, even_w_in, conv_a_w, conv_a_b, conv_a_ln_g, conv_a_ln_b, q_norm, k_norm, even_w_out, odd_norm, odd_w_in, conv_c_w, odd_w_out, ffn_norm, ffn_w1, ffn_w3, ffn_w2):
    p = _prepare(rel_bias, even_norm, even_w_in, conv_a_w, conv_a_b, conv_a_ln_g, conv_a_ln_b, q_norm, k_norm,
                 even_w_out, odd_norm, odd_w_in, conv_c_w, odd_w_out, ffn_norm, ffn_w1, ffn_w3, ffn_w2)
    return (_trunk(x_prompt, p), _trunk(x_sample, p))
```

```python
import functools
import math

import numpy as np
import jax
import jax.numpy as jnp
from jax import lax
from jax.experimental import pallas as pl
from jax.experimental.pallas import tpu as pltpu

F32 = jnp.float32
BF16 = jnp.bfloat16

D_MODEL = 1024
DEPTH = 4
HEAD_DIM = 64
WINDOWS = (128, 512, 2048)
DILATIONS = (1, 4, 16)
N_GROUPS = 3
HEADS_PER_GROUP = 8
GROUP_W = HEADS_PER_GROUP * HEAD_DIM
CONV_A_CH = D_MODEL // 2
CONV_A_K = 31
FFN_HIDDEN = 2816
N_BUCKETS = 32
MAX_DISTANCE = 1024
EPS = 1e-6
NEG_INF = -1e30
LOG2E = math.log2(math.e)
HALF = 64
assert all(w // (2 * d) == HALF for w, d in zip(WINDOWS, DILATIONS))

V7X_VMEM_BYTES = 64 * 1024 * 1024
VMEM_LIMIT_BYTES = V7X_VMEM_BYTES * 7 // 8
LANES = 128
SUBLANES = 8
MXU_DIM = 256

TM = 512
PERM = MXU_DIM
SEG_W = MXU_DIM
STAT_W = 2 * HEADS_PER_GROUP
STAT_PARTS = 3
ATT_SUB = 128
ATT_KEYS = ATT_SUB + 2 * HALF
ATT_STEP = 1024
assert ATT_KEYS == 2 * LANES
N_PAIRS = GROUP_W // LANES
CONV_A_TS = 512
CONV_A_RB = 32
CONV_A_HALO = 16
FFN_CHUNKS = ((0, 1536), (1536, 2816))
PACK_W = N_GROUPS * HEADS_PER_GROUP


def _params(n_axes):
    return pltpu.CompilerParams(dimension_semantics=("parallel",) * n_axes,
                                vmem_limit_bytes=VMEM_LIMIT_BYTES)


def _resident(a):
    if isinstance(a, tuple):
        a, layer = a
        nd = a.ndim
        return pl.BlockSpec((None,) + a.shape[1:], lambda *_: (layer,) + (0,) * (nd - 1), pipeline_mode=pl.Buffered(1))
    nd = a.ndim
    return pl.BlockSpec(a.shape, lambda *_: (0,) * nd, pipeline_mode=pl.Buffered(1))


def _arrays(params):
    return [a[0] if isinstance(a, tuple) else a for a in params]


def _rows(tm, width):
    return pl.BlockSpec((None, tm, width), lambda b, i: (b, i, 0))


def _classes(d, tm, width):
    if d == 1:
        return pl.BlockSpec((None, None, tm, width), lambda b, i: (b, 0, i, 0))
    return pl.BlockSpec((None, d, tm // d, width), lambda b, i: (b, 0, i, 0))


def _qkv_tile(d, tm):
    if d == 1:
        return pl.BlockSpec((None, None, 3, tm, GROUP_W), lambda b, i: (b, 0, 0, i, 0))
    return pl.BlockSpec((None, d, 3, tm // d, GROUP_W), lambda b, i: (b, 0, 0, i, 0))


def _rmsnorm_bf16(x, g):
    ms = jnp.mean(x * x, axis=-1, keepdims=True)
    return (x * lax.rsqrt(ms + EPS) * g).astype(BF16)


def _dot(a, b):
    return jnp.dot(a, b, preferred_element_type=F32)


def _even_in_kernel(x_ref, g_ref, w_ref, seg_ref, qg_ref, kg_ref, p4_ref, p16_ref,
                    a_ref, q0_ref, q1_ref, q2_ref):
    h = _rmsnorm_bf16(x_ref[...], g_ref[...])
    tm = h.shape[0]

    def proj(hh, c):
        return _dot(hh, w_ref[:, c * GROUP_W:(c + 1) * GROUP_W])

    a_ref[...] = proj(h, 0) * jax.nn.sigmoid(proj(h, 1))
    seg = seg_ref[...]
    for g, (d, perm_ref, out_ref) in enumerate(zip(DILATIONS, (None, p4_ref, p16_ref), (q0_ref, q1_ref, q2_ref))):
        if d == 1:
            hg = h
        else:
            hg = jnp.concatenate([_dot(perm_ref[...], h[s:s + PERM]).astype(BF16) for s in range(0, tm, PERM)], axis=0)
        for part in range(3):
            p = proj(hg, 2 + part * N_GROUPS + g)
            if part < 2:
                gain = qg_ref[...] if part == 0 else kg_ref[...]
                sq = (p * p).astype(BF16)
                ms = jnp.concatenate([_dot(sq[:, c:c + SEG_W], seg) for c in range(0, GROUP_W, SEG_W)], axis=1)
                ms = ms * (1.0 / HEAD_DIM)
                p = p * lax.rsqrt(ms + EPS) * gain
            val = p.astype(BF16)
            if d == 1:
                out_ref[part] = val
            else:
                n = PERM // d
                for s in range(tm // PERM):
                    for r in range(d):
                        out_ref[r, part, s * n:(s + 1) * n, :] = val[s * PERM + r * n:s * PERM + (r + 1) * n]


def _even_in(x, g, w, seg, qg, kg, p4, p16):
    batch, seq, _ = x.shape
    outs = [jax.ShapeDtypeStruct((batch, seq, CONV_A_CH), F32)]
    outs += [jax.ShapeDtypeStruct((batch, d, 3, seq // d, GROUP_W), BF16) for d in DILATIONS]
    return pl.pallas_call(
        _even_in_kernel,
        grid=(batch, seq // TM),
        in_specs=[_rows(TM, D_MODEL)] + [_resident(a) for a in (g, w, seg, qg, kg, p4, p16)],
        out_specs=[_rows(TM, CONV_A_CH)] + [_qkv_tile(d, TM) for d in DILATIONS],
        out_shape=outs,
        compiler_params=_params(2),
        name="even_in",
    )(x, *_arrays((g, w, seg, qg, kg, p4, p16)))


def _conv_a_kernel(prev_ref, a_ref, next_ref, wb_ref, b_ref, lg_ref, lb_ref, o_ref, buf_ref, sh_ref, *, n_tiles):
    i = pl.program_id(1)
    ts = a_ref.shape[0]
    h = CONV_A_HALO
    rows = ts + 2 * h
    buf_ref[0:h, :] = jnp.where(i > 0, prev_ref[...], 0.0)
    buf_ref[h:h + ts, :] = a_ref[...]
    buf_ref[h + ts:rows, :] = jnp.where(i < n_tiles - 1, next_ref[...], 0.0)
    n_sh = rows - SUBLANES
    for v in range(SUBLANES):
        sh_ref[v] = buf_ref[v:v + n_sh, :].reshape(n_sh // SUBLANES, SUBLANES, CONV_A_CH)
    off = h - (CONV_A_K - 1) // 2
    nb = CONV_A_RB // SUBLANES
    for r in range(0, ts, CONV_A_RB):
        t0 = r // SUBLANES
        acc = None
        for v in range(SUBLANES):
            us = [u for u in range((CONV_A_K + off) // SUBLANES + 1) if 0 <= SUBLANES * u + v - off < CONV_A_K]
            x = sh_ref[v, t0 + us[0]:t0 + us[-1] + nb]
            for u in us:
                term = x[u - us[0]:u - us[0] + nb] * wb_ref[SUBLANES * u + v - off]
                acc = term if acc is None else acc + term
        acc = acc.reshape(CONV_A_RB, CONV_A_CH) + b_ref[...]
        mu = jnp.mean(acc, axis=-1, keepdims=True)
        xc = acc - mu
        var = jnp.mean(xc * xc, axis=-1, keepdims=True)
        y = xc * lax.rsqrt(var + EPS) * lg_ref[...] + lb_ref[...]
        o_ref[r:r + CONV_A_RB, :] = (y * jax.nn.sigmoid(y)).astype(BF16)


def _conv_a(a, wb, b, lg, lb):
    batch, seq, _ = a.shape
    ts = CONV_A_TS
    n_tiles = seq // ts
    hb = ts // CONV_A_HALO
    rows = ts + 2 * CONV_A_HALO
    main = _rows(ts, CONV_A_CH)
    prev = pl.BlockSpec((None, CONV_A_HALO, CONV_A_CH), lambda bi, i: (bi, jnp.maximum(i * hb - 1, 0), 0))
    nxt = pl.BlockSpec((None, CONV_A_HALO, CONV_A_CH),
                       lambda bi, i: (bi, jnp.minimum((i + 1) * hb, seq // CONV_A_HALO - 1), 0))
    return pl.pallas_call(
        functools.partial(_conv_a_kernel, n_tiles=n_tiles),
        grid=(batch, n_tiles),
        in_specs=[prev, main, nxt] + [_resident(t) for t in (wb, b, lg, lb)],
        out_specs=main,
        out_shape=jax.ShapeDtypeStruct((batch, seq, CONV_A_CH), BF16),
        scratch_shapes=[pltpu.VMEM((rows, CONV_A_CH), F32),
                        pltpu.VMEM((SUBLANES, rows // SUBLANES - 1, SUBLANES, CONV_A_CH), F32)],
        compiler_params=_params(2),
        name="conv_a",
    )(a, a, a, *_arrays((wb, b, lg, lb)))


def _attn_kernel(*refs, n_tiles, halo):
    if halo:
        q_ref, kp_ref, kc_ref, kn_ref, vp_ref, vc_ref, vn_ref, bm_ref, o_ref, l_ref, kw_ref, vw_ref = refs
        edges = ((kw_ref, kp_ref, kn_ref), (vw_ref, vp_ref, vn_ref))
    else:
        q_ref, kc_ref, vc_ref, bm_ref, o_ref, l_ref, kw_ref, vw_ref = refs
        edges = ((kw_ref, None, None), (vw_ref, None, None))
    i = pl.program_id(2)
    n_cls, tq = q_ref.shape[0], q_ref.shape[1]
    lane = lax.broadcasted_iota(jnp.int32, (1, LANES), 1)
    first_head = lane < HEAD_DIM
    keep_first = first_head.astype(BF16)
    keep_second = 1.0 - keep_first
    n_sub = tq // ATT_SUB
    lanes = [slice(pr * LANES, (pr + 1) * LANES) for pr in range(N_PAIRS)]
    for c in range(n_cls):
        for (dst, before, after), cur in zip(edges, (kc_ref, vc_ref)):
            pad = jnp.zeros((HALF, GROUP_W), BF16)
            dst[c, 0:HALF, :] = pad if before is None else before[c]
            dst[c, HALF:HALF + tq, :] = cur[c]
            dst[c, HALF + tq:2 * HALF + tq, :] = pad if after is None else after[c]
        for s in range(n_sub):
            r0 = s * ATT_SUB
            edge_lo = edge_hi = None
            if s == 0:
                edge_lo = jnp.where(lane >= jnp.where(i == 0, HALF, 0), 0.0, NEG_INF)
            if s == n_sub - 1:
                edge_hi = jnp.where(lane < jnp.where(i == n_tiles - 1, LANES - HALF, LANES), 0.0, NEG_INF)
            scs = []
            for pr, cs in enumerate(lanes):
                q2 = q_ref[c, r0:r0 + ATT_SUB, cs]
                qs = jnp.concatenate([q2 * keep_first, q2 * keep_second], axis=0)
                sc = lax.dot_general(qs, kw_ref[c, r0:r0 + ATT_KEYS, cs], (((1,), (1,)), ((), ())),
                                     preferred_element_type=F32) + bm_ref[pr]
                if edge_lo is not None or edge_hi is not None:
                    left, right = sc[:, :LANES], sc[:, LANES:]
                    left = left if edge_lo is None else left + edge_lo
                    right = right if edge_hi is None else right + edge_hi
                    sc = jnp.concatenate([left, right], axis=1)
                scs.append(sc)
            ms = [jnp.max(sc, axis=-1, keepdims=True) for sc in scs]
            ps = [jnp.exp2(sc - m) for sc, m in zip(scs, ms)]
            dens = [jnp.sum(p, axis=-1, keepdims=True) for p in ps]
            pvs = [_dot(p.astype(BF16), vw_ref[c, r0:r0 + ATT_KEYS, cs]) for p, cs in zip(ps, lanes)]
            stat_rows = jnp.zeros((ATT_SUB, LANES), F32)
            for pr, (cs, pv, m, den) in enumerate(zip(lanes, pvs, ms, dens)):
                o_ref[c, r0:r0 + ATT_SUB, cs] = jnp.where(first_head, pv[0:ATT_SUB], pv[ATT_SUB:]).astype(BF16)
                for hd, rows in ((2 * pr, slice(0, ATT_SUB)), (2 * pr + 1, slice(ATT_SUB, 2 * ATT_SUB))):
                    both = jnp.where(lane < HEADS_PER_GROUP, m[rows], den[rows])
                    stat_rows = jnp.where((lane == hd) | (lane == HEADS_PER_GROUP + hd), both, stat_rows)
            packed, rest = None, stat_rows
            for t in range(STAT_PARTS):
                part = rest.astype(BF16).astype(F32)
                rest = rest - part
                part = part if t == 0 else pltpu.roll(part, t * STAT_W, 1)
                packed = part if packed is None else packed + part
            l_ref[c, r0:r0 + ATT_SUB, :] = packed.astype(BF16)


def _attention_group(qkv, bm):
    batch, d, _, cls_len, _ = qkv.shape
    tq = min(cls_len, ATT_STEP)
    n_tiles = cls_len // tq
    halo = n_tiles > 1
    n_cls = 1 if halo else min(d, ATT_STEP // tq)
    hb = tq // HALF

    def main(part):
        return pl.BlockSpec((None, n_cls, None, tq, GROUP_W), lambda b, r, i: (b, r, part, i, 0))

    def before(part):
        return pl.BlockSpec((None, 1, None, HALF, GROUP_W),
                            lambda b, r, i: (b, r, part, jnp.maximum(i * hb - 1, 0), 0))

    def after(part):
        return pl.BlockSpec((None, 1, None, HALF, GROUP_W),
                            lambda b, r, i: (b, r, part, jnp.minimum((i + 1) * hb, cls_len // HALF - 1), 0))

    def out(width):
        return pl.BlockSpec((None, n_cls, tq, width), lambda b, r, i: (b, r, i, 0))

    if halo:
        in_specs = [main(0), before(1), main(1), after(1), before(2), main(2), after(2), _resident(bm)]
    else:
        in_specs = [main(0), main(1), main(2), _resident(bm)]
    window = pltpu.VMEM((n_cls, tq + 2 * HALF, GROUP_W), BF16)
    return pl.pallas_call(
        functools.partial(_attn_kernel, n_tiles=n_tiles, halo=halo),
        grid=(batch, d // n_cls, n_tiles),
        in_specs=in_specs,
        out_specs=[out(GROUP_W), out(LANES)],
        out_shape=[jax.ShapeDtypeStruct((batch, d, cls_len, GROUP_W), BF16),
                   jax.ShapeDtypeStruct((batch, d, cls_len, LANES), BF16)],
        scratch_shapes=[window, window],
        compiler_params=_params(3),
        name=f"attn_d{d}",
    )(*([qkv] * (len(in_specs) - 1)), bm)


def _token_order(ref, d, perm_t):
    if d == 1:
        return ref[...].astype(F32)
    n = PERM // d
    blocks = []
    for s in range(ref.shape[1] // n):
        cm = jnp.concatenate([ref[r, s * n:(s + 1) * n, :] for r in range(d)], axis=0)
        blocks.append(_dot(perm_t, cm))
    return jnp.concatenate(blocks, axis=0)


def _ffn_tail(x, g_ref, w1_ref, w3_ref, w2_ref):
    h = _rmsnorm_bf16(x, g_ref[...])
    y = x
    for lo, hi in FFN_CHUNKS:
        a = _dot(h, w1_ref[:, lo:hi])
        b = _dot(h, w3_ref[:, lo:hi])
        y = y + _dot((a * jax.nn.sigmoid(a) * b).astype(BF16), w2_ref[lo:hi, :])
    return y


def _even_out_kernel(x_ref, a_ref, o0_ref, o1_ref, o2_ref, l0_ref, l1_ref, l2_ref, p4t_ref, p16t_ref, ex_ref,
                     w_ref, fg_ref, w1_ref, w3_ref, w2_ref, y_ref):
    perms = (None, p4t_ref[...], p16t_ref[...])
    lane = lax.broadcasted_iota(jnp.int32, (1, LANES), 1)
    stats = []
    for l_ref, d, pt in zip((l0_ref, l1_ref, l2_ref), DILATIONS, perms):
        parts = _token_order(l_ref, d, pt)
        st = parts
        for t in range(1, STAT_PARTS):
            st = st + pltpu.roll(parts, LANES - t * STAT_W, 1)
        stats.append(st)
    m = jnp.maximum(jnp.maximum(stats[0], stats[1]), stats[2])
    es = [jnp.exp2(st - m) for st in stats]
    dens = [pltpu.roll(st, LANES - HEADS_PER_GROUP, 1) for st in stats]
    inv = 1.0 / (es[0] * dens[0] + es[1] * dens[1] + es[2] * dens[2])
    packed = None
    for g, e in enumerate(es):
        al = jnp.where(lane < HEADS_PER_GROUP, e * inv, 0.0)
        al = al if g == 0 else pltpu.roll(al, g * HEADS_PER_GROUP, 1)
        packed = al if packed is None else packed + al
    terms, rest = None, packed
    for t in range(STAT_PARTS):
        part = rest.astype(BF16).astype(F32)
        rest = rest - part
        part = part if t == 0 else pltpu.roll(part, t * PACK_W, 1)
        terms = part if terms is None else terms + part
    wgt = _dot(terms.astype(BF16), ex_ref[...])
    o = None
    for g, (o_ref, d, pt) in enumerate(zip((o0_ref, o1_ref, o2_ref), DILATIONS, perms)):
        term = wgt[:, g * GROUP_W:(g + 1) * GROUP_W] * _token_order(o_ref, d, pt)
        o = term if o is None else o + term
    y_ref[...] = x_ref[...] + _dot(a_ref[...], w_ref[0:CONV_A_CH, :]) + _dot(o.astype(BF16), w_ref[CONV_A_CH:, :])
    y_ref[...] = _ffn_tail(y_ref[...], fg_ref, w1_ref, w3_ref, w2_ref)


def _even_out(x, a, os, ls, p4t, p16t, ex, w, ffn):
    batch, seq, _ = x.shape
    return pl.pallas_call(
        _even_out_kernel,
        grid=(batch, seq // TM),
        in_specs=[_rows(TM, D_MODEL), _rows(TM, CONV_A_CH)]
                 + [_classes(d, TM, GROUP_W) for d in DILATIONS] + [_classes(d, TM, LANES) for d in DILATIONS]
                 + [_resident(t) for t in (p4t, p16t, ex, w, *ffn)],
        out_specs=_rows(TM, D_MODEL),
        out_shape=jax.ShapeDtypeStruct((batch, seq, D_MODEL), F32),
        compiler_params=_params(2),
        name="even_out_ffn",
    )(x, a, *os, *ls, *_arrays((p4t, p16t, ex, w, *ffn)))


def _odd_kernel(prev_ref, x_ref, next_ref, g_ref, win_ref, cw_ref, wout_ref, fg_ref, w1_ref, w3_ref, w2_ref,
                y_ref, buf_ref, *, n_tiles):
    i = pl.program_id(1)
    x = x_ref[...]
    tm = x.shape[0]
    hb = SUBLANES
    halo = jnp.concatenate([prev_ref[...], next_ref[...]], axis=0)
    h = jnp.concatenate([_rmsnorm_bf16(x, g_ref[...]), _rmsnorm_bf16(halo, g_ref[...])], axis=0)
    gate = _dot(h[0:tm], win_ref[:, 0:D_MODEL])
    cu = _dot(h, win_ref[:, D_MODEL:2 * D_MODEL]) * _dot(h, win_ref[:, 2 * D_MODEL:3 * D_MODEL])
    buf_ref[0:hb, :] = jnp.where(i > 0, cu[tm:tm + hb], 0.0)
    buf_ref[hb:hb + tm, :] = cu[0:tm]
    buf_ref[hb + tm:2 * hb + tm, :] = jnp.where(i < n_tiles - 1, cu[tm + hb:], 0.0)
    conv = (buf_ref[hb - 1:hb - 1 + tm, :] * cw_ref[0:1, :] + buf_ref[hb:hb + tm, :] * cw_ref[1:2, :]
            + buf_ref[hb + 1:hb + 1 + tm, :] * cw_ref[2:3, :])
    y_ref[...] = x + _dot((gate * conv).astype(BF16), wout_ref[...])
    y_ref[...] = _ffn_tail(y_ref[...], fg_ref, w1_ref, w3_ref, w2_ref)


def _odd(x, g, win, cw, wout, ffn):
    batch, seq, _ = x.shape
    n_tiles = seq // TM
    per = TM // SUBLANES
    main = _rows(TM, D_MODEL)
    prev = pl.BlockSpec((None, SUBLANES, D_MODEL), lambda bi, i: (bi, jnp.maximum(i * per - 1, 0), 0))
    nxt = pl.BlockSpec((None, SUBLANES, D_MODEL), lambda bi, i: (bi, jnp.minimum((i + 1) * per, seq // SUBLANES - 1), 0))
    return pl.pallas_call(
        functools.partial(_odd_kernel, n_tiles=n_tiles),
        grid=(batch, n_tiles),
        in_specs=[prev, main, nxt] + [_resident(t) for t in (g, win, cw, wout, *ffn)],
        out_specs=main,
        out_shape=jax.ShapeDtypeStruct((batch, seq, D_MODEL), F32),
        scratch_shapes=[pltpu.VMEM((TM + 2 * SUBLANES, D_MODEL), F32)],
        compiler_params=_params(2),
        name="odd_ffn",
    )(x, x, x, *_arrays((g, win, cw, wout, *ffn)))


def _t5_bucket_np(rel):
    nb = N_BUCKETS // 2
    max_exact = nb // 2
    n = np.abs(rel)
    nf = np.maximum(n, 1).astype(np.float32)
    large = max_exact + (np.log(nf / max_exact) / math.log(MAX_DISTANCE / max_exact) * (nb - max_exact)).astype(np.int32)
    large = np.minimum(large, nb - 1)
    return np.where(rel > 0, nb, 0) + np.where(n < max_exact, n, large)


def _bias_windows(rel_bias):
    n_off = ATT_SUB + ATT_KEYS - 1
    off = np.arange(n_off) - (ATT_SUB - 1) - HALF
    band = np.abs(off) <= HALF
    out = []
    for g, d in enumerate(DILATIONS):
        tab = rel_bias[:, g * HEADS_PER_GROUP:(g + 1) * HEADS_PER_GROUP].astype(F32)
        vec = jnp.take(tab, jnp.asarray(_t5_bucket_np(off * d)), axis=0)
        vec = jnp.where(band[:, None], vec * LOG2E, NEG_INF).T
        flat = jnp.tile(jnp.pad(vec, ((0, 0), (0, 1))), (1, ATT_SUB))[:, :ATT_SUB * n_off]
        skew = flat.reshape(HEADS_PER_GROUP, ATT_SUB, n_off)
        win = skew[:, :, ATT_SUB - 1:ATT_SUB - 1 + ATT_KEYS]
        out.append(win.reshape(N_PAIRS, 2 * ATT_SUB, ATT_KEYS))
    return out


def _head_segments():
    h = np.arange(SEG_W) // HEAD_DIM
    return jnp.asarray(h[:, None] == h[None, :], BF16)


def _head_expand():
    src = np.arange(LANES)
    col = np.arange(N_GROUPS * GROUP_W)
    hit = (src[:, None] % PACK_W == (col // GROUP_W) * HEADS_PER_GROUP + (col % GROUP_W) // HEAD_DIM)[:, :]
    return jnp.asarray(hit & (src[:, None] < STAT_PARTS * PACK_W), BF16)


def _class_perm(d):
    n = PERM // d
    src = (np.arange(PERM) % n) * d + np.arange(PERM) // n
    return np.arange(PERM)[None, :] == src[:, None]


def _trunk(x, p):
    for layer in range(DEPTH):
        i = layer // 2
        ffn = tuple((p[k], layer) for k in ("ffn_norm", "ffn_w1", "ffn_w3", "ffn_w2"))
        if layer % 2 == 0:
            a, *qkvs = _even_in(x, (p["even_norm"], i), (p["even_w_in"], i), p["seg"], (p["q_gain"], i),
                                (p["k_gain"], i), p["perm"][1], p["perm"][2])
            a = _conv_a(a, *((p[k], i) for k in ("conv_a_w", "conv_a_b", "conv_a_ln_g", "conv_a_ln_b")))
            os, ls = zip(*[_attention_group(qkv, bm) for qkv, bm in zip(qkvs, p["bias"])])
            x = _even_out(x, a, os, ls, p["perm_t"][1], p["perm_t"][2], p["expand"], (p["even_w_out"], i), ffn)
        else:
            x = _odd(x, (p["odd_norm"], i), (p["odd_w_in"], i), (p["conv_c_w"], i), (p["odd_w_out"], i), ffn)
    return x


def _prepare(rel_bias, even_norm, even_w_in, conv_a_w, conv_a_b, conv_a_ln_g, conv_a_ln_b, q_norm, k_norm,
             even_w_out, odd_norm, odd_w_in, conv_c_w, odd_w_out, ffn_norm, ffn_w1, ffn_w3, ffn_w2):
    row = lambda a: a.astype(F32)[:, None, :]
    tile_heads = lambda a: jnp.tile(a.astype(F32), (1, HEADS_PER_GROUP))[:, None, :]
    perms = [None if d == 1 else _class_perm(d) for d in DILATIONS]
    return dict(
        bias=_bias_windows(rel_bias), seg=_head_segments(), expand=_head_expand(),
        perm=[None if m is None else jnp.asarray(m, BF16) for m in perms],
        perm_t=[None if m is None else jnp.asarray(m.T, BF16) for m in perms],
        even_norm=row(even_norm), even_w_in=even_w_in.astype(BF16),
        conv_a_w=jnp.broadcast_to(conv_a_w.astype(F32)[:, :, None, :], conv_a_w.shape[:2] + (SUBLANES, CONV_A_CH)),
        conv_a_b=row(conv_a_b), conv_a_ln_g=row(conv_a_ln_g), conv_a_ln_b=row(conv_a_ln_b),
        q_gain=tile_heads(q_norm) * (HEAD_DIM ** -0.5 * LOG2E), k_gain=tile_heads(k_norm),
        even_w_out=even_w_out.astype(BF16),
        odd_norm=row(odd_norm), odd_w_in=odd_w_in.astype(BF16), conv_c_w=conv_c_w.astype(F32),
        odd_w_out=odd_w_out.astype(BF16),
        ffn_norm=row(ffn_norm), ffn_w1=ffn_w1.astype(BF16), ffn_w3=ffn_w3.astype(BF16), ffn_w2=ffn_w2.astype(BF16),
    )


def kernel(x_prompt, x_sample, rel_bias, even_norm, even_w_in, conv_a_w, conv_a_b, conv_a_ln_g, conv_a_ln_b, q_norm, k_norm, even_w_out, odd_norm, odd_w_in, conv_c_w, odd_w_out, ffn_norm, ffn_w1, ffn_w3, ffn_w2):
    p = _prepare(rel_bias, even_norm, even_w_in, conv_a_w, conv_a_b, conv_a_ln_g, conv_a_ln_b, q_norm, k_norm,
                 even_w_out, odd_norm, odd_w_in, conv_c_w, odd_w_out, ffn_norm, ffn_w1, ffn_w3, ffn_w2)
    return (_trunk(x_prompt, p), _trunk(x_sample, p))
```

```python
import functools
import math

import numpy as np
import jax
import jax.numpy as jnp
from jax import lax
from jax.experimental import pallas as pl
from jax.experimental.pallas import tpu as pltpu

F32 = jnp.float32
BF16 = jnp.bfloat16

D_MODEL = 1024
DEPTH = 4
HEAD_DIM = 64
WINDOWS = (128, 512, 2048)
DILATIONS = (1, 4, 16)
N_GROUPS = 3
HEADS_PER_GROUP = 8
GROUP_W = HEADS_PER_GROUP * HEAD_DIM
CONV_A_CH = D_MODEL // 2
CONV_A_K = 31
FFN_HIDDEN = 2816
N_BUCKETS = 32
MAX_DISTANCE = 1024
EPS = 1e-6
NEG_INF = -1e30
LOG2E = math.log2(math.e)
HALF = 64
assert all(w // (2 * d) == HALF for w, d in zip(WINDOWS, DILATIONS))

V7X_VMEM_BYTES = 64 * 1024 * 1024
VMEM_LIMIT_BYTES = V7X_VMEM_BYTES * 7 // 8
LANES = 128
SUBLANES = 8
MXU_DIM = 256

TM = 512
ROW_SPLIT = 2
PERM = MXU_DIM
SEG_W = MXU_DIM
STAT_W = 2 * HEADS_PER_GROUP
STAT_PARTS = 3
ATT_SUB = 128
ATT_KEYS = ATT_SUB + 2 * HALF
ATT_STEP = 2048
assert ATT_KEYS == 2 * LANES
N_PAIRS = GROUP_W // LANES
CONV_A_TS = 512
CONV_A_RB = 32
CONV_A_HALO = 16
FFN_CHUNKS = ((0, 1536), (1536, 2816))
PACK_W = N_GROUPS * HEADS_PER_GROUP


def _params(n_axes):
    return pltpu.CompilerParams(dimension_semantics=("parallel",) * n_axes,
                                vmem_limit_bytes=VMEM_LIMIT_BYTES)


def _resident(a):
    if isinstance(a, tuple):
        a, layer = a
        nd = a.ndim
        return pl.BlockSpec((None,) + a.shape[1:], lambda *_: (layer,) + (0,) * (nd - 1), pipeline_mode=pl.Buffered(1))
    nd = a.ndim
    return pl.BlockSpec(a.shape, lambda *_: (0,) * nd, pipeline_mode=pl.Buffered(1))


def _arrays(params):
    return [a[0] if isinstance(a, tuple) else a for a in params]


def _rows(tm, width):
    return pl.BlockSpec((None, tm, width), lambda b, i: (b, i, 0))


def _classes(d, tm, width):
    if d == 1:
        return pl.BlockSpec((None, None, tm, width), lambda b, i: (b, 0, i, 0))
    return pl.BlockSpec((None, d, tm // d, width), lambda b, i: (b, 0, i, 0))


def _qkv_tile(d, tm):
    if d == 1:
        return pl.BlockSpec((None, None, 3, tm, GROUP_W), lambda b, i: (b, 0, 0, i, 0))
    return pl.BlockSpec((None, d, 3, tm // d, GROUP_W), lambda b, i: (b, 0, 0, i, 0))


def _rmsnorm_bf16(x, g):
    ms = jnp.mean(x * x, axis=-1, keepdims=True)
    return (x * lax.rsqrt(ms + EPS) * g).astype(BF16)


def _dot(a, b):
    return jnp.dot(a, b, preferred_element_type=F32)


def _even_in_kernel(x_ref, g_ref, w_ref, seg_ref, qg_ref, kg_ref, p4_ref, p16_ref,
                    a_ref, q0_ref, q1_ref, q2_ref):
    h = _rmsnorm_bf16(x_ref[...], g_ref[...])
    tm = h.shape[0]

    def proj(hh, c):
        return _dot(hh, w_ref[:, c * GROUP_W:(c + 1) * GROUP_W])

    a_ref[...] = proj(h, 0) * jax.nn.sigmoid(proj(h, 1))
    seg = seg_ref[...]
    for g, (d, perm_ref, out_ref) in enumerate(zip(DILATIONS, (None, p4_ref, p16_ref), (q0_ref, q1_ref, q2_ref))):
        if d == 1:
            hg = h
        else:
            hg = jnp.concatenate([_dot(perm_ref[...], h[s:s + PERM]).astype(BF16) for s in range(0, tm, PERM)], axis=0)
        for part in range(3):
            p = proj(hg, 2 + part * N_GROUPS + g)
            if part < 2:
                gain = qg_ref[...] if part == 0 else kg_ref[...]
                sq = (p * p).astype(BF16)
                ms = jnp.concatenate([_dot(sq[:, c:c + SEG_W], seg) for c in range(0, GROUP_W, SEG_W)], axis=1)
                ms = ms * (1.0 / HEAD_DIM)
                p = p * lax.rsqrt(ms + EPS) * gain
            val = p.astype(BF16)
            if d == 1:
                out_ref[part] = val
            else:
                n = PERM // d
                for s in range(tm // PERM):
                    for r in range(d):
                        out_ref[r, part, s * n:(s + 1) * n, :] = val[s * PERM + r * n:s * PERM + (r + 1) * n]


def _even_in(x, g, w, seg, qg, kg, p4, p16):
    batch, seq, _ = x.shape
    outs = [jax.ShapeDtypeStruct((batch, seq, CONV_A_CH), F32)]
    outs += [jax.ShapeDtypeStruct((batch, d, 3, seq // d, GROUP_W), BF16) for d in DILATIONS]
    return pl.pallas_call(
        _even_in_kernel,
        grid=(batch, seq // TM),
        in_specs=[_rows(TM, D_MODEL)] + [_resident(a) for a in (g, w, seg, qg, kg, p4, p16)],
        out_specs=[_rows(TM, CONV_A_CH)] + [_qkv_tile(d, TM) for d in DILATIONS],
        out_shape=outs,
        compiler_params=_params(2),
        name="even_in",
    )(x, *_arrays((g, w, seg, qg, kg, p4, p16)))


def _conv_a_kernel(prev_ref, a_ref, next_ref, wb_ref, b_ref, lg_ref, lb_ref, o_ref, buf_ref, sh_ref, *, n_tiles):
    i = pl.program_id(1)
    ts = a_ref.shape[0]
    h = CONV_A_HALO
    rows = ts + 2 * h
    buf_ref[0:h, :] = jnp.where(i > 0, prev_ref[...], 0.0)
    buf_ref[h:h + ts, :] = a_ref[...]
    buf_ref[h + ts:rows, :] = jnp.where(i < n_tiles - 1, next_ref[...], 0.0)
    n_sh = rows - SUBLANES
    for v in range(SUBLANES):
        sh_ref[v] = buf_ref[v:v + n_sh, :].reshape(n_sh // SUBLANES, SUBLANES, CONV_A_CH)
    off = h - (CONV_A_K - 1) // 2
    nb = CONV_A_RB // SUBLANES
    for r in range(0, ts, CONV_A_RB):
        t0 = r // SUBLANES
        acc = None
        for v in range(SUBLANES):
            us = [u for u in range((CONV_A_K + off) // SUBLANES + 1) if 0 <= SUBLANES * u + v - off < CONV_A_K]
            x = sh_ref[v, t0 + us[0]:t0 + us[-1] + nb]
            for u in us:
                term = x[u - us[0]:u - us[0] + nb] * wb_ref[SUBLANES * u + v - off]
                acc = term if acc is None else acc + term
        acc = acc.reshape(CONV_A_RB, CONV_A_CH) + b_ref[...]
        mu = jnp.mean(acc, axis=-1, keepdims=True)
        xc = acc - mu
        var = jnp.mean(xc * xc, axis=-1, keepdims=True)
        y = xc * lax.rsqrt(var + EPS) * lg_ref[...] + lb_ref[...]
        o_ref[r:r + CONV_A_RB, :] = (y * jax.nn.sigmoid(y)).astype(BF16)


def _conv_a(a, wb, b, lg, lb):
    batch, seq, _ = a.shape
    ts = CONV_A_TS
    n_tiles = seq // ts
    hb = ts // CONV_A_HALO
    rows = ts + 2 * CONV_A_HALO
    main = _rows(ts, CONV_A_CH)
    prev = pl.BlockSpec((None, CONV_A_HALO, CONV_A_CH), lambda bi, i: (bi, jnp.maximum(i * hb - 1, 0), 0))
    nxt = pl.BlockSpec((None, CONV_A_HALO, CONV_A_CH),
                       lambda bi, i: (bi, jnp.minimum((i + 1) * hb, seq // CONV_A_HALO - 1), 0))
    return pl.pallas_call(
        functools.partial(_conv_a_kernel, n_tiles=n_tiles),
        grid=(batch, n_tiles),
        in_specs=[prev, main, nxt] + [_resident(t) for t in (wb, b, lg, lb)],
        out_specs=main,
        out_shape=jax.ShapeDtypeStruct((batch, seq, CONV_A_CH), BF16),
        scratch_shapes=[pltpu.VMEM((rows, CONV_A_CH), F32),
                        pltpu.VMEM((SUBLANES, rows // SUBLANES - 1, SUBLANES, CONV_A_CH), F32)],
        compiler_params=_params(2),
        name="conv_a",
    )(a, a, a, *_arrays((wb, b, lg, lb)))


def _attn_kernel(*refs, n_tiles, halo):
    if halo:
        q_ref, kp_ref, kc_ref, kn_ref, vp_ref, vc_ref, vn_ref, bm_ref, o_ref, l_ref, kw_ref, vw_ref = refs
        edges = ((kw_ref, kp_ref, kn_ref), (vw_ref, vp_ref, vn_ref))
    else:
        q_ref, kc_ref, vc_ref, bm_ref, o_ref, l_ref, kw_ref, vw_ref = refs
        edges = ((kw_ref, None, None), (vw_ref, None, None))
    i = pl.program_id(2)
    n_cls, tq = q_ref.shape[0], q_ref.shape[1]
    lane = lax.broadcasted_iota(jnp.int32, (1, LANES), 1)
    first_head = lane < HEAD_DIM
    keep_first = first_head.astype(BF16)
    keep_second = 1.0 - keep_first
    n_sub = tq // ATT_SUB
    lanes = [slice(pr * LANES, (pr + 1) * LANES) for pr in range(N_PAIRS)]
    for c in range(n_cls):
        for (dst, before, after), cur in zip(edges, (kc_ref, vc_ref)):
            pad = jnp.zeros((HALF, GROUP_W), BF16)
            dst[c, 0:HALF, :] = pad if before is None else before[c]
            dst[c, HALF:HALF + tq, :] = cur[c]
            dst[c, HALF + tq:2 * HALF + tq, :] = pad if after is None else after[c]
        for s in range(n_sub):
            r0 = s * ATT_SUB
            edge_lo = edge_hi = None
            if s == 0:
                edge_lo = jnp.where(lane >= jnp.where(i == 0, HALF, 0), 0.0, NEG_INF)
            if s == n_sub - 1:
                edge_hi = jnp.where(lane < jnp.where(i == n_tiles - 1, LANES - HALF, LANES), 0.0, NEG_INF)
            scs = []
            for pr, cs in enumerate(lanes):
                q2 = q_ref[c, r0:r0 + ATT_SUB, cs]
                qs = jnp.concatenate([q2 * keep_first, q2 * keep_second], axis=0)
                sc = lax.dot_general(qs, kw_ref[c, r0:r0 + ATT_KEYS, cs], (((1,), (1,)), ((), ())),
                                     preferred_element_type=F32) + bm_ref[pr]
                if edge_lo is not None or edge_hi is not None:
                    left, right = sc[:, :LANES], sc[:, LANES:]
                    left = left if edge_lo is None else left + edge_lo
                    right = right if edge_hi is None else right + edge_hi
                    sc = jnp.concatenate([left, right], axis=1)
                scs.append(sc)
            ms = [jnp.max(sc, axis=-1, keepdims=True) for sc in scs]
            ps = [jnp.exp2(sc - m) for sc, m in zip(scs, ms)]
            dens = [jnp.sum(p, axis=-1, keepdims=True) for p in ps]
            pvs = [_dot(p.astype(BF16), vw_ref[c, r0:r0 + ATT_KEYS, cs]) for p, cs in zip(ps, lanes)]
            stat_rows = jnp.zeros((ATT_SUB, LANES), F32)
            for pr, (cs, pv, m, den) in enumerate(zip(lanes, pvs, ms, dens)):
                o_ref[c, r0:r0 + ATT_SUB, cs] = jnp.where(first_head, pv[0:ATT_SUB], pv[ATT_SUB:]).astype(BF16)
                for hd, rows in ((2 * pr, slice(0, ATT_SUB)), (2 * pr + 1, slice(ATT_SUB, 2 * ATT_SUB))):
                    both = jnp.where(lane < HEADS_PER_GROUP, m[rows], den[rows])
                    stat_rows = jnp.where((lane == hd) | (lane == HEADS_PER_GROUP + hd), both, stat_rows)
            packed, rest = None, stat_rows
            for t in range(STAT_PARTS):
                part = rest.astype(BF16).astype(F32)
                rest = rest - part
                part = part if t == 0 else pltpu.roll(part, t * STAT_W, 1)
                packed = part if packed is None else packed + part
            l_ref[c, r0:r0 + ATT_SUB, :] = packed.astype(BF16)


def _attention_group(qkv, bm):
    batch, d, _, cls_len, _ = qkv.shape
    tq = min(cls_len, ATT_STEP)
    n_tiles = cls_len // tq
    halo = n_tiles > 1
    n_cls = 1 if halo else min(d, ATT_STEP // tq)
    hb = tq // HALF

    def main(part):
        return pl.BlockSpec((None, n_cls, None, tq, GROUP_W), lambda b, r, i: (b, r, part, i, 0))

    def before(part):
        return pl.BlockSpec((None, 1, None, HALF, GROUP_W),
                            lambda b, r, i: (b, r, part, jnp.maximum(i * hb - 1, 0), 0))

    def after(part):
        return pl.BlockSpec((None, 1, None, HALF, GROUP_W),
                            lambda b, r, i: (b, r, part, jnp.minimum((i + 1) * hb, cls_len // HALF - 1), 0))

    def out(width):
        return pl.BlockSpec((None, n_cls, tq, width), lambda b, r, i: (b, r, i, 0))

    if halo:
        in_specs = [main(0), before(1), main(1), after(1), before(2), main(2), after(2), _resident(bm)]
    else:
        in_specs = [main(0), main(1), main(2), _resident(bm)]
    window = pltpu.VMEM((n_cls, tq + 2 * HALF, GROUP_W), BF16)
    return pl.pallas_call(
        functools.partial(_attn_kernel, n_tiles=n_tiles, halo=halo),
        grid=(batch, d // n_cls, n_tiles),
        in_specs=in_specs,
        out_specs=[out(GROUP_W), out(LANES)],
        out_shape=[jax.ShapeDtypeStruct((batch, d, cls_len, GROUP_W), BF16),
                   jax.ShapeDtypeStruct((batch, d, cls_len, LANES), BF16)],
        scratch_shapes=[window, window],
        compiler_params=_params(3),
        name=f"attn_d{d}",
    )(*([qkv] * (len(in_specs) - 1)), bm)


def _token_order(ref, d, perm_t, s):
    if d == 1:
        return ref[s * PERM:(s + 1) * PERM, :].astype(F32)
    n = PERM // d
    cm = jnp.concatenate([ref[r, s * n:(s + 1) * n, :] for r in range(d)], axis=0)
    return _dot(perm_t, cm)


def _ffn_rows(y_ref, parts, g_ref, w1_ref, w3_ref, w2_ref):
    xs = [y_ref[p, :] for p in parts]
    hs = [_rmsnorm_bf16(x, g_ref[...]) for x in xs]
    ys = xs
    for lo, hi in FFN_CHUNKS:
        acts = []
        for h in hs:
            a = _dot(h, w1_ref[:, lo:hi])
            acts.append((a * jax.nn.sigmoid(a) * _dot(h, w3_ref[:, lo:hi])).astype(BF16))
        ys = [y + _dot(act, w2_ref[lo:hi, :]) for y, act in zip(ys, acts)]
    for p, y in zip(parts, ys):
        y_ref[p, :] = y


def _even_out_kernel(x_ref, a_ref, o0_ref, o1_ref, o2_ref, l0_ref, l1_ref, l2_ref, p4t_ref, p16t_ref, ex_ref,
                     w_ref, fg_ref, w1_ref, w3_ref, w2_ref, y_ref):
    perms = (None, p4t_ref[...], p16t_ref[...])
    lane = lax.broadcasted_iota(jnp.int32, (1, LANES), 1)
    parts = [slice(s * PERM, (s + 1) * PERM) for s in range(x_ref.shape[0] // PERM)]
    for s, p in enumerate(parts):
        stats = []
        for l_ref, d, pt in zip((l0_ref, l1_ref, l2_ref), DILATIONS, perms):
            terms = _token_order(l_ref, d, pt, s)
            st = terms
            for t in range(1, STAT_PARTS):
                st = st + pltpu.roll(terms, LANES - t * STAT_W, 1)
            stats.append(st)
        m = jnp.maximum(jnp.maximum(stats[0], stats[1]), stats[2])
        es = [jnp.exp2(st - m) for st in stats]
        dens = [pltpu.roll(st, LANES - HEADS_PER_GROUP, 1) for st in stats]
        inv = 1.0 / (es[0] * dens[0] + es[1] * dens[1] + es[2] * dens[2])
        packed = None
        for g, e in enumerate(es):
            al = jnp.where(lane < HEADS_PER_GROUP, e * inv, 0.0)
            al = al if g == 0 else pltpu.roll(al, g * HEADS_PER_GROUP, 1)
            packed = al if packed is None else packed + al
        terms, rest = None, packed
        for t in range(STAT_PARTS):
            part = rest.astype(BF16).astype(F32)
            rest = rest - part
            part = part if t == 0 else pltpu.roll(part, t * PACK_W, 1)
            terms = part if terms is None else terms + part
        wgt = _dot(terms.astype(BF16), ex_ref[...])
        o = None
        for g, (o_ref, d, pt) in enumerate(zip((o0_ref, o1_ref, o2_ref), DILATIONS, perms)):
            term = wgt[:, g * GROUP_W:(g + 1) * GROUP_W] * _token_order(o_ref, d, pt, s)
            o = term if o is None else o + term
        y_ref[p, :] = (x_ref[p, :] + _dot(a_ref[p, :], w_ref[0:CONV_A_CH, :])
                       + _dot(o.astype(BF16), w_ref[CONV_A_CH:, :]))
    _ffn_rows(y_ref, parts, fg_ref, w1_ref, w3_ref, w2_ref)


def _even_out(x, a, os, ls, p4t, p16t, ex, w, ffn):
    batch, seq, _ = x.shape
    return pl.pallas_call(
        _even_out_kernel,
        grid=(batch, seq // TM),
        in_specs=[_rows(TM, D_MODEL), _rows(TM, CONV_A_CH)]
                 + [_classes(d, TM, GROUP_W) for d in DILATIONS] + [_classes(d, TM, LANES) for d in DILATIONS]
                 + [_resident(t) for t in (p4t, p16t, ex, w, *ffn)],
        out_specs=_rows(TM, D_MODEL),
        out_shape=jax.ShapeDtypeStruct((batch, seq, D_MODEL), F32),
        compiler_params=_params(2),
        name="even_out_ffn",
    )(x, a, *os, *ls, *_arrays((p4t, p16t, ex, w, *ffn)))


def _odd_kernel(prev_ref, x_ref, next_ref, g_ref, win_ref, cw_ref, wout_ref, fg_ref, w1_ref, w3_ref, w2_ref,
                y_ref, buf_ref, *, n_tiles):
    i = pl.program_id(1)
    tm = x_ref.shape[0]
    hb = SUBLANES
    n_split = buf_ref.shape[0]
    rows = tm // n_split
    parts = [slice(k * rows, (k + 1) * rows) for k in range(n_split)]
    g = g_ref[...]
    xs = [x_ref[p, :] for p in parts]
    hs, keep = [], []
    for k, p in enumerate(parts):
        before = prev_ref[...] if k == 0 else x_ref[p.start - hb:p.start, :]
        after = next_ref[...] if k == n_split - 1 else x_ref[p.stop:p.stop + hb, :]
        halo = jnp.concatenate([before, after], axis=0)
        hs.append(jnp.concatenate([_rmsnorm_bf16(xs[k], g), _rmsnorm_bf16(halo, g)], axis=0))
        keep.append((i > 0 if k == 0 else None, i < n_tiles - 1 if k == n_split - 1 else None))
    gates = [_dot(h[0:rows], win_ref[:, 0:D_MODEL]) for h in hs]
    cus = [_dot(h, win_ref[:, D_MODEL:2 * D_MODEL]) * _dot(h, win_ref[:, 2 * D_MODEL:3 * D_MODEL]) for h in hs]
    for k, (cu, (keep_lo, keep_hi)) in enumerate(zip(cus, keep)):
        lo, hi = cu[rows:rows + hb], cu[rows + hb:]
        buf_ref[k, 0:hb, :] = lo if keep_lo is None else jnp.where(keep_lo, lo, 0.0)
        buf_ref[k, hb:hb + rows, :] = cu[0:rows]
        buf_ref[k, hb + rows:2 * hb + rows, :] = hi if keep_hi is None else jnp.where(keep_hi, hi, 0.0)
    convs = [buf_ref[k, hb - 1:hb - 1 + rows, :] * cw_ref[0:1, :] + buf_ref[k, hb:hb + rows, :] * cw_ref[1:2, :]
             + buf_ref[k, hb + 1:hb + 1 + rows, :] * cw_ref[2:3, :] for k in range(n_split)]
    for p, x, gate, conv in zip(parts, xs, gates, convs):
        y_ref[p, :] = x + _dot((gate * conv).astype(BF16), wout_ref[...])
    _ffn_rows(y_ref, parts, fg_ref, w1_ref, w3_ref, w2_ref)


def _odd(x, g, win, cw, wout, ffn):
    batch, seq, _ = x.shape
    n_tiles = seq // TM
    per = TM // SUBLANES
    main = _rows(TM, D_MODEL)
    prev = pl.BlockSpec((None, SUBLANES, D_MODEL), lambda bi, i: (bi, jnp.maximum(i * per - 1, 0), 0))
    nxt = pl.BlockSpec((None, SUBLANES, D_MODEL), lambda bi, i: (bi, jnp.minimum((i + 1) * per, seq // SUBLANES - 1), 0))
    return pl.pallas_call(
        functools.partial(_odd_kernel, n_tiles=n_tiles),
        grid=(batch, n_tiles),
        in_specs=[prev, main, nxt] + [_resident(t) for t in (g, win, cw, wout, *ffn)],
        out_specs=main,
        out_shape=jax.ShapeDtypeStruct((batch, seq, D_MODEL), F32),
        scratch_shapes=[pltpu.VMEM((ROW_SPLIT, TM // ROW_SPLIT + 2 * SUBLANES, D_MODEL), F32)],
        compiler_params=_params(2),
        name="odd_ffn",
    )(x, x, x, *_arrays((g, win, cw, wout, *ffn)))


def _t5_bucket_np(rel):
    nb = N_BUCKETS // 2
    max_exact = nb // 2
    n = np.abs(rel)
    nf = np.maximum(n, 1).astype(np.float32)
    large = max_exact + (np.log(nf / max_exact) / math.log(MAX_DISTANCE / max_exact) * (nb - max_exact)).astype(np.int32)
    large = np.minimum(large, nb - 1)
    return np.where(rel > 0, nb, 0) + np.where(n < max_exact, n, large)


def _bias_windows(rel_bias):
    n_off = ATT_SUB + ATT_KEYS - 1
    off = np.arange(n_off) - (ATT_SUB - 1) - HALF
    band = np.abs(off) <= HALF
    out = []
    for g, d in enumerate(DILATIONS):
        tab = rel_bias[:, g * HEADS_PER_GROUP:(g + 1) * HEADS_PER_GROUP].astype(F32)
        vec = jnp.take(tab, jnp.asarray(_t5_bucket_np(off * d)), axis=0)
        vec = jnp.where(band[:, None], vec * LOG2E, NEG_INF).T
        flat = jnp.tile(jnp.pad(vec, ((0, 0), (0, 1))), (1, ATT_SUB))[:, :ATT_SUB * n_off]
        skew = flat.reshape(HEADS_PER_GROUP, ATT_SUB, n_off)
        win = skew[:, :, ATT_SUB - 1:ATT_SUB - 1 + ATT_KEYS]
        out.append(win.reshape(N_PAIRS, 2 * ATT_SUB, ATT_KEYS))
    return out


def _head_segments():
    h = np.arange(SEG_W) // HEAD_DIM
    return jnp.asarray(h[:, None] == h[None, :], BF16)


def _head_expand():
    src = np.arange(LANES)
    col = np.arange(N_GROUPS * GROUP_W)
    hit = (src[:, None] % PACK_W == (col // GROUP_W) * HEADS_PER_GROUP + (col % GROUP_W) // HEAD_DIM)[:, :]
    return jnp.asarray(hit & (src[:, None] < STAT_PARTS * PACK_W), BF16)


def _class_perm(d):
    n = PERM // d
    src = (np.arange(PERM) % n) * d + np.arange(PERM) // n
    return np.arange(PERM)[None, :] == src[:, None]


def _trunk(x, p):
    for layer in range(DEPTH):
        i = layer // 2
        ffn = tuple((p[k], layer) for k in ("ffn_norm", "ffn_w1", "ffn_w3", "ffn_w2"))
        if layer % 2 == 0:
            a, *qkvs = _even_in(x, (p["even_norm"], i), (p["even_w_in"], i), p["seg"], (p["q_gain"], i),
                                (p["k_gain"], i), p["perm"][1], p["perm"][2])
            a = _conv_a(a, *((p[k], i) for k in ("conv_a_w", "conv_a_b", "conv_a_ln_g", "conv_a_ln_b")))
            os, ls = zip(*[_attention_group(qkv, bm) for qkv, bm in zip(qkvs, p["bias"])])
            x = _even_out(x, a, os, ls, p["perm_t"][1], p["perm_t"][2], p["expand"], (p["even_w_out"], i), ffn)
        else:
            x = _odd(x, (p["odd_norm"], i), (p["odd_w_in"], i), (p["conv_c_w"], i), (p["odd_w_out"], i), ffn)
    return x


def _prepare(rel_bias, even_norm, even_w_in, conv_a_w, conv_a_b, conv_a_ln_g, conv_a_ln_b, q_norm, k_norm,
             even_w_out, odd_norm, odd_w_in, conv_c_w, odd_w_out, ffn_norm, ffn_w1, ffn_w3, ffn_w2):
    row = lambda a: a.astype(F32)[:, None, :]
    tile_heads = lambda a: jnp.tile(a.astype(F32), (1, HEADS_PER_GROUP))[:, None, :]
    perms = [None if d == 1 else _class_perm(d) for d in DILATIONS]
    return dict(
        bias=_bias_windows(rel_bias), seg=_head_segments(), expand=_head_expand(),
        perm=[None if m is None else jnp.asarray(m, BF16) for m in perms],
        perm_t=[None if m is None else jnp.asarray(m.T, BF16) for m in perms],
        even_norm=row(even_norm), even_w_in=even_w_in.astype(BF16),
        conv_a_w=jnp.broadcast_to(conv_a_w.astype(F32)[:, :, None, :], conv_a_w.shape[:2] + (SUBLANES, CONV_A_CH)),
        conv_a_b=row(conv_a_b), conv_a_ln_g=row(conv_a_ln_g), conv_a_ln_b=row(conv_a_ln_b),
        q_gain=tile_heads(q_norm) * (HEAD_DIM ** -0.5 * LOG2E), k_gain=tile_heads(k_norm),
        even_w_out=even_w_out.astype(BF16),
        odd_norm=row(odd_norm), odd_w_in=odd_w_in.astype(BF16), conv_c_w=conv_c_w.astype(F32),
        odd_w_out=odd_w_out.astype(BF16),
        ffn_norm=row(ffn_norm), ffn_w1=ffn_w1.astype(BF16), ffn_w3=ffn_w3.astype(BF16), ffn_w2=ffn_w2.astype(BF16),
    )


def kernel(x_prompt, x_sample, rel_bias, even_norm, even_w_in, conv_a_w, conv_a_b, conv_a_ln_g, conv_a_ln_b, q_norm, k_norm, even_w_out, odd_norm, odd_w_in, conv_c_w, odd_w_out, ffn_norm, ffn_w1, ffn_w3, ffn_w2):
    p = _prepare(rel_bias, even_norm, even_w_in, conv_a_w, conv_a_b, conv_a_ln_g, conv_a_ln_b, q_norm, k_norm,
                 even_w_out, odd_norm, odd_w_in, conv_c_w, odd_w_out, ffn_norm, ffn_w1, ffn_w3, ffn_w2)
    return (_trunk(x_prompt, p), _trunk(x_sample, p))
```

```python
import functools
import math

import numpy as np
import jax
import jax.numpy as jnp
from jax import lax
from jax.experimental import pallas as pl
from jax.experimental.pallas import tpu as pltpu

F32 = jnp.float32
BF16 = jnp.bfloat16

D_MODEL = 1024
DEPTH = 4
HEAD_DIM = 64
WINDOWS = (128, 512, 2048)
DILATIONS = (1, 4, 16)
N_GROUPS = 3
HEADS_PER_GROUP = 8
GROUP_W = HEADS_PER_GROUP * HEAD_DIM
CONV_A_CH = D_MODEL // 2
CONV_A_K = 31
FFN_HIDDEN = 2816
N_BUCKETS = 32
MAX_DISTANCE = 1024
EPS = 1e-6
NEG_INF = -1e30
LOG2E = math.log2(math.e)
HALF = 64
assert all(w // (2 * d) == HALF for w, d in zip(WINDOWS, DILATIONS))

V7X_VMEM_BYTES = 64 * 1024 * 1024
VMEM_LIMIT_BYTES = V7X_VMEM_BYTES * 7 // 8
LANES = 128
SUBLANES = 8
MXU_DIM = 256

TM = 512
ROW_SPLIT = 2
PERM = MXU_DIM
SEG_W = MXU_DIM
STAT_W = 2 * HEADS_PER_GROUP
STAT_PARTS = 3
ATT_SUB = 128
ATT_KEYS = ATT_SUB + 2 * HALF
ATT_STEP = 2048
assert ATT_KEYS == 2 * LANES
N_PAIRS = GROUP_W // LANES
CONV_A_TS = 512
CONV_A_RB = 32
CONV_A_HALO = 16
FFN_CHUNKS = ((0, 1536), (1536, 2816))
PACK_W = N_GROUPS * HEADS_PER_GROUP


def _params(n_axes):
    return pltpu.CompilerParams(dimension_semantics=("parallel",) * n_axes,
                                vmem_limit_bytes=VMEM_LIMIT_BYTES)


def _resident(a):
    if isinstance(a, tuple):
        a, layer = a
        nd = a.ndim
        return pl.BlockSpec((None,) + a.shape[1:], lambda *_: (layer,) + (0,) * (nd - 1), pipeline_mode=pl.Buffered(1))
    nd = a.ndim
    return pl.BlockSpec(a.shape, lambda *_: (0,) * nd, pipeline_mode=pl.Buffered(1))


def _arrays(params):
    return [a[0] if isinstance(a, tuple) else a for a in params]


def _rows(tm, width):
    return pl.BlockSpec((None, tm, width), lambda b, i: (b, i, 0))


def _classes(d, tm, width):
    if d == 1:
        return pl.BlockSpec((None, None, tm, width), lambda b, i: (b, 0, i, 0))
    return pl.BlockSpec((None, d, tm // d, width), lambda b, i: (b, 0, i, 0))


def _qkv_tile(d, tm):
    if d == 1:
        return pl.BlockSpec((None, None, 3, tm, GROUP_W), lambda b, i: (b, 0, 0, i, 0))
    return pl.BlockSpec((None, d, 3, tm // d, GROUP_W), lambda b, i: (b, 0, 0, i, 0))


def _rmsnorm_bf16(x, g):
    ms = jnp.mean(x * x, axis=-1, keepdims=True)
    return (x * lax.rsqrt(ms + EPS) * g).astype(BF16)


def _dot(a, b):
    return jnp.dot(a, b, preferred_element_type=F32)


def _even_in_kernel(x_ref, g_ref, w_ref, seg_ref, qg_ref, kg_ref, p4_ref, p16_ref,
                    a_ref, q0_ref, q1_ref, q2_ref):
    parts = [slice(s * PERM, (s + 1) * PERM) for s in range(x_ref.shape[0] // PERM)]
    hs = [_rmsnorm_bf16(x_ref[p, :], g_ref[...]) for p in parts]

    def proj(hh, c):
        return _dot(hh, w_ref[:, c * GROUP_W:(c + 1) * GROUP_W])

    vals = [proj(h, 0) for h in hs]
    gates = [proj(h, 1) for h in hs]
    for p, val, gate in zip(parts, vals, gates):
        a_ref[p, :] = val * jax.nn.sigmoid(gate)
    seg = seg_ref[...]
    for g, (d, perm_ref, out_ref) in enumerate(zip(DILATIONS, (None, p4_ref, p16_ref), (q0_ref, q1_ref, q2_ref))):
        hgs = hs if d == 1 else [_dot(perm_ref[...], h).astype(BF16) for h in hs]
        for part in range(3):
            ps = [proj(hg, 2 + part * N_GROUPS + g) for hg in hgs]
            if part < 2:
                gain = qg_ref[...] if part == 0 else kg_ref[...]
                sqs = [(p * p).astype(BF16) for p in ps]
                mss = [jnp.concatenate([_dot(sq[:, c:c + SEG_W], seg) for c in range(0, GROUP_W, SEG_W)], axis=1)
                       for sq in sqs]
                ps = [p * lax.rsqrt(ms * (1.0 / HEAD_DIM) + EPS) * gain for p, ms in zip(ps, mss)]
            for s, p in enumerate(ps):
                val = p.astype(BF16)
                if d == 1:
                    out_ref[part, s * PERM:(s + 1) * PERM, :] = val
                else:
                    n = PERM // d
                    for r in range(d):
                        out_ref[r, part, s * n:(s + 1) * n, :] = val[r * n:(r + 1) * n]


def _even_in(x, g, w, seg, qg, kg, p4, p16):
    batch, seq, _ = x.shape
    outs = [jax.ShapeDtypeStruct((batch, seq, CONV_A_CH), F32)]
    outs += [jax.ShapeDtypeStruct((batch, d, 3, seq // d, GROUP_W), BF16) for d in DILATIONS]
    return pl.pallas_call(
        _even_in_kernel,
        grid=(batch, seq // TM),
        in_specs=[_rows(TM, D_MODEL)] + [_resident(a) for a in (g, w, seg, qg, kg, p4, p16)],
        out_specs=[_rows(TM, CONV_A_CH)] + [_qkv_tile(d, TM) for d in DILATIONS],
        out_shape=outs,
        compiler_params=_params(2),
        name="even_in",
    )(x, *_arrays((g, w, seg, qg, kg, p4, p16)))


def _conv_a_kernel(prev_ref, a_ref, next_ref, wb_ref, b_ref, lg_ref, lb_ref, o_ref, buf_ref, sh_ref, *, n_tiles):
    i = pl.program_id(1)
    ts = a_ref.shape[0]
    h = CONV_A_HALO
    rows = ts + 2 * h
    buf_ref[0:h, :] = jnp.where(i > 0, prev_ref[...], 0.0)
    buf_ref[h:h + ts, :] = a_ref[...]
    buf_ref[h + ts:rows, :] = jnp.where(i < n_tiles - 1, next_ref[...], 0.0)
    n_sh = rows - SUBLANES
    for v in range(SUBLANES):
        sh_ref[v] = buf_ref[v:v + n_sh, :].reshape(n_sh // SUBLANES, SUBLANES, CONV_A_CH)
    off = h - (CONV_A_K - 1) // 2
    nb = CONV_A_RB // SUBLANES
    for r in range(0, ts, CONV_A_RB):
        t0 = r // SUBLANES
        acc = None
        for v in range(SUBLANES):
            us = [u for u in range((CONV_A_K + off) // SUBLANES + 1) if 0 <= SUBLANES * u + v - off < CONV_A_K]
            x = sh_ref[v, t0 + us[0]:t0 + us[-1] + nb]
            for u in us:
                term = x[u - us[0]:u - us[0] + nb] * wb_ref[SUBLANES * u + v - off]
                acc = term if acc is None else acc + term
        acc = acc.reshape(CONV_A_RB, CONV_A_CH) + b_ref[...]
        mu = jnp.mean(acc, axis=-1, keepdims=True)
        xc = acc - mu
        var = jnp.mean(xc * xc, axis=-1, keepdims=True)
        y = xc * lax.rsqrt(var + EPS) * lg_ref[...] + lb_ref[...]
        o_ref[r:r + CONV_A_RB, :] = (y * jax.nn.sigmoid(y)).astype(BF16)


def _conv_a(a, wb, b, lg, lb):
    batch, seq, _ = a.shape
    ts = CONV_A_TS
    n_tiles = seq // ts
    hb = ts // CONV_A_HALO
    rows = ts + 2 * CONV_A_HALO
    main = _rows(ts, CONV_A_CH)
    prev = pl.BlockSpec((None, CONV_A_HALO, CONV_A_CH), lambda bi, i: (bi, jnp.maximum(i * hb - 1, 0), 0))
    nxt = pl.BlockSpec((None, CONV_A_HALO, CONV_A_CH),
                       lambda bi, i: (bi, jnp.minimum((i + 1) * hb, seq // CONV_A_HALO - 1), 0))
    return pl.pallas_call(
        functools.partial(_conv_a_kernel, n_tiles=n_tiles),
        grid=(batch, n_tiles),
        in_specs=[prev, main, nxt] + [_resident(t) for t in (wb, b, lg, lb)],
        out_specs=main,
        out_shape=jax.ShapeDtypeStruct((batch, seq, CONV_A_CH), BF16),
        scratch_shapes=[pltpu.VMEM((rows, CONV_A_CH), F32),
                        pltpu.VMEM((SUBLANES, rows // SUBLANES - 1, SUBLANES, CONV_A_CH), F32)],
        compiler_params=_params(2),
        name="conv_a",
    )(a, a, a, *_arrays((wb, b, lg, lb)))


def _attn_kernel(*refs, n_tiles, halo):
    if halo:
        q_ref, kp_ref, kc_ref, kn_ref, vp_ref, vc_ref, vn_ref, bm_ref, o_ref, l_ref, kw_ref, vw_ref = refs
        edges = ((kw_ref, kp_ref, kn_ref), (vw_ref, vp_ref, vn_ref))
    else:
        q_ref, kc_ref, vc_ref, bm_ref, o_ref, l_ref, kw_ref, vw_ref = refs
        edges = ((kw_ref, None, None), (vw_ref, None, None))
    i = pl.program_id(2)
    n_cls, tq = q_ref.shape[0], q_ref.shape[1]
    lane = lax.broadcasted_iota(jnp.int32, (1, LANES), 1)
    first_head = lane < HEAD_DIM
    keep_first = first_head.astype(BF16)
    keep_second = 1.0 - keep_first
    n_sub = tq // ATT_SUB
    lanes = [slice(pr * LANES, (pr + 1) * LANES) for pr in range(N_PAIRS)]
    for c in range(n_cls):
        for (dst, before, after), cur in zip(edges, (kc_ref, vc_ref)):
            pad = jnp.zeros((HALF, GROUP_W), BF16)
            dst[c, 0:HALF, :] = pad if before is None else before[c]
            dst[c, HALF:HALF + tq, :] = cur[c]
            dst[c, HALF + tq:2 * HALF + tq, :] = pad if after is None else after[c]
        for s in range(n_sub):
            r0 = s * ATT_SUB
            edge_lo = edge_hi = None
            if s == 0:
                edge_lo = jnp.where(lane >= jnp.where(i == 0, HALF, 0), 0.0, NEG_INF)
            if s == n_sub - 1:
                edge_hi = jnp.where(lane < jnp.where(i == n_tiles - 1, LANES - HALF, LANES), 0.0, NEG_INF)
            scs = []
            for pr, cs in enumerate(lanes):
                q2 = q_ref[c, r0:r0 + ATT_SUB, cs]
                qs = jnp.concatenate([q2 * keep_first, q2 * keep_second], axis=0)
                sc = lax.dot_general(qs, kw_ref[c, r0:r0 + ATT_KEYS, cs], (((1,), (1,)), ((), ())),
                                     preferred_element_type=F32) + bm_ref[pr]
                if edge_lo is not None or edge_hi is not None:
                    left, right = sc[:, :LANES], sc[:, LANES:]
                    left = left if edge_lo is None else left + edge_lo
                    right = right if edge_hi is None else right + edge_hi
                    sc = jnp.concatenate([left, right], axis=1)
                scs.append(sc)
            ms = [jnp.max(sc, axis=-1, keepdims=True) for sc in scs]
            ps = [jnp.exp2(sc - m) for sc, m in zip(scs, ms)]
            dens = [jnp.sum(p, axis=-1, keepdims=True) for p in ps]
            pvs = [_dot(p.astype(BF16), vw_ref[c, r0:r0 + ATT_KEYS, cs]) for p, cs in zip(ps, lanes)]
            stat_rows = jnp.zeros((ATT_SUB, LANES), F32)
            for pr, (cs, pv, m, den) in enumerate(zip(lanes, pvs, ms, dens)):
                o_ref[c, r0:r0 + ATT_SUB, cs] = jnp.where(first_head, pv[0:ATT_SUB], pv[ATT_SUB:]).astype(BF16)
                for hd, rows in ((2 * pr, slice(0, ATT_SUB)), (2 * pr + 1, slice(ATT_SUB, 2 * ATT_SUB))):
                    both = jnp.where(lane < HEADS_PER_GROUP, m[rows], den[rows])
                    stat_rows = jnp.where((lane == hd) | (lane == HEADS_PER_GROUP + hd), both, stat_rows)
            packed, rest = None, stat_rows
            for t in range(STAT_PARTS):
                part = rest.astype(BF16).astype(F32)
                rest = rest - part
                part = part if t == 0 else pltpu.roll(part, t * STAT_W, 1)
                packed = part if packed is None else packed + part
            l_ref[c, r0:r0 + ATT_SUB, :] = packed.astype(BF16)


def _attention_group(qkv, bm):
    batch, d, _, cls_len, _ = qkv.shape
    tq = min(cls_len, ATT_STEP)
    n_tiles = cls_len // tq
    halo = n_tiles > 1
    n_cls = 1 if halo else min(d, ATT_STEP // tq)
    hb = tq // HALF

    def main(part):
        return pl.BlockSpec((None, n_cls, None, tq, GROUP_W), lambda b, r, i: (b, r, part, i, 0))

    def before(part):
        return pl.BlockSpec((None, 1, None, HALF, GROUP_W),
                            lambda b, r, i: (b, r, part, jnp.maximum(i * hb - 1, 0), 0))

    def after(part):
        return pl.BlockSpec((None, 1, None, HALF, GROUP_W),
                            lambda b, r, i: (b, r, part, jnp.minimum((i + 1) * hb, cls_len // HALF - 1), 0))

    def out(width):
        return pl.BlockSpec((None, n_cls, tq, width), lambda b, r, i: (b, r, i, 0))

    if halo:
        in_specs = [main(0), before(1), main(1), after(1), before(2), main(2), after(2), _resident(bm)]
    else:
        in_specs = [main(0), main(1), main(2), _resident(bm)]
    window = pltpu.VMEM((n_cls, tq + 2 * HALF, GROUP_W), BF16)
    return pl.pallas_call(
        functools.partial(_attn_kernel, n_tiles=n_tiles, halo=halo),
        grid=(batch, d // n_cls, n_tiles),
        in_specs=in_specs,
        out_specs=[out(GROUP_W), out(LANES)],
        out_shape=[jax.ShapeDtypeStruct((batch, d, cls_len, GROUP_W), BF16),
                   jax.ShapeDtypeStruct((batch, d, cls_len, LANES), BF16)],
        scratch_shapes=[window, window],
        compiler_params=_params(3),
        name=f"attn_d{d}",
    )(*([qkv] * (len(in_specs) - 1)), bm)


def _token_order(ref, d, perm_t, s):
    if d == 1:
        return ref[s * PERM:(s + 1) * PERM, :].astype(F32)
    n = PERM // d
    cm = jnp.concatenate([ref[r, s * n:(s + 1) * n, :] for r in range(d)], axis=0)
    return _dot(perm_t, cm)


def _ffn_rows(y_ref, parts, g_ref, w1_ref, w3_ref, w2_ref):
    xs = [y_ref[p, :] for p in parts]
    hs = [_rmsnorm_bf16(x, g_ref[...]) for x in xs]
    ys = xs
    for lo, hi in FFN_CHUNKS:
        acts = []
        for h in hs:
            a = _dot(h, w1_ref[:, lo:hi])
            acts.append((a * jax.nn.sigmoid(a) * _dot(h, w3_ref[:, lo:hi])).astype(BF16))
        ys = [y + _dot(act, w2_ref[lo:hi, :]) for y, act in zip(ys, acts)]
    for p, y in zip(parts, ys):
        y_ref[p, :] = y


def _even_out_kernel(x_ref, a_ref, o0_ref, o1_ref, o2_ref, l0_ref, l1_ref, l2_ref, p4t_ref, p16t_ref, ex_ref,
                     w_ref, fg_ref, w1_ref, w3_ref, w2_ref, y_ref):
    perms = (None, p4t_ref[...], p16t_ref[...])
    lane = lax.broadcasted_iota(jnp.int32, (1, LANES), 1)
    parts = [slice(s * PERM, (s + 1) * PERM) for s in range(x_ref.shape[0] // PERM)]
    blocks = range(len(parts))
    groups = tuple(zip(DILATIONS, perms))
    l_refs, o_refs = (l0_ref, l1_ref, l2_ref), (o0_ref, o1_ref, o2_ref)

    def unpack(terms):
        st = terms
        for t in range(1, STAT_PARTS):
            st = st + pltpu.roll(terms, LANES - t * STAT_W, 1)
        return st

    stats = [[unpack(_token_order(l_ref, d, pt, s)) for l_ref, (d, pt) in zip(l_refs, groups)] for s in blocks]
    ms = [jnp.maximum(jnp.maximum(st[0], st[1]), st[2]) for st in stats]
    es = [[jnp.exp2(t - m) for t in st] for st, m in zip(stats, ms)]
    dens = [[pltpu.roll(t, LANES - HEADS_PER_GROUP, 1) for t in st] for st in stats]
    invs = [1.0 / (e[0] * dn[0] + e[1] * dn[1] + e[2] * dn[2]) for e, dn in zip(es, dens)]
    wgts = []
    for e, inv in zip(es, invs):
        packed = None
        for g in range(N_GROUPS):
            al = jnp.where(lane < HEADS_PER_GROUP, e[g] * inv, 0.0)
            al = al if g == 0 else pltpu.roll(al, g * HEADS_PER_GROUP, 1)
            packed = al if packed is None else packed + al
        terms, rest = None, packed
        for t in range(STAT_PARTS):
            part = rest.astype(BF16).astype(F32)
            rest = rest - part
            part = part if t == 0 else pltpu.roll(part, t * PACK_W, 1)
            terms = part if terms is None else terms + part
        wgts.append(_dot(terms.astype(BF16), ex_ref[...]))
    outs = []
    for s, wgt in zip(blocks, wgts):
        o = None
        for g, (o_ref, (d, pt)) in enumerate(zip(o_refs, groups)):
            term = wgt[:, g * GROUP_W:(g + 1) * GROUP_W] * _token_order(o_ref, d, pt, s)
            o = term if o is None else o + term
        outs.append(o.astype(BF16))
    for p, o in zip(parts, outs):
        y_ref[p, :] = x_ref[p, :] + _dot(a_ref[p, :], w_ref[0:CONV_A_CH, :]) + _dot(o, w_ref[CONV_A_CH:, :])
    _ffn_rows(y_ref, parts, fg_ref, w1_ref, w3_ref, w2_ref)


def _even_out(x, a, os, ls, p4t, p16t, ex, w, ffn):
    batch, seq, _ = x.shape
    return pl.pallas_call(
        _even_out_kernel,
        grid=(batch, seq // TM),
        in_specs=[_rows(TM, D_MODEL), _rows(TM, CONV_A_CH)]
                 + [_classes(d, TM, GROUP_W) for d in DILATIONS] + [_classes(d, TM, LANES) for d in DILATIONS]
                 + [_resident(t) for t in (p4t, p16t, ex, w, *ffn)],
        out_specs=_rows(TM, D_MODEL),
        out_shape=jax.ShapeDtypeStruct((batch, seq, D_MODEL), F32),
        compiler_params=_params(2),
        name="even_out_ffn",
    )(x, a, *os, *ls, *_arrays((p4t, p16t, ex, w, *ffn)))


def _odd_kernel(prev_ref, x_ref, next_ref, g_ref, win_ref, cw_ref, wout_ref, fg_ref, w1_ref, w3_ref, w2_ref,
                y_ref, buf_ref, *, n_tiles):
    i = pl.program_id(1)
    tm = x_ref.shape[0]
    hb = SUBLANES
    n_split = buf_ref.shape[0]
    rows = tm // n_split
    parts = [slice(k * rows, (k + 1) * rows) for k in range(n_split)]
    g = g_ref[...]
    xs = [x_ref[p, :] for p in parts]
    hs, keep = [], []
    for k, p in enumerate(parts):
        before = prev_ref[...] if k == 0 else x_ref[p.start - hb:p.start, :]
        after = next_ref[...] if k == n_split - 1 else x_ref[p.stop:p.stop + hb, :]
        halo = jnp.concatenate([before, after], axis=0)
        hs.append(jnp.concatenate([_rmsnorm_bf16(xs[k], g), _rmsnorm_bf16(halo, g)], axis=0))
        keep.append((i > 0 if k == 0 else None, i < n_tiles - 1 if k == n_split - 1 else None))
    gates = [_dot(h[0:rows], win_ref[:, 0:D_MODEL]) for h in hs]
    cus = [_dot(h, win_ref[:, D_MODEL:2 * D_MODEL]) * _dot(h, win_ref[:, 2 * D_MODEL:3 * D_MODEL]) for h in hs]
    for k, (cu, (keep_lo, keep_hi)) in enumerate(zip(cus, keep)):
        lo, hi = cu[rows:rows + hb], cu[rows + hb:]
        buf_ref[k, 0:hb, :] = lo if keep_lo is None else jnp.where(keep_lo, lo, 0.0)
        buf_ref[k, hb:hb + rows, :] = cu[0:rows]
        buf_ref[k, hb + rows:2 * hb + rows, :] = hi if keep_hi is None else jnp.where(keep_hi, hi, 0.0)
    convs = [buf_ref[k, hb - 1:hb - 1 + rows, :] * cw_ref[0:1, :] + buf_ref[k, hb:hb + rows, :] * cw_ref[1:2, :]
             + buf_ref[k, hb + 1:hb + 1 + rows, :] * cw_ref[2:3, :] for k in range(n_split)]
    for p, x, gate, conv in zip(parts, xs, gates, convs):
        y_ref[p, :] = x + _dot((gate * conv).astype(BF16), wout_ref[...])
    _ffn_rows(y_ref, parts, fg_ref, w1_ref, w3_ref, w2_ref)


def _odd(x, g, win, cw, wout, ffn):
    batch, seq, _ = x.shape
    n_tiles = seq // TM
    per = TM // SUBLANES
    main = _rows(TM, D_MODEL)
    prev = pl.BlockSpec((None, SUBLANES, D_MODEL), lambda bi, i: (bi, jnp.maximum(i * per - 1, 0), 0))
    nxt = pl.BlockSpec((None, SUBLANES, D_MODEL), lambda bi, i: (bi, jnp.minimum((i + 1) * per, seq // SUBLANES - 1), 0))
    return pl.pallas_call(
        functools.partial(_odd_kernel, n_tiles=n_tiles),
        grid=(batch, n_tiles),
        in_specs=[prev, main, nxt] + [_resident(t) for t in (g, win, cw, wout, *ffn)],
        out_specs=main,
        out_shape=jax.ShapeDtypeStruct((batch, seq, D_MODEL), F32),
        scratch_shapes=[pltpu.VMEM((ROW_SPLIT, TM // ROW_SPLIT + 2 * SUBLANES, D_MODEL), F32)],
        compiler_params=_params(2),
        name="odd_ffn",
    )(x, x, x, *_arrays((g, win, cw, wout, *ffn)))


def _t5_bucket_np(rel):
    nb = N_BUCKETS // 2
    max_exact = nb // 2
    n = np.abs(rel)
    nf = np.maximum(n, 1).astype(np.float32)
    large = max_exact + (np.log(nf / max_exact) / math.log(MAX_DISTANCE / max_exact) * (nb - max_exact)).astype(np.int32)
    large = np.minimum(large, nb - 1)
    return np.where(rel > 0, nb, 0) + np.where(n < max_exact, n, large)


def _bias_windows(rel_bias):
    n_off = ATT_SUB + ATT_KEYS - 1
    off = np.arange(n_off) - (ATT_SUB - 1) - HALF
    band = np.abs(off) <= HALF
    out = []
    for g, d in enumerate(DILATIONS):
        tab = rel_bias[:, g * HEADS_PER_GROUP:(g + 1) * HEADS_PER_GROUP].astype(F32)
        vec = jnp.take(tab, jnp.asarray(_t5_bucket_np(off * d)), axis=0)
        vec = jnp.where(band[:, None], vec * LOG2E, NEG_INF).T
        flat = jnp.tile(jnp.pad(vec, ((0, 0), (0, 1))), (1, ATT_SUB))[:, :ATT_SUB * n_off]
        skew = flat.reshape(HEADS_PER_GROUP, ATT_SUB, n_off)
        win = skew[:, :, ATT_SUB - 1:ATT_SUB - 1 + ATT_KEYS]
        out.append(win.reshape(N_PAIRS, 2 * ATT_SUB, ATT_KEYS))
    return out


def _head_segments():
    h = np.arange(SEG_W) // HEAD_DIM
    return jnp.asarray(h[:, None] == h[None, :], BF16)


def _head_expand():
    src = np.arange(LANES)
    col = np.arange(N_GROUPS * GROUP_W)
    hit = (src[:, None] % PACK_W == (col // GROUP_W) * HEADS_PER_GROUP + (col % GROUP_W) // HEAD_DIM)[:, :]
    return jnp.asarray(hit & (src[:, None] < STAT_PARTS * PACK_W), BF16)


def _class_perm(d):
    n = PERM // d
    src = (np.arange(PERM) % n) * d + np.arange(PERM) // n
    return np.arange(PERM)[None, :] == src[:, None]


def _trunk(x, p):
    for layer in range(DEPTH):
        i = layer // 2
        ffn = tuple((p[k], layer) for k in ("ffn_norm", "ffn_w1", "ffn_w3", "ffn_w2"))
        if layer % 2 == 0:
            a, *qkvs = _even_in(x, (p["even_norm"], i), (p["even_w_in"], i), p["seg"], (p["q_gain"], i),
                                (p["k_gain"], i), p["perm"][1], p["perm"][2])
            a = _conv_a(a, *((p[k], i) for k in ("conv_a_w", "conv_a_b", "conv_a_ln_g", "conv_a_ln_b")))
            os, ls = zip(*[_attention_group(qkv, bm) for qkv, bm in zip(qkvs, p["bias"])])
            x = _even_out(x, a, os, ls, p["perm_t"][1], p["perm_t"][2], p["expand"], (p["even_w_out"], i), ffn)
        else:
            x = _odd(x, (p["odd_norm"], i), (p["odd_w_in"], i), (p["conv_c_w"], i), (p["odd_w_out"], i), ffn)
    return x


def _prepare(rel_bias, even_norm, even_w_in, conv_a_w, conv_a_b, conv_a_ln_g, conv_a_ln_b, q_norm, k_norm,
             even_w_out, odd_norm, odd_w_in, conv_c_w, odd_w_out, ffn_norm, ffn_w1, ffn_w3, ffn_w2):
    row = lambda a: a.astype(F32)[:, None, :]
    tile_heads = lambda a: jnp.tile(a.astype(F32), (1, HEADS_PER_GROUP))[:, None, :]
    perms = [None if d == 1 else _class_perm(d) for d in DILATIONS]
    return dict(
        bias=_bias_windows(rel_bias), seg=_head_segments(), expand=_head_expand(),
        perm=[None if m is None else jnp.asarray(m, BF16) for m in perms],
        perm_t=[None if m is None else jnp.asarray(m.T, BF16) for m in perms],
        even_norm=row(even_norm), even_w_in=even_w_in.astype(BF16),
        conv_a_w=jnp.broadcast_to(conv_a_w.astype(F32)[:, :, None, :], conv_a_w.shape[:2] + (SUBLANES, CONV_A_CH)),
        conv_a_b=row(conv_a_b), conv_a_ln_g=row(conv_a_ln_g), conv_a_ln_b=row(conv_a_ln_b),
        q_gain=tile_heads(q_norm) * (HEAD_DIM ** -0.5 * LOG2E), k_gain=tile_heads(k_norm),
        even_w_out=even_w_out.astype(BF16),
        odd_norm=row(odd_norm), odd_w_in=odd_w_in.astype(BF16), conv_c_w=conv_c_w.astype(F32),
        odd_w_out=odd_w_out.astype(BF16),
        ffn_norm=row(ffn_norm), ffn_w1=ffn_w1.astype(BF16), ffn_w3=ffn_w3.astype(BF16), ffn_w2=ffn_w2.astype(BF16),
    )


def kernel(x_prompt, x_sample, rel_bias, even_norm, even_w_in, conv_a_w, conv_a_b, conv_a_ln_g, conv_a_ln_b, q_norm, k_norm, even_w_out, odd_norm, odd_w_in, conv_c_w, odd_w_out, ffn_norm, ffn_w1, ffn_w3, ffn_w2):
    p = _prepare(rel_bias, even_norm, even_w_in, conv_a_w, conv_a_b, conv_a_ln_g, conv_a_ln_b, q_norm, k_norm,
                 even_w_out, odd_norm, odd_w_in, conv_c_w, odd_w_out, ffn_norm, ffn_w1, ffn_w3, ffn_w2)
    return (_trunk(x_prompt, p), _trunk(x_sample, p))
```

```python
import functools
import math

import numpy as np
import jax
import jax.numpy as jnp
from jax import lax
from jax.experimental import pallas as pl
from jax.experimental.pallas import tpu as pltpu

F32 = jnp.float32
BF16 = jnp.bfloat16

D_MODEL = 1024
DEPTH = 4
HEAD_DIM = 64
WINDOWS = (128, 512, 2048)
DILATIONS = (1, 4, 16)
N_GROUPS = 3
HEADS_PER_GROUP = 8
GROUP_W = HEADS_PER_GROUP * HEAD_DIM
CONV_A_CH = D_MODEL // 2
CONV_A_K = 31
FFN_HIDDEN = 2816
N_BUCKETS = 32
MAX_DISTANCE = 1024
EPS = 1e-6
NEG_INF = -1e30
LOG2E = math.log2(math.e)
HALF = 64
assert all(w // (2 * d) == HALF for w, d in zip(WINDOWS, DILATIONS))

V7X_VMEM_BYTES = 64 * 1024 * 1024
VMEM_LIMIT_BYTES = V7X_VMEM_BYTES * 7 // 8
LANES = 128
SUBLANES = 8
MXU_DIM = 256

TM = 512
ROW_SPLIT = 2
PERM = MXU_DIM
STAT_W = 2 * HEADS_PER_GROUP
STAT_PARTS = 3
ATT_SUB = 128
ATT_KEYS = ATT_SUB + 2 * HALF
ATT_STEP = 2048
assert ATT_KEYS == 2 * LANES
N_PAIRS = GROUP_W // LANES
CONV_A_TS = 512
CONV_A_RB = 32
CONV_A_HALO = 16
FFN_CHUNKS = ((0, 1536), (1536, 2816))
PACK_W = N_GROUPS * HEADS_PER_GROUP


def _params(n_axes):
    return pltpu.CompilerParams(dimension_semantics=("parallel",) * n_axes,
                                vmem_limit_bytes=VMEM_LIMIT_BYTES)


def _resident(a):
    if isinstance(a, tuple):
        a, layer = a
        nd = a.ndim
        return pl.BlockSpec((None,) + a.shape[1:], lambda *_: (layer,) + (0,) * (nd - 1), pipeline_mode=pl.Buffered(1))
    nd = a.ndim
    return pl.BlockSpec(a.shape, lambda *_: (0,) * nd, pipeline_mode=pl.Buffered(1))


def _arrays(params):
    return [a[0] if isinstance(a, tuple) else a for a in params]


def _rows(tm, width):
    return pl.BlockSpec((None, tm, width), lambda b, i: (b, i, 0))


def _classes(d, tm, width):
    if d == 1:
        return pl.BlockSpec((None, None, tm, width), lambda b, i: (b, 0, i, 0))
    return pl.BlockSpec((None, d, tm // d, width), lambda b, i: (b, 0, i, 0))


def _qkv_tile(d, tm):
    if d == 1:
        return pl.BlockSpec((None, None, 3, tm, GROUP_W), lambda b, i: (b, 0, 0, i, 0))
    return pl.BlockSpec((None, d, 3, tm // d, GROUP_W), lambda b, i: (b, 0, 0, i, 0))


def _rmsnorm_bf16(x, g):
    ms = jnp.mean(x * x, axis=-1, keepdims=True)
    return (x * lax.rsqrt(ms + EPS) * g).astype(BF16)


def _dot(a, b):
    return jnp.dot(a, b, preferred_element_type=F32)


def _even_in_kernel(x_ref, g_ref, w_ref, qg_ref, kg_ref, p4_ref, p16_ref,
                    a_ref, q0_ref, q1_ref, q2_ref):
    parts = [slice(s * PERM, (s + 1) * PERM) for s in range(x_ref.shape[0] // PERM)]
    hs = [_rmsnorm_bf16(x_ref[p, :], g_ref[...]) for p in parts]

    def proj(hh, c):
        return _dot(hh, w_ref[:, c * GROUP_W:(c + 1) * GROUP_W])

    vals = [proj(h, 0) for h in hs]
    gates = [proj(h, 1) for h in hs]
    for p, val, gate in zip(parts, vals, gates):
        a_ref[p, :] = val * jax.nn.sigmoid(gate)
    first_head = lax.broadcasted_iota(jnp.int32, (1, LANES), 1) < HEAD_DIM
    for g, (d, perm_ref, out_ref) in enumerate(zip(DILATIONS, (None, p4_ref, p16_ref), (q0_ref, q1_ref, q2_ref))):
        hgs = hs if d == 1 else [_dot(perm_ref[...], h).astype(BF16) for h in hs]
        for part in range(3):
            ps = [proj(hg, 2 + part * N_GROUPS + g) for hg in hgs]
            if part < 2:
                gain = qg_ref[...] if part == 0 else kg_ref[...]
                mss = []
                for p in ps:
                    cols = []
                    for c in range(0, GROUP_W, LANES):
                        sq = p[:, c:c + LANES] * p[:, c:c + LANES]
                        lo = jnp.sum(jnp.where(first_head, sq, 0.0), axis=-1, keepdims=True)
                        hi = jnp.sum(jnp.where(first_head, 0.0, sq), axis=-1, keepdims=True)
                        cols.append(jnp.where(first_head, lo, hi))
                    mss.append(jnp.concatenate(cols, axis=1))
                ps = [p * lax.rsqrt(ms * (1.0 / HEAD_DIM) + EPS) * gain for p, ms in zip(ps, mss)]
            for s, p in enumerate(ps):
                val = p.astype(BF16)
                if d == 1:
                    out_ref[part, s * PERM:(s + 1) * PERM, :] = val
                else:
                    n = PERM // d
                    for r in range(d):
                        out_ref[r, part, s * n:(s + 1) * n, :] = val[r * n:(r + 1) * n]


def _even_in(x, g, w, qg, kg, p4, p16):
    batch, seq, _ = x.shape
    outs = [jax.ShapeDtypeStruct((batch, seq, CONV_A_CH), F32)]
    outs += [jax.ShapeDtypeStruct((batch, d, 3, seq // d, GROUP_W), BF16) for d in DILATIONS]
    return pl.pallas_call(
        _even_in_kernel,
        grid=(batch, seq // TM),
        in_specs=[_rows(TM, D_MODEL)] + [_resident(a) for a in (g, w, qg, kg, p4, p16)],
        out_specs=[_rows(TM, CONV_A_CH)] + [_qkv_tile(d, TM) for d in DILATIONS],
        out_shape=outs,
        compiler_params=_params(2),
        name="even_in",
    )(x, *_arrays((g, w, qg, kg, p4, p16)))


def _conv_a_kernel(prev_ref, a_ref, next_ref, wb_ref, b_ref, lg_ref, lb_ref, o_ref, buf_ref, sh_ref, *, n_tiles):
    i = pl.program_id(1)
    ts = a_ref.shape[0]
    h = CONV_A_HALO
    rows = ts + 2 * h
    buf_ref[0:h, :] = jnp.where(i > 0, prev_ref[...], 0.0)
    buf_ref[h:h + ts, :] = a_ref[...]
    buf_ref[h + ts:rows, :] = jnp.where(i < n_tiles - 1, next_ref[...], 0.0)
    n_sh = rows - SUBLANES
    for v in range(SUBLANES):
        sh_ref[v] = buf_ref[v:v + n_sh, :].reshape(n_sh // SUBLANES, SUBLANES, CONV_A_CH)
    off = h - (CONV_A_K - 1) // 2
    nb = CONV_A_RB // SUBLANES
    for r in range(0, ts, CONV_A_RB):
        t0 = r // SUBLANES
        acc = None
        for v in range(SUBLANES):
            us = [u for u in range((CONV_A_K + off) // SUBLANES + 1) if 0 <= SUBLANES * u + v - off < CONV_A_K]
            x = sh_ref[v, t0 + us[0]:t0 + us[-1] + nb]
            for u in us:
                term = x[u - us[0]:u - us[0] + nb] * wb_ref[SUBLANES * u + v - off]
                acc = term if acc is None else acc + term
        acc = acc.reshape(CONV_A_RB, CONV_A_CH) + b_ref[...]
        mu = jnp.mean(acc, axis=-1, keepdims=True)
        xc = acc - mu
        var = jnp.mean(xc * xc, axis=-1, keepdims=True)
        y = xc * lax.rsqrt(var + EPS) * lg_ref[...] + lb_ref[...]
        o_ref[r:r + CONV_A_RB, :] = (y * jax.nn.sigmoid(y)).astype(BF16)


def _conv_a(a, wb, b, lg, lb):
    batch, seq, _ = a.shape
    ts = CONV_A_TS
    n_tiles = seq // ts
    hb = ts // CONV_A_HALO
    rows = ts + 2 * CONV_A_HALO
    main = _rows(ts, CONV_A_CH)
    prev = pl.BlockSpec((None, CONV_A_HALO, CONV_A_CH), lambda bi, i: (bi, jnp.maximum(i * hb - 1, 0), 0))
    nxt = pl.BlockSpec((None, CONV_A_HALO, CONV_A_CH),
                       lambda bi, i: (bi, jnp.minimum((i + 1) * hb, seq // CONV_A_HALO - 1), 0))
    return pl.pallas_call(
        functools.partial(_conv_a_kernel, n_tiles=n_tiles),
        grid=(batch, n_tiles),
        in_specs=[prev, main, nxt] + [_resident(t) for t in (wb, b, lg, lb)],
        out_specs=main,
        out_shape=jax.ShapeDtypeStruct((batch, seq, CONV_A_CH), BF16),
        scratch_shapes=[pltpu.VMEM((rows, CONV_A_CH), F32),
                        pltpu.VMEM((SUBLANES, rows // SUBLANES - 1, SUBLANES, CONV_A_CH), F32)],
        compiler_params=_params(2),
        name="conv_a",
    )(a, a, a, *_arrays((wb, b, lg, lb)))


def _attn_kernel(*refs, n_tiles, halo):
    if halo:
        q_ref, kp_ref, kc_ref, kn_ref, vp_ref, vc_ref, vn_ref, bm_ref, o_ref, l_ref, kw_ref, vw_ref = refs
        edges = ((kw_ref, kp_ref, kn_ref), (vw_ref, vp_ref, vn_ref))
    else:
        q_ref, kc_ref, vc_ref, bm_ref, o_ref, l_ref, kw_ref, vw_ref = refs
        edges = ((kw_ref, None, None), (vw_ref, None, None))
    i = pl.program_id(2)
    n_cls, tq = q_ref.shape[0], q_ref.shape[1]
    lane = lax.broadcasted_iota(jnp.int32, (1, LANES), 1)
    first_head = lane < HEAD_DIM
    keep_first = first_head.astype(BF16)
    keep_second = 1.0 - keep_first
    n_sub = tq // ATT_SUB
    lanes = [slice(pr * LANES, (pr + 1) * LANES) for pr in range(N_PAIRS)]
    for c in range(n_cls):
        for (dst, before, after), cur in zip(edges, (kc_ref, vc_ref)):
            pad = jnp.zeros((HALF, GROUP_W), BF16)
            dst[c, 0:HALF, :] = pad if before is None else before[c]
            dst[c, HALF:HALF + tq, :] = cur[c]
            dst[c, HALF + tq:2 * HALF + tq, :] = pad if after is None else after[c]
        for s in range(n_sub):
            r0 = s * ATT_SUB
            edge_lo = edge_hi = None
            if s == 0:
                edge_lo = jnp.where(lane >= jnp.where(i == 0, HALF, 0), 0.0, NEG_INF)
            if s == n_sub - 1:
                edge_hi = jnp.where(lane < jnp.where(i == n_tiles - 1, LANES - HALF, LANES), 0.0, NEG_INF)
            scs = []
            for pr, cs in enumerate(lanes):
                q2 = q_ref[c, r0:r0 + ATT_SUB, cs]
                qs = jnp.concatenate([q2 * keep_first, q2 * keep_second], axis=0)
                sc = lax.dot_general(qs, kw_ref[c, r0:r0 + ATT_KEYS, cs], (((1,), (1,)), ((), ())),
                                     preferred_element_type=F32) + bm_ref[pr]
                if edge_lo is not None or edge_hi is not None:
                    left, right = sc[:, :LANES], sc[:, LANES:]
                    left = left if edge_lo is None else left + edge_lo
                    right = right if edge_hi is None else right + edge_hi
                    sc = jnp.concatenate([left, right], axis=1)
                scs.append(sc)
            ms = [jnp.max(sc, axis=-1, keepdims=True) for sc in scs]
            ps = [jnp.exp2(sc - m) for sc, m in zip(scs, ms)]
            dens = [jnp.sum(p, axis=-1, keepdims=True) for p in ps]
            pvs = [_dot(p.astype(BF16), vw_ref[c, r0:r0 + ATT_KEYS, cs]) for p, cs in zip(ps, lanes)]
            stat_rows = jnp.zeros((ATT_SUB, LANES), F32)
            for pr, (cs, pv, m, den) in enumerate(zip(lanes, pvs, ms, dens)):
                o_ref[c, r0:r0 + ATT_SUB, cs] = jnp.where(first_head, pv[0:ATT_SUB], pv[ATT_SUB:]).astype(BF16)
                for hd, rows in ((2 * pr, slice(0, ATT_SUB)), (2 * pr + 1, slice(ATT_SUB, 2 * ATT_SUB))):
                    both = jnp.where(lane < HEADS_PER_GROUP, m[rows], den[rows])
                    stat_rows = jnp.where((lane == hd) | (lane == HEADS_PER_GROUP + hd), both, stat_rows)
            packed, rest = None, stat_rows
            for t in range(STAT_PARTS):
                part = rest.astype(BF16).astype(F32)
                rest = rest - part
                part = part if t == 0 else pltpu.roll(part, t * STAT_W, 1)
                packed = part if packed is None else packed + part
            l_ref[c, r0:r0 + ATT_SUB, :] = packed.astype(BF16)


def _attention_group(qkv, bm):
    batch, d, _, cls_len, _ = qkv.shape
    tq = min(cls_len, ATT_STEP)
    n_tiles = cls_len // tq
    halo = n_tiles > 1
    n_cls = 1 if halo else min(d, ATT_STEP // tq)
    hb = tq // HALF

    def main(part):
        return pl.BlockSpec((None, n_cls, None, tq, GROUP_W), lambda b, r, i: (b, r, part, i, 0))

    def before(part):
        return pl.BlockSpec((None, 1, None, HALF, GROUP_W),
                            lambda b, r, i: (b, r, part, jnp.maximum(i * hb - 1, 0), 0))

    def after(part):
        return pl.BlockSpec((None, 1, None, HALF, GROUP_W),
                            lambda b, r, i: (b, r, part, jnp.minimum((i + 1) * hb, cls_len // HALF - 1), 0))

    def out(width):
        return pl.BlockSpec((None, n_cls, tq, width), lambda b, r, i: (b, r, i, 0))

    if halo:
        in_specs = [main(0), before(1), main(1), after(1), before(2), main(2), after(2), _resident(bm)]
    else:
        in_specs = [main(0), main(1), main(2), _resident(bm)]
    window = pltpu.VMEM((n_cls, tq + 2 * HALF, GROUP_W), BF16)
    return pl.pallas_call(
        functools.partial(_attn_kernel, n_tiles=n_tiles, halo=halo),
        grid=(batch, d // n_cls, n_tiles),
        in_specs=in_specs,
        out_specs=[out(GROUP_W), out(LANES)],
        out_shape=[jax.ShapeDtypeStruct((batch, d, cls_len, GROUP_W), BF16),
                   jax.ShapeDtypeStruct((batch, d, cls_len, LANES), BF16)],
        scratch_shapes=[window, window],
        compiler_params=_params(3),
        name=f"attn_d{d}",
    )(*([qkv] * (len(in_specs) - 1)), bm)


def _token_order(ref, d, perm_t, s):
    if d == 1:
        return ref[s * PERM:(s + 1) * PERM, :].astype(F32)
    n = PERM // d
    cm = jnp.concatenate([ref[r, s * n:(s + 1) * n, :] for r in range(d)], axis=0)
    return _dot(perm_t, cm)


def _ffn_rows(y_ref, parts, g_ref, w1_ref, w3_ref, w2_ref):
    xs = [y_ref[p, :] for p in parts]
    hs = [_rmsnorm_bf16(x, g_ref[...]) for x in xs]
    ys = xs
    for lo, hi in FFN_CHUNKS:
        acts = []
        for h in hs:
            a = _dot(h, w1_ref[:, lo:hi])
            acts.append((a * jax.nn.sigmoid(a) * _dot(h, w3_ref[:, lo:hi])).astype(BF16))
        ys = [y + _dot(act, w2_ref[lo:hi, :]) for y, act in zip(ys, acts)]
    for p, y in zip(parts, ys):
        y_ref[p, :] = y


def _even_out_kernel(x_ref, a_ref, o0_ref, o1_ref, o2_ref, l0_ref, l1_ref, l2_ref, p4t_ref, p16t_ref, ex_ref,
                     w_ref, fg_ref, w1_ref, w3_ref, w2_ref, y_ref):
    perms = (None, p4t_ref[...], p16t_ref[...])
    lane = lax.broadcasted_iota(jnp.int32, (1, LANES), 1)
    parts = [slice(s * PERM, (s + 1) * PERM) for s in range(x_ref.shape[0] // PERM)]
    blocks = range(len(parts))
    groups = tuple(zip(DILATIONS, perms))
    l_refs, o_refs = (l0_ref, l1_ref, l2_ref), (o0_ref, o1_ref, o2_ref)

    def unpack(terms):
        st = terms
        for t in range(1, STAT_PARTS):
            st = st + pltpu.roll(terms, LANES - t * STAT_W, 1)
        return st

    stats = [[unpack(_token_order(l_ref, d, pt, s)) for l_ref, (d, pt) in zip(l_refs, groups)] for s in blocks]
    ms = [jnp.maximum(jnp.maximum(st[0], st[1]), st[2]) for st in stats]
    es = [[jnp.exp2(t - m) for t in st] for st, m in zip(stats, ms)]
    dens = [[pltpu.roll(t, LANES - HEADS_PER_GROUP, 1) for t in st] for st in stats]
    invs = [1.0 / (e[0] * dn[0] + e[1] * dn[1] + e[2] * dn[2]) for e, dn in zip(es, dens)]
    wgts = []
    for e, inv in zip(es, invs):
        packed = None
        for g in range(N_GROUPS):
            al = jnp.where(lane < HEADS_PER_GROUP, e[g] * inv, 0.0)
            al = al if g == 0 else pltpu.roll(al, g * HEADS_PER_GROUP, 1)
            packed = al if packed is None else packed + al
        terms, rest = None, packed
        for t in range(STAT_PARTS):
            part = rest.astype(BF16).astype(F32)
            rest = rest - part
            part = part if t == 0 else pltpu.roll(part, t * PACK_W, 1)
            terms = part if terms is None else terms + part
        wgts.append(_dot(terms.astype(BF16), ex_ref[...]))
    outs = []
    for s, wgt in zip(blocks, wgts):
        o = None
        for g, (o_ref, (d, pt)) in enumerate(zip(o_refs, groups)):
            term = wgt[:, g * GROUP_W:(g + 1) * GROUP_W] * _token_order(o_ref, d, pt, s)
            o = term if o is None else o + term
        outs.append(o.astype(BF16))
    for p, o in zip(parts, outs):
        y_ref[p, :] = x_ref[p, :] + _dot(a_ref[p, :], w_ref[0:CONV_A_CH, :]) + _dot(o, w_ref[CONV_A_CH:, :])
    _ffn_rows(y_ref, parts, fg_ref, w1_ref, w3_ref, w2_ref)


def _even_out(x, a, os, ls, p4t, p16t, ex, w, ffn):
    batch, seq, _ = x.shape
    return pl.pallas_call(
        _even_out_kernel,
        grid=(batch, seq // TM),
        in_specs=[_rows(TM, D_MODEL), _rows(TM, CONV_A_CH)]
                 + [_classes(d, TM, GROUP_W) for d in DILATIONS] + [_classes(d, TM, LANES) for d in DILATIONS]
                 + [_resident(t) for t in (p4t, p16t, ex, w, *ffn)],
        out_specs=_rows(TM, D_MODEL),
        out_shape=jax.ShapeDtypeStruct((batch, seq, D_MODEL), F32),
        compiler_params=_params(2),
        name="even_out_ffn",
    )(x, a, *os, *ls, *_arrays((p4t, p16t, ex, w, *ffn)))


def _odd_kernel(prev_ref, x_ref, next_ref, g_ref, win_ref, cw_ref, wout_ref, fg_ref, w1_ref, w3_ref, w2_ref,
                y_ref, buf_ref, *, n_tiles):
    i = pl.program_id(1)
    tm = x_ref.shape[0]
    hb = SUBLANES
    n_split = buf_ref.shape[0]
    rows = tm // n_split
    parts = [slice(k * rows, (k + 1) * rows) for k in range(n_split)]
    g = g_ref[...]
    xs = [x_ref[p, :] for p in parts]
    hs, keep = [], []
    for k, p in enumerate(parts):
        before = prev_ref[...] if k == 0 else x_ref[p.start - hb:p.start, :]
        after = next_ref[...] if k == n_split - 1 else x_ref[p.stop:p.stop + hb, :]
        halo = jnp.concatenate([before, after], axis=0)
        hs.append(jnp.concatenate([_rmsnorm_bf16(xs[k], g), _rmsnorm_bf16(halo, g)], axis=0))
        keep.append((i > 0 if k == 0 else None, i < n_tiles - 1 if k == n_split - 1 else None))
    gates = [_dot(h[0:rows], win_ref[:, 0:D_MODEL]) for h in hs]
    cus = [_dot(h, win_ref[:, D_MODEL:2 * D_MODEL]) * _dot(h, win_ref[:, 2 * D_MODEL:3 * D_MODEL]) for h in hs]
    for k, (cu, (keep_lo, keep_hi)) in enumerate(zip(cus, keep)):
        lo, hi = cu[rows:rows + hb], cu[rows + hb:]
        buf_ref[k, 0:hb, :] = lo if keep_lo is None else jnp.where(keep_lo, lo, 0.0)
        buf_ref[k, hb:hb + rows, :] = cu[0:rows]
        buf_ref[k, hb + rows:2 * hb + rows, :] = hi if keep_hi is None else jnp.where(keep_hi, hi, 0.0)
    convs = [buf_ref[k, hb - 1:hb - 1 + rows, :] * cw_ref[0:1, :] + buf_ref[k, hb:hb + rows, :] * cw_ref[1:2, :]
             + buf_ref[k, hb + 1:hb + 1 + rows, :] * cw_ref[2:3, :] for k in range(n_split)]
    for p, x, gate, conv in zip(parts, xs, gates, convs):
        y_ref[p, :] = x + _dot((gate * conv).astype(BF16), wout_ref[...])
    _ffn_rows(y_ref, parts, fg_ref, w1_ref, w3_ref, w2_ref)


def _odd(x, g, win, cw, wout, ffn):
    batch, seq, _ = x.shape
    n_tiles = seq // TM
    per = TM // SUBLANES
    main = _rows(TM, D_MODEL)
    prev = pl.BlockSpec((None, SUBLANES, D_MODEL), lambda bi, i: (bi, jnp.maximum(i * per - 1, 0), 0))
    nxt = pl.BlockSpec((None, SUBLANES, D_MODEL), lambda bi, i: (bi, jnp.minimum((i + 1) * per, seq // SUBLANES - 1), 0))
    return pl.pallas_call(
        functools.partial(_odd_kernel, n_tiles=n_tiles),
        grid=(batch, n_tiles),
        in_specs=[prev, main, nxt] + [_resident(t) for t in (g, win, cw, wout, *ffn)],
        out_specs=main,
        out_shape=jax.ShapeDtypeStruct((batch, seq, D_MODEL), F32),
        scratch_shapes=[pltpu.VMEM((ROW_SPLIT, TM // ROW_SPLIT + 2 * SUBLANES, D_MODEL), F32)],
        compiler_params=_params(2),
        name="odd_ffn",
    )(x, x, x, *_arrays((g, win, cw, wout, *ffn)))


def _t5_bucket_np(rel):
    nb = N_BUCKETS // 2
    max_exact = nb // 2
    n = np.abs(rel)
    nf = np.maximum(n, 1).astype(np.float32)
    large = max_exact + (np.log(nf / max_exact) / math.log(MAX_DISTANCE / max_exact) * (nb - max_exact)).astype(np.int32)
    large = np.minimum(large, nb - 1)
    return np.where(rel > 0, nb, 0) + np.where(n < max_exact, n, large)


def _bias_windows(rel_bias):
    n_off = ATT_SUB + ATT_KEYS - 1
    off = np.arange(n_off) - (ATT_SUB - 1) - HALF
    band = np.abs(off) <= HALF
    out = []
    for g, d in enumerate(DILATIONS):
        tab = rel_bias[:, g * HEADS_PER_GROUP:(g + 1) * HEADS_PER_GROUP].astype(F32)
        vec = jnp.take(tab, jnp.asarray(_t5_bucket_np(off * d)), axis=0)
        vec = jnp.where(band[:, None], vec * LOG2E, NEG_INF).T
        flat = jnp.tile(jnp.pad(vec, ((0, 0), (0, 1))), (1, ATT_SUB))[:, :ATT_SUB * n_off]
        skew = flat.reshape(HEADS_PER_GROUP, ATT_SUB, n_off)
        win = skew[:, :, ATT_SUB - 1:ATT_SUB - 1 + ATT_KEYS]
        out.append(win.reshape(N_PAIRS, 2 * ATT_SUB, ATT_KEYS))
    return out


def _head_expand():
    src = np.arange(LANES)
    col = np.arange(N_GROUPS * GROUP_W)
    hit = (src[:, None] % PACK_W == (col // GROUP_W) * HEADS_PER_GROUP + (col % GROUP_W) // HEAD_DIM)[:, :]
    return jnp.asarray(hit & (src[:, None] < STAT_PARTS * PACK_W), BF16)


def _class_perm(d):
    n = PERM // d
    src = (np.arange(PERM) % n) * d + np.arange(PERM) // n
    return np.arange(PERM)[None, :] == src[:, None]


def _trunk(x, p):
    for layer in range(DEPTH):
        i = layer // 2
        ffn = tuple((p[k], layer) for k in ("ffn_norm", "ffn_w1", "ffn_w3", "ffn_w2"))
        if layer % 2 == 0:
            a, *qkvs = _even_in(x, (p["even_norm"], i), (p["even_w_in"], i), (p["q_gain"], i),
                                (p["k_gain"], i), p["perm"][1], p["perm"][2])
            a = _conv_a(a, *((p[k], i) for k in ("conv_a_w", "conv_a_b", "conv_a_ln_g", "conv_a_ln_b")))
            os, ls = zip(*[_attention_group(qkv, bm) for qkv, bm in zip(qkvs, p["bias"])])
            x = _even_out(x, a, os, ls, p["perm_t"][1], p["perm_t"][2], p["expand"], (p["even_w_out"], i), ffn)
        else:
            x = _odd(x, (p["odd_norm"], i), (p["odd_w_in"], i), (p["conv_c_w"], i), (p["odd_w_out"], i), ffn)
    return x


def _prepare(rel_bias, even_norm, even_w_in, conv_a_w, conv_a_b, conv_a_ln_g, conv_a_ln_b, q_norm, k_norm,
             even_w_out, odd_norm, odd_w_in, conv_c_w, odd_w_out, ffn_norm, ffn_w1, ffn_w3, ffn_w2):
    row = lambda a: a.astype(F32)[:, None, :]
    tile_heads = lambda a: jnp.tile(a.astype(F32), (1, HEADS_PER_GROUP))[:, None, :]
    perms = [None if d == 1 else _class_perm(d) for d in DILATIONS]
    return dict(
        bias=_bias_windows(rel_bias), expand=_head_expand(),
        perm=[None if m is None else jnp.asarray(m, BF16) for m in perms],
        perm_t=[None if m is None else jnp.asarray(m.T, BF16) for m in perms],
        even_norm=row(even_norm), even_w_in=even_w_in.astype(BF16),
        conv_a_w=jnp.broadcast_to(conv_a_w.astype(F32)[:, :, None, :], conv_a_w.shape[:2] + (SUBLANES, CONV_A_CH)),
        conv_a_b=row(conv_a_b), conv_a_ln_g=row(conv_a_ln_g), conv_a_ln_b=row(conv_a_ln_b),
        q_gain=tile_heads(q_norm) * (HEAD_DIM ** -0.5 * LOG2E), k_gain=tile_heads(k_norm),
        even_w_out=even_w_out.astype(BF16),
        odd_norm=row(odd_norm), odd_w_in=odd_w_in.astype(BF16), conv_c_w=conv_c_w.astype(F32),
        odd_w_out=odd_w_out.astype(BF16),
        ffn_norm=row(ffn_norm), ffn_w1=ffn_w1.astype(BF16), ffn_w3=ffn_w3.astype(BF16), ffn_w2=ffn_w2.astype(BF16),
    )


def kernel(x_prompt, x_sample, rel_bias, even_norm, even_w_in, conv_a_w, conv_a_b, conv_a_ln_g, conv_a_ln_b, q_norm, k_norm, even_w_out, odd_norm, odd_w_in, conv_c_w, odd_w_out, ffn_norm, ffn_w1, ffn_w3, ffn_w2):
    p = _prepare(rel_bias, even_norm, even_w_in, conv_a_w, conv_a_b, conv_a_ln_g, conv_a_ln_b, q_norm, k_norm,
                 even_w_out, odd_norm, odd_w_in, conv_c_w, odd_w_out, ffn_norm, ffn_w1, ffn_w3, ffn_w2)
    return (_trunk(x_prompt, p), _trunk(x_sample, p))
```

```python
import functools
import math

import numpy as np
import jax
import jax.numpy as jnp
from jax import lax
from jax.experimental import pallas as pl
from jax.experimental.pallas import tpu as pltpu

F32 = jnp.float32
BF16 = jnp.bfloat16

D_MODEL = 1024
DEPTH = 4
HEAD_DIM = 64
WINDOWS = (128, 512, 2048)
DILATIONS = (1, 4, 16)
N_GROUPS = 3
HEADS_PER_GROUP = 8
GROUP_W = HEADS_PER_GROUP * HEAD_DIM
CONV_A_CH = D_MODEL // 2
CONV_A_K = 31
FFN_HIDDEN = 2816
N_BUCKETS = 32
MAX_DISTANCE = 1024
EPS = 1e-6
NEG_INF = -1e30
LOG2E = math.log2(math.e)
HALF = 64
assert all(w // (2 * d) == HALF for w, d in zip(WINDOWS, DILATIONS))

V7X_VMEM_BYTES = 64 * 1024 * 1024
VMEM_LIMIT_BYTES = V7X_VMEM_BYTES * 7 // 8
LANES = 128
SUBLANES = 8
MXU_DIM = 256

TM = 512
ROW_SPLIT = 2
PERM = MXU_DIM
STAT_W = 2 * HEADS_PER_GROUP
STAT_PARTS = 3
ATT_SUB = 128
ATT_KEYS = ATT_SUB + 2 * HALF
ATT_STEP = 2048
assert ATT_KEYS == 2 * LANES
N_PAIRS = GROUP_W // LANES
CONV_A_TS = 512
CONV_A_RB = 32
CONV_A_HALO = 16
FFN_CHUNKS = ((0, 1536), (1536, 2816))
PACK_W = N_GROUPS * HEADS_PER_GROUP


def _params(n_axes):
    return pltpu.CompilerParams(dimension_semantics=("parallel",) * n_axes,
                                vmem_limit_bytes=VMEM_LIMIT_BYTES)


def _resident(a):
    if isinstance(a, tuple):
        a, layer = a
        nd = a.ndim
        return pl.BlockSpec((None,) + a.shape[1:], lambda *_: (layer,) + (0,) * (nd - 1), pipeline_mode=pl.Buffered(1))
    nd = a.ndim
    return pl.BlockSpec(a.shape, lambda *_: (0,) * nd, pipeline_mode=pl.Buffered(1))


def _arrays(params):
    return [a[0] if isinstance(a, tuple) else a for a in params]


def _rows(tm, width):
    return pl.BlockSpec((None, tm, width), lambda b, i: (b, i, 0))


def _classes(d, tm, width):
    if d == 1:
        return pl.BlockSpec((None, None, tm, width), lambda b, i: (b, 0, i, 0))
    return pl.BlockSpec((None, d, tm // d, width), lambda b, i: (b, 0, i, 0))


def _qkv_tile(d, tm):
    if d == 1:
        return pl.BlockSpec((None, None, 3, tm, GROUP_W), lambda b, i: (b, 0, 0, i, 0))
    return pl.BlockSpec((None, d, 3, tm // d, GROUP_W), lambda b, i: (b, 0, 0, i, 0))


def _rmsnorm_bf16(x, g):
    ms = jnp.mean(x * x, axis=-1, keepdims=True)
    return (x * lax.rsqrt(ms + EPS) * g).astype(BF16)


def _dot(a, b):
    return jnp.dot(a, b, preferred_element_type=F32)


def _even_in_kernel(x_ref, g_ref, w_ref, qg_ref, kg_ref, a_ref, q0_ref, q1_ref, q2_ref, slab_ref):
    parts = [slice(s * PERM, (s + 1) * PERM) for s in range(x_ref.shape[0] // PERM)]
    x = x_ref[...]
    hf = x * lax.rsqrt(jnp.mean(x * x, axis=-1, keepdims=True) + EPS) * g_ref[...]
    hs = [hf[p].astype(BF16) for p in parts]
    for k in range(D_MODEL // LANES):
        slab_ref[k] = hf[:, k * LANES:(k + 1) * LANES]

    def class_major(s, d):
        n = PERM // d
        slabs = [jnp.concatenate([slab_ref[k, pl.ds(s * PERM + r, n, stride=d), :] for r in range(d)], axis=0)
                 for k in range(D_MODEL // LANES)]
        return jnp.concatenate(slabs, axis=1).astype(BF16)

    def proj(hh, c):
        return _dot(hh, w_ref[:, c * GROUP_W:(c + 1) * GROUP_W])

    vals = [proj(h, 0) for h in hs]
    gates = [proj(h, 1) for h in hs]
    for p, val, gate in zip(parts, vals, gates):
        a_ref[p, :] = val * jax.nn.sigmoid(gate)
    first_head = lax.broadcasted_iota(jnp.int32, (1, LANES), 1) < HEAD_DIM
    for g, (d, out_ref) in enumerate(zip(DILATIONS, (q0_ref, q1_ref, q2_ref))):
        hgs = hs if d == 1 else [class_major(s, d) for s in range(len(parts))]
        for part in range(3):
            ps = [proj(hg, 2 + part * N_GROUPS + g) for hg in hgs]
            if part < 2:
                gain = qg_ref[...] if part == 0 else kg_ref[...]
                mss = []
                for p in ps:
                    cols = []
                    for c in range(0, GROUP_W, LANES):
                        sq = p[:, c:c + LANES] * p[:, c:c + LANES]
                        lo = jnp.sum(jnp.where(first_head, sq, 0.0), axis=-1, keepdims=True)
                        hi = jnp.sum(jnp.where(first_head, 0.0, sq), axis=-1, keepdims=True)
                        cols.append(jnp.where(first_head, lo, hi))
                    mss.append(jnp.concatenate(cols, axis=1))
                ps = [p * lax.rsqrt(ms * (1.0 / HEAD_DIM) + EPS) * gain for p, ms in zip(ps, mss)]
            for s, p in enumerate(ps):
                val = p.astype(BF16)
                if d == 1:
                    out_ref[part, s * PERM:(s + 1) * PERM, :] = val
                else:
                    n = PERM // d
                    for r in range(d):
                        out_ref[r, part, s * n:(s + 1) * n, :] = val[r * n:(r + 1) * n]


def _even_in(x, g, w, qg, kg):
    batch, seq, _ = x.shape
    outs = [jax.ShapeDtypeStruct((batch, seq, CONV_A_CH), F32)]
    outs += [jax.ShapeDtypeStruct((batch, d, 3, seq // d, GROUP_W), BF16) for d in DILATIONS]
    return pl.pallas_call(
        _even_in_kernel,
        grid=(batch, seq // TM),
        in_specs=[_rows(TM, D_MODEL)] + [_resident(a) for a in (g, w, qg, kg)],
        out_specs=[_rows(TM, CONV_A_CH)] + [_qkv_tile(d, TM) for d in DILATIONS],
        out_shape=outs,
        scratch_shapes=[pltpu.VMEM((D_MODEL // LANES, TM, LANES), F32)],
        compiler_params=_params(2),
        name="even_in",
    )(x, *_arrays((g, w, qg, kg)))


def _conv_a_kernel(prev_ref, a_ref, next_ref, wb_ref, b_ref, lg_ref, lb_ref, o_ref, buf_ref, sh_ref, *, n_tiles):
    i = pl.program_id(1)
    ts = a_ref.shape[0]
    h = CONV_A_HALO
    rows = ts + 2 * h
    buf_ref[0:h, :] = jnp.where(i > 0, prev_ref[...], 0.0)
    buf_ref[h:h + ts, :] = a_ref[...]
    buf_ref[h + ts:rows, :] = jnp.where(i < n_tiles - 1, next_ref[...], 0.0)
    n_sh = rows - SUBLANES
    for v in range(SUBLANES):
        sh_ref[v] = buf_ref[v:v + n_sh, :].reshape(n_sh // SUBLANES, SUBLANES, CONV_A_CH)
    off = h - (CONV_A_K - 1) // 2
    nb = CONV_A_RB // SUBLANES
    for r in range(0, ts, CONV_A_RB):
        t0 = r // SUBLANES
        acc = None
        for v in range(SUBLANES):
            us = [u for u in range((CONV_A_K + off) // SUBLANES + 1) if 0 <= SUBLANES * u + v - off < CONV_A_K]
            x = sh_ref[v, t0 + us[0]:t0 + us[-1] + nb]
            for u in us:
                term = x[u - us[0]:u - us[0] + nb] * wb_ref[SUBLANES * u + v - off]
                acc = term if acc is None else acc + term
        acc = acc.reshape(CONV_A_RB, CONV_A_CH) + b_ref[...]
        mu = jnp.mean(acc, axis=-1, keepdims=True)
        xc = acc - mu
        var = jnp.mean(xc * xc, axis=-1, keepdims=True)
        y = xc * lax.rsqrt(var + EPS) * lg_ref[...] + lb_ref[...]
        o_ref[r:r + CONV_A_RB, :] = (y * jax.nn.sigmoid(y)).astype(BF16)


def _conv_a(a, wb, b, lg, lb):
    batch, seq, _ = a.shape
    ts = CONV_A_TS
    n_tiles = seq // ts
    hb = ts // CONV_A_HALO
    rows = ts + 2 * CONV_A_HALO
    main = _rows(ts, CONV_A_CH)
    prev = pl.BlockSpec((None, CONV_A_HALO, CONV_A_CH), lambda bi, i: (bi, jnp.maximum(i * hb - 1, 0), 0))
    nxt = pl.BlockSpec((None, CONV_A_HALO, CONV_A_CH),
                       lambda bi, i: (bi, jnp.minimum((i + 1) * hb, seq // CONV_A_HALO - 1), 0))
    return pl.pallas_call(
        functools.partial(_conv_a_kernel, n_tiles=n_tiles),
        grid=(batch, n_tiles),
        in_specs=[prev, main, nxt] + [_resident(t) for t in (wb, b, lg, lb)],
        out_specs=main,
        out_shape=jax.ShapeDtypeStruct((batch, seq, CONV_A_CH), BF16),
        scratch_shapes=[pltpu.VMEM((rows, CONV_A_CH), F32),
                        pltpu.VMEM((SUBLANES, rows // SUBLANES - 1, SUBLANES, CONV_A_CH), F32)],
        compiler_params=_params(2),
        name="conv_a",
    )(a, a, a, *_arrays((wb, b, lg, lb)))


def _attn_kernel(*refs, n_tiles, halo):
    if halo:
        q_ref, kp_ref, kc_ref, kn_ref, vp_ref, vc_ref, vn_ref, bm_ref, o_ref, l_ref, kw_ref, vw_ref = refs
        edges = ((kw_ref, kp_ref, kn_ref), (vw_ref, vp_ref, vn_ref))
    else:
        q_ref, kc_ref, vc_ref, bm_ref, o_ref, l_ref, kw_ref, vw_ref = refs
        edges = ((kw_ref, None, None), (vw_ref, None, None))
    i = pl.program_id(2)
    n_cls, tq = q_ref.shape[0], q_ref.shape[1]
    lane = lax.broadcasted_iota(jnp.int32, (1, LANES), 1)
    first_head = lane < HEAD_DIM
    keep_first = first_head.astype(BF16)
    keep_second = 1.0 - keep_first
    n_sub = tq // ATT_SUB
    lanes = [slice(pr * LANES, (pr + 1) * LANES) for pr in range(N_PAIRS)]
    for c in range(n_cls):
        for (dst, before, after), cur in zip(edges, (kc_ref, vc_ref)):
            pad = jnp.zeros((HALF, GROUP_W), BF16)
            dst[c, 0:HALF, :] = pad if before is None else before[c]
            dst[c, HALF:HALF + tq, :] = cur[c]
            dst[c, HALF + tq:2 * HALF + tq, :] = pad if after is None else after[c]
        for s in range(n_sub):
            r0 = s * ATT_SUB
            edge_lo = edge_hi = None
            if s == 0:
                edge_lo = jnp.where(lane >= jnp.where(i == 0, HALF, 0), 0.0, NEG_INF)
            if s == n_sub - 1:
                edge_hi = jnp.where(lane < jnp.where(i == n_tiles - 1, LANES - HALF, LANES), 0.0, NEG_INF)
            scs = []
            for pr, cs in enumerate(lanes):
                q2 = q_ref[c, r0:r0 + ATT_SUB, cs]
                qs = jnp.concatenate([q2 * keep_first, q2 * keep_second], axis=0)
                sc = lax.dot_general(qs, kw_ref[c, r0:r0 + ATT_KEYS, cs], (((1,), (1,)), ((), ())),
                                     preferred_element_type=F32) + bm_ref[pr]
                if edge_lo is not None or edge_hi is not None:
                    left, right = sc[:, :LANES], sc[:, LANES:]
                    left = left if edge_lo is None else left + edge_lo
                    right = right if edge_hi is None else right + edge_hi
                    sc = jnp.concatenate([left, right], axis=1)
                scs.append(sc)
            ms = [jnp.max(sc, axis=-1, keepdims=True) for sc in scs]
            ps = [jnp.exp2(sc - m) for sc, m in zip(scs, ms)]
            dens = [jnp.sum(p, axis=-1, keepdims=True) for p in ps]
            pvs = [_dot(p.astype(BF16), vw_ref[c, r0:r0 + ATT_KEYS, cs]) for p, cs in zip(ps, lanes)]
            stat_rows = jnp.zeros((ATT_SUB, LANES), F32)
            for pr, (cs, pv, m, den) in enumerate(zip(lanes, pvs, ms, dens)):
                o_ref[c, r0:r0 + ATT_SUB, cs] = jnp.where(first_head, pv[0:ATT_SUB], pv[ATT_SUB:]).astype(BF16)
                for hd, rows in ((2 * pr, slice(0, ATT_SUB)), (2 * pr + 1, slice(ATT_SUB, 2 * ATT_SUB))):
                    both = jnp.where(lane < HEADS_PER_GROUP, m[rows], den[rows])
                    stat_rows = jnp.where((lane == hd) | (lane == HEADS_PER_GROUP + hd), both, stat_rows)
            packed, rest = None, stat_rows
            for t in range(STAT_PARTS):
                part = rest.astype(BF16).astype(F32)
                rest = rest - part
                part = part if t == 0 else pltpu.roll(part, t * STAT_W, 1)
                packed = part if packed is None else packed + part
            l_ref[c, r0:r0 + ATT_SUB, :] = packed.astype(BF16)


def _attention_group(qkv, bm):
    batch, d, _, cls_len, _ = qkv.shape
    tq = min(cls_len, ATT_STEP)
    n_tiles = cls_len // tq
    halo = n_tiles > 1
    n_cls = 1 if halo else min(d, ATT_STEP // tq)
    hb = tq // HALF

    def main(part):
        return pl.BlockSpec((None, n_cls, None, tq, GROUP_W), lambda b, r, i: (b, r, part, i, 0))

    def before(part):
        return pl.BlockSpec((None, 1, None, HALF, GROUP_W),
                            lambda b, r, i: (b, r, part, jnp.maximum(i * hb - 1, 0), 0))

    def after(part):
        return pl.BlockSpec((None, 1, None, HALF, GROUP_W),
                            lambda b, r, i: (b, r, part, jnp.minimum((i + 1) * hb, cls_len // HALF - 1), 0))

    def out(width):
        return pl.BlockSpec((None, n_cls, tq, width), lambda b, r, i: (b, r, i, 0))

    if halo:
        in_specs = [main(0), before(1), main(1), after(1), before(2), main(2), after(2), _resident(bm)]
    else:
        in_specs = [main(0), main(1), main(2), _resident(bm)]
    window = pltpu.VMEM((n_cls, tq + 2 * HALF, GROUP_W), BF16)
    return pl.pallas_call(
        functools.partial(_attn_kernel, n_tiles=n_tiles, halo=halo),
        grid=(batch, d // n_cls, n_tiles),
        in_specs=in_specs,
        out_specs=[out(GROUP_W), out(LANES)],
        out_shape=[jax.ShapeDtypeStruct((batch, d, cls_len, GROUP_W), BF16),
                   jax.ShapeDtypeStruct((batch, d, cls_len, LANES), BF16)],
        scratch_shapes=[window, window],
        compiler_params=_params(3),
        name=f"attn_d{d}",
    )(*([qkv] * (len(in_specs) - 1)), bm)


def _token_order(ref, d, perm_t, s):
    if d == 1:
        return ref[s * PERM:(s + 1) * PERM, :].astype(F32)
    n = PERM // d
    cm = jnp.concatenate([ref[r, s * n:(s + 1) * n, :] for r in range(d)], axis=0)
    return _dot(perm_t, cm)


def _ffn_rows(y_ref, parts, g_ref, w1_ref, w3_ref, w2_ref):
    xs = [y_ref[p, :] for p in parts]
    hs = [_rmsnorm_bf16(x, g_ref[...]) for x in xs]
    ys = xs
    for lo, hi in FFN_CHUNKS:
        acts = []
        for h in hs:
            a = _dot(h, w1_ref[:, lo:hi])
            acts.append((a * jax.nn.sigmoid(a) * _dot(h, w3_ref[:, lo:hi])).astype(BF16))
        ys = [y + _dot(act, w2_ref[lo:hi, :]) for y, act in zip(ys, acts)]
    for p, y in zip(parts, ys):
        y_ref[p, :] = y


def _even_out_kernel(x_ref, a_ref, o0_ref, o1_ref, o2_ref, l0_ref, l1_ref, l2_ref, p4t_ref, p16t_ref, ex_ref,
                     w_ref, fg_ref, w1_ref, w3_ref, w2_ref, y_ref):
    perms = (None, p4t_ref[...], p16t_ref[...])
    lane = lax.broadcasted_iota(jnp.int32, (1, LANES), 1)
    parts = [slice(s * PERM, (s + 1) * PERM) for s in range(x_ref.shape[0] // PERM)]
    blocks = range(len(parts))
    groups = tuple(zip(DILATIONS, perms))
    l_refs, o_refs = (l0_ref, l1_ref, l2_ref), (o0_ref, o1_ref, o2_ref)

    def unpack(terms):
        st = terms
        for t in range(1, STAT_PARTS):
            st = st + pltpu.roll(terms, LANES - t * STAT_W, 1)
        return st

    stats = [[unpack(_token_order(l_ref, d, pt, s)) for l_ref, (d, pt) in zip(l_refs, groups)] for s in blocks]
    ms = [jnp.maximum(jnp.maximum(st[0], st[1]), st[2]) for st in stats]
    es = [[jnp.exp2(t - m) for t in st] for st, m in zip(stats, ms)]
    dens = [[pltpu.roll(t, LANES - HEADS_PER_GROUP, 1) for t in st] for st in stats]
    invs = [1.0 / (e[0] * dn[0] + e[1] * dn[1] + e[2] * dn[2]) for e, dn in zip(es, dens)]
    wgts = []
    for e, inv in zip(es, invs):
        packed = None
        for g in range(N_GROUPS):
            al = jnp.where(lane < HEADS_PER_GROUP, e[g] * inv, 0.0)
            al = al if g == 0 else pltpu.roll(al, g * HEADS_PER_GROUP, 1)
            packed = al if packed is None else packed + al
        terms, rest = None, packed
        for t in range(STAT_PARTS):
            part = rest.astype(BF16).astype(F32)
            rest = rest - part
            part = part if t == 0 else pltpu.roll(part, t * PACK_W, 1)
            terms = part if terms is None else terms + part
        wgts.append(_dot(terms.astype(BF16), ex_ref[...]))
    outs = []
    for s, wgt in zip(blocks, wgts):
        o = None
        for g, (o_ref, (d, pt)) in enumerate(zip(o_refs, groups)):
            term = wgt[:, g * GROUP_W:(g + 1) * GROUP_W] * _token_order(o_ref, d, pt, s)
            o = term if o is None else o + term
        outs.append(o.astype(BF16))
    for p, o in zip(parts, outs):
        y_ref[p, :] = x_ref[p, :] + _dot(a_ref[p, :], w_ref[0:CONV_A_CH, :]) + _dot(o, w_ref[CONV_A_CH:, :])
    _ffn_rows(y_ref, parts, fg_ref, w1_ref, w3_ref, w2_ref)


def _even_out(x, a, os, ls, p4t, p16t, ex, w, ffn):
    batch, seq, _ = x.shape
    return pl.pallas_call(
        _even_out_kernel,
        grid=(batch, seq // TM),
        in_specs=[_rows(TM, D_MODEL), _rows(TM, CONV_A_CH)]
                 + [_classes(d, TM, GROUP_W) for d in DILATIONS] + [_classes(d, TM, LANES) for d in DILATIONS]
                 + [_resident(t) for t in (p4t, p16t, ex, w, *ffn)],
        out_specs=_rows(TM, D_MODEL),
        out_shape=jax.ShapeDtypeStruct((batch, seq, D_MODEL), F32),
        compiler_params=_params(2),
        name="even_out_ffn",
    )(x, a, *os, *ls, *_arrays((p4t, p16t, ex, w, *ffn)))


def _odd_kernel(prev_ref, x_ref, next_ref, g_ref, win_ref, cw_ref, wout_ref, fg_ref, w1_ref, w3_ref, w2_ref,
                y_ref, buf_ref, *, n_tiles):
    i = pl.program_id(1)
    tm = x_ref.shape[0]
    hb = SUBLANES
    n_split = buf_ref.shape[0]
    rows = tm // n_split
    parts = [slice(k * rows, (k + 1) * rows) for k in range(n_split)]
    g = g_ref[...]
    xs = [x_ref[p, :] for p in parts]
    hs, keep = [], []
    for k, p in enumerate(parts):
        before = prev_ref[...] if k == 0 else x_ref[p.start - hb:p.start, :]
        after = next_ref[...] if k == n_split - 1 else x_ref[p.stop:p.stop + hb, :]
        halo = jnp.concatenate([before, after], axis=0)
        hs.append(jnp.concatenate([_rmsnorm_bf16(xs[k], g), _rmsnorm_bf16(halo, g)], axis=0))
        keep.append((i > 0 if k == 0 else None, i < n_tiles - 1 if k == n_split - 1 else None))
    gates = [_dot(h[0:rows], win_ref[:, 0:D_MODEL]) for h in hs]
    cus = [_dot(h, win_ref[:, D_MODEL:2 * D_MODEL]) * _dot(h, win_ref[:, 2 * D_MODEL:3 * D_MODEL]) for h in hs]
    for k, (cu, (keep_lo, keep_hi)) in enumerate(zip(cus, keep)):
        lo, hi = cu[rows:rows + hb], cu[rows + hb:]
        buf_ref[k, 0:hb, :] = lo if keep_lo is None else jnp.where(keep_lo, lo, 0.0)
        buf_ref[k, hb:hb + rows, :] = cu[0:rows]
        buf_ref[k, hb + rows:2 * hb + rows, :] = hi if keep_hi is None else jnp.where(keep_hi, hi, 0.0)
    convs = [buf_ref[k, hb - 1:hb - 1 + rows, :] * cw_ref[0:1, :] + buf_ref[k, hb:hb + rows, :] * cw_ref[1:2, :]
             + buf_ref[k, hb + 1:hb + 1 + rows, :] * cw_ref[2:3, :] for k in range(n_split)]
    for p, x, gate, conv in zip(parts, xs, gates, convs):
        y_ref[p, :] = x + _dot((gate * conv).astype(BF16), wout_ref[...])
    _ffn_rows(y_ref, parts, fg_ref, w1_ref, w3_ref, w2_ref)


def _odd(x, g, win, cw, wout, ffn):
    batch, seq, _ = x.shape
    n_tiles = seq // TM
    per = TM // SUBLANES
    main = _rows(TM, D_MODEL)
    prev = pl.BlockSpec((None, SUBLANES, D_MODEL), lambda bi, i: (bi, jnp.maximum(i * per - 1, 0), 0))
    nxt = pl.BlockSpec((None, SUBLANES, D_MODEL), lambda bi, i: (bi, jnp.minimum((i + 1) * per, seq // SUBLANES - 1), 0))
    return pl.pallas_call(
        functools.partial(_odd_kernel, n_tiles=n_tiles),
        grid=(batch, n_tiles),
        in_specs=[prev, main, nxt] + [_resident(t) for t in (g, win, cw, wout, *ffn)],
        out_specs=main,
        out_shape=jax.ShapeDtypeStruct((batch, seq, D_MODEL), F32),
        scratch_shapes=[pltpu.VMEM((ROW_SPLIT, TM // ROW_SPLIT + 2 * SUBLANES, D_MODEL), F32)],
        compiler_params=_params(2),
        name="odd_ffn",
    )(x, x, x, *_arrays((g, win, cw, wout, *ffn)))


def _t5_bucket_np(rel):
    nb = N_BUCKETS // 2
    max_exact = nb // 2
    n = np.abs(rel)
    nf = np.maximum(n, 1).astype(np.float32)
    large = max_exact + (np.log(nf / max_exact) / math.log(MAX_DISTANCE / max_exact) * (nb - max_exact)).astype(np.int32)
    large = np.minimum(large, nb - 1)
    return np.where(rel > 0, nb, 0) + np.where(n < max_exact, n, large)


def _bias_windows(rel_bias):
    n_off = ATT_SUB + ATT_KEYS - 1
    off = np.arange(n_off) - (ATT_SUB - 1) - HALF
    band = np.abs(off) <= HALF
    out = []
    for g, d in enumerate(DILATIONS):
        tab = rel_bias[:, g * HEADS_PER_GROUP:(g + 1) * HEADS_PER_GROUP].astype(F32)
        vec = jnp.take(tab, jnp.asarray(_t5_bucket_np(off * d)), axis=0)
        vec = jnp.where(band[:, None], vec * LOG2E, NEG_INF).T
        flat = jnp.tile(jnp.pad(vec, ((0, 0), (0, 1))), (1, ATT_SUB))[:, :ATT_SUB * n_off]
        skew = flat.reshape(HEADS_PER_GROUP, ATT_SUB, n_off)
        win = skew[:, :, ATT_SUB - 1:ATT_SUB - 1 + ATT_KEYS]
        out.append(win.reshape(N_PAIRS, 2 * ATT_SUB, ATT_KEYS))
    return out


def _head_expand():
    src = np.arange(LANES)
    col = np.arange(N_GROUPS * GROUP_W)
    hit = (src[:, None] % PACK_W == (col // GROUP_W) * HEADS_PER_GROUP + (col % GROUP_W) // HEAD_DIM)[:, :]
    return jnp.asarray(hit & (src[:, None] < STAT_PARTS * PACK_W), BF16)


def _class_perm(d):
    n = PERM // d
    src = (np.arange(PERM) % n) * d + np.arange(PERM) // n
    return np.arange(PERM)[None, :] == src[:, None]


def _trunk(x, p):
    for layer in range(DEPTH):
        i = layer // 2
        ffn = tuple((p[k], layer) for k in ("ffn_norm", "ffn_w1", "ffn_w3", "ffn_w2"))
        if layer % 2 == 0:
            a, *qkvs = _even_in(x, (p["even_norm"], i), (p["even_w_in"], i), (p["q_gain"], i), (p["k_gain"], i))
            a = _conv_a(a, *((p[k], i) for k in ("conv_a_w", "conv_a_b", "conv_a_ln_g", "conv_a_ln_b")))
            os, ls = zip(*[_attention_group(qkv, bm) for qkv, bm in zip(qkvs, p["bias"])])
            x = _even_out(x, a, os, ls, p["perm_t"][1], p["perm_t"][2], p["expand"], (p["even_w_out"], i), ffn)
        else:
            x = _odd(x, (p["odd_norm"], i), (p["odd_w_in"], i), (p["conv_c_w"], i), (p["odd_w_out"], i), ffn)
    return x


def _prepare(rel_bias, even_norm, even_w_in, conv_a_w, conv_a_b, conv_a_ln_g, conv_a_ln_b, q_norm, k_norm,
             even_w_out, odd_norm, odd_w_in, conv_c_w, odd_w_out, ffn_norm, ffn_w1, ffn_w3, ffn_w2):
    row = lambda a: a.astype(F32)[:, None, :]
    tile_heads = lambda a: jnp.tile(a.astype(F32), (1, HEADS_PER_GROUP))[:, None, :]
    perms = [None if d == 1 else _class_perm(d) for d in DILATIONS]
    return dict(
        bias=_bias_windows(rel_bias), expand=_head_expand(),
        perm=[None if m is None else jnp.asarray(m, BF16) for m in perms],
        perm_t=[None if m is None else jnp.asarray(m.T, BF16) for m in perms],
        even_norm=row(even_norm), even_w_in=even_w_in.astype(BF16),
        conv_a_w=jnp.broadcast_to(conv_a_w.astype(F32)[:, :, None, :], conv_a_w.shape[:2] + (SUBLANES, CONV_A_CH)),
        conv_a_b=row(conv_a_b), conv_a_ln_g=row(conv_a_ln_g), conv_a_ln_b=row(conv_a_ln_b),
        q_gain=tile_heads(q_norm) * (HEAD_DIM ** -0.5 * LOG2E), k_gain=tile_heads(k_norm),
        even_w_out=even_w_out.astype(BF16),
        odd_norm=row(odd_norm), odd_w_in=odd_w_in.astype(BF16), conv_c_w=conv_c_w.astype(F32),
        odd_w_out=odd_w_out.astype(BF16),
        ffn_norm=row(ffn_norm), ffn_w1=ffn_w1.astype(BF16), ffn_w3=ffn_w3.astype(BF16), ffn_w2=ffn_w2.astype(BF16),
    )


def kernel(x_prompt, x_sample, rel_bias, even_norm, even_w_in, conv_a_w, conv_a_b, conv_a_ln_g, conv_a_ln_b, q_norm, k_norm, even_w_out, odd_norm, odd_w_in, conv_c_w, odd_w_out, ffn_norm, ffn_w1, ffn_w3, ffn_w2):
    p = _prepare(rel_bias, even_norm, even_w_in, conv_a_w, conv_a_b, conv_a_ln_g, conv_a_ln_b, q_norm, k_norm,
                 even_w_out, odd_norm, odd_w_in, conv_c_w, odd_w_out, ffn_norm, ffn_w1, ffn_w3, ffn_w2)
    return (_trunk(x_prompt, p), _trunk(x_sample, p))
```

```python
import functools
import math

import numpy as np
import jax
import jax.numpy as jnp
from jax import lax
from jax.experimental import pallas as pl
from jax.experimental.pallas import tpu as pltpu

F32 = jnp.float32
BF16 = jnp.bfloat16

D_MODEL = 1024
DEPTH = 4
HEAD_DIM = 64
WINDOWS = (128, 512, 2048)
DILATIONS = (1, 4, 16)
N_GROUPS = 3
HEADS_PER_GROUP = 8
GROUP_W = HEADS_PER_GROUP * HEAD_DIM
CONV_A_CH = D_MODEL // 2
CONV_A_K = 31
FFN_HIDDEN = 2816
N_BUCKETS = 32
MAX_DISTANCE = 1024
EPS = 1e-6
NEG_INF = -1e30
LOG2E = math.log2(math.e)
HALF = 64
assert all(w // (2 * d) == HALF for w, d in zip(WINDOWS, DILATIONS))

V7X_VMEM_BYTES = 64 * 1024 * 1024
VMEM_LIMIT_BYTES = V7X_VMEM_BYTES * 7 // 8
LANES = 128
SUBLANES = 8
MXU_DIM = 256

TM = 512
ROW_SPLIT = 2
PERM = MXU_DIM
WGT_PARTS = 3
ATT_SUB = 128
ATT_KEYS = ATT_SUB + 2 * HALF
ATT_STEP = 2048
assert ATT_KEYS == 2 * LANES
N_PAIRS = GROUP_W // LANES
CONV_A_TS = 512
CONV_A_RB = 32
CONV_A_HALO = 16
FFN_CHUNKS = ((0, 1536), (1536, 2816))
PACK_W = N_GROUPS * HEADS_PER_GROUP


def _params(n_axes):
    return pltpu.CompilerParams(dimension_semantics=("parallel",) * n_axes,
                                vmem_limit_bytes=VMEM_LIMIT_BYTES)


def _resident(a):
    if isinstance(a, tuple):
        a, layer = a
        nd = a.ndim
        return pl.BlockSpec((None,) + a.shape[1:], lambda *_: (layer,) + (0,) * (nd - 1), pipeline_mode=pl.Buffered(1))
    nd = a.ndim
    return pl.BlockSpec(a.shape, lambda *_: (0,) * nd, pipeline_mode=pl.Buffered(1))


def _arrays(params):
    return [a[0] if isinstance(a, tuple) else a for a in params]


def _rows(tm, width):
    return pl.BlockSpec((None, tm, width), lambda b, i: (b, i, 0))


def _classes(d, tm, width):
    if d == 1:
        return pl.BlockSpec((None, None, tm, width), lambda b, i: (b, 0, i, 0))
    return pl.BlockSpec((None, d, tm // d, width), lambda b, i: (b, 0, i, 0))


def _qkv_tile(d, tm):
    if d == 1:
        return pl.BlockSpec((None, None, 3, tm, GROUP_W), lambda b, i: (b, 0, 0, i, 0))
    return pl.BlockSpec((None, d, 3, tm // d, GROUP_W), lambda b, i: (b, 0, 0, i, 0))


def _rmsnorm_bf16(x, g):
    ms = jnp.mean(x * x, axis=-1, keepdims=True)
    return (x * lax.rsqrt(ms + EPS) * g).astype(BF16)


def _dot(a, b):
    return jnp.dot(a, b, preferred_element_type=F32)


def _even_in_kernel(x_ref, g_ref, w_ref, qg_ref, kg_ref, a_ref, q0_ref, q1_ref, q2_ref, slab_ref):
    parts = [slice(s * PERM, (s + 1) * PERM) for s in range(x_ref.shape[0] // PERM)]
    x = x_ref[...]
    hf = x * lax.rsqrt(jnp.mean(x * x, axis=-1, keepdims=True) + EPS) * g_ref[...]
    hs = [hf[p].astype(BF16) for p in parts]
    for k in range(D_MODEL // LANES):
        slab_ref[k] = hf[:, k * LANES:(k + 1) * LANES]

    def class_major(s, d):
        n = PERM // d
        slabs = [jnp.concatenate([slab_ref[k, pl.ds(s * PERM + r, n, stride=d), :] for r in range(d)], axis=0)
                 for k in range(D_MODEL // LANES)]
        return jnp.concatenate(slabs, axis=1).astype(BF16)

    def proj(hh, c):
        return _dot(hh, w_ref[:, c * GROUP_W:(c + 1) * GROUP_W])

    vals = [proj(h, 0) for h in hs]
    gates = [proj(h, 1) for h in hs]
    for p, val, gate in zip(parts, vals, gates):
        a_ref[p, :] = val * jax.nn.sigmoid(gate)
    first_head = lax.broadcasted_iota(jnp.int32, (1, LANES), 1) < HEAD_DIM
    for g, (d, out_ref) in enumerate(zip(DILATIONS, (q0_ref, q1_ref, q2_ref))):
        hgs = hs if d == 1 else [class_major(s, d) for s in range(len(parts))]
        for part in range(3):
            ps = [proj(hg, 2 + part * N_GROUPS + g) for hg in hgs]
            if part < 2:
                gain = qg_ref[...] if part == 0 else kg_ref[...]
                mss = []
                for p in ps:
                    cols = []
                    for c in range(0, GROUP_W, LANES):
                        sq = p[:, c:c + LANES] * p[:, c:c + LANES]
                        lo = jnp.sum(jnp.where(first_head, sq, 0.0), axis=-1, keepdims=True)
                        hi = jnp.sum(jnp.where(first_head, 0.0, sq), axis=-1, keepdims=True)
                        cols.append(jnp.where(first_head, lo, hi))
                    mss.append(jnp.concatenate(cols, axis=1))
                ps = [p * lax.rsqrt(ms * (1.0 / HEAD_DIM) + EPS) * gain for p, ms in zip(ps, mss)]
            for s, p in enumerate(ps):
                val = p.astype(BF16)
                if d == 1:
                    out_ref[part, s * PERM:(s + 1) * PERM, :] = val
                else:
                    n = PERM // d
                    for r in range(d):
                        out_ref[r, part, s * n:(s + 1) * n, :] = val[r * n:(r + 1) * n]


def _even_in(x, g, w, qg, kg):
    batch, seq, _ = x.shape
    outs = [jax.ShapeDtypeStruct((batch, seq, CONV_A_CH), F32)]
    outs += [jax.ShapeDtypeStruct((batch, d, 3, seq // d, GROUP_W), BF16) for d in DILATIONS]
    return pl.pallas_call(
        _even_in_kernel,
        grid=(batch, seq // TM),
        in_specs=[_rows(TM, D_MODEL)] + [_resident(a) for a in (g, w, qg, kg)],
        out_specs=[_rows(TM, CONV_A_CH)] + [_qkv_tile(d, TM) for d in DILATIONS],
        out_shape=outs,
        scratch_shapes=[pltpu.VMEM((D_MODEL // LANES, TM, LANES), F32)],
        compiler_params=_params(2),
        name="even_in",
    )(x, *_arrays((g, w, qg, kg)))


def _conv_a_kernel(prev_ref, a_ref, next_ref, wb_ref, b_ref, lg_ref, lb_ref, o_ref, buf_ref, sh_ref, *, n_tiles):
    i = pl.program_id(1)
    ts = a_ref.shape[0]
    h = CONV_A_HALO
    rows = ts + 2 * h
    buf_ref[0:h, :] = jnp.where(i > 0, prev_ref[...], 0.0)
    buf_ref[h:h + ts, :] = a_ref[...]
    buf_ref[h + ts:rows, :] = jnp.where(i < n_tiles - 1, next_ref[...], 0.0)
    n_sh = rows - SUBLANES
    for v in range(SUBLANES):
        sh_ref[v] = buf_ref[v:v + n_sh, :].reshape(n_sh // SUBLANES, SUBLANES, CONV_A_CH)
    off = h - (CONV_A_K - 1) // 2
    nb = CONV_A_RB // SUBLANES
    for r in range(0, ts, CONV_A_RB):
        t0 = r // SUBLANES
        acc = None
        for v in range(SUBLANES):
            us = [u for u in range((CONV_A_K + off) // SUBLANES + 1) if 0 <= SUBLANES * u + v - off < CONV_A_K]
            x = sh_ref[v, t0 + us[0]:t0 + us[-1] + nb]
            for u in us:
                term = x[u - us[0]:u - us[0] + nb] * wb_ref[SUBLANES * u + v - off]
                acc = term if acc is None else acc + term
        acc = acc.reshape(CONV_A_RB, CONV_A_CH) + b_ref[...]
        mu = jnp.mean(acc, axis=-1, keepdims=True)
        xc = acc - mu
        var = jnp.mean(xc * xc, axis=-1, keepdims=True)
        y = xc * lax.rsqrt(var + EPS) * lg_ref[...] + lb_ref[...]
        o_ref[r:r + CONV_A_RB, :] = (y * jax.nn.sigmoid(y)).astype(BF16)


def _conv_a(a, wb, b, lg, lb):
    batch, seq, _ = a.shape
    ts = CONV_A_TS
    n_tiles = seq // ts
    hb = ts // CONV_A_HALO
    rows = ts + 2 * CONV_A_HALO
    main = _rows(ts, CONV_A_CH)
    prev = pl.BlockSpec((None, CONV_A_HALO, CONV_A_CH), lambda bi, i: (bi, jnp.maximum(i * hb - 1, 0), 0))
    nxt = pl.BlockSpec((None, CONV_A_HALO, CONV_A_CH),
                       lambda bi, i: (bi, jnp.minimum((i + 1) * hb, seq // CONV_A_HALO - 1), 0))
    return pl.pallas_call(
        functools.partial(_conv_a_kernel, n_tiles=n_tiles),
        grid=(batch, n_tiles),
        in_specs=[prev, main, nxt] + [_resident(t) for t in (wb, b, lg, lb)],
        out_specs=main,
        out_shape=jax.ShapeDtypeStruct((batch, seq, CONV_A_CH), BF16),
        scratch_shapes=[pltpu.VMEM((rows, CONV_A_CH), F32),
                        pltpu.VMEM((SUBLANES, rows // SUBLANES - 1, SUBLANES, CONV_A_CH), F32)],
        compiler_params=_params(2),
        name="conv_a",
    )(a, a, a, *_arrays((wb, b, lg, lb)))


def _attn_kernel(*refs, n_tiles, halo):
    if halo:
        q_ref, kp_ref, kc_ref, kn_ref, vp_ref, vc_ref, vn_ref, bm_ref, o_ref, l_ref, kw_ref, vw_ref = refs
        edges = ((kw_ref, kp_ref, kn_ref), (vw_ref, vp_ref, vn_ref))
    else:
        q_ref, kc_ref, vc_ref, bm_ref, o_ref, l_ref, kw_ref, vw_ref = refs
        edges = ((kw_ref, None, None), (vw_ref, None, None))
    i = pl.program_id(2)
    n_cls, tq = q_ref.shape[0], q_ref.shape[1]
    lane = lax.broadcasted_iota(jnp.int32, (1, LANES), 1)
    first_head = lane < HEAD_DIM
    keep_first = first_head.astype(BF16)
    keep_second = 1.0 - keep_first
    n_sub = tq // ATT_SUB
    lanes = [slice(pr * LANES, (pr + 1) * LANES) for pr in range(N_PAIRS)]
    for c in range(n_cls):
        for (dst, before, after), cur in zip(edges, (kc_ref, vc_ref)):
            pad = jnp.zeros((HALF, GROUP_W), BF16)
            dst[c, 0:HALF, :] = pad if before is None else before[c]
            dst[c, HALF:HALF + tq, :] = cur[c]
            dst[c, HALF + tq:2 * HALF + tq, :] = pad if after is None else after[c]
        for s in range(n_sub):
            r0 = s * ATT_SUB
            edge_lo = edge_hi = None
            if s == 0:
                edge_lo = jnp.where(lane >= jnp.where(i == 0, HALF, 0), 0.0, NEG_INF)
            if s == n_sub - 1:
                edge_hi = jnp.where(lane < jnp.where(i == n_tiles - 1, LANES - HALF, LANES), 0.0, NEG_INF)
            scs = []
            for pr, cs in enumerate(lanes):
                q2 = q_ref[c, r0:r0 + ATT_SUB, cs]
                qs = jnp.concatenate([q2 * keep_first, q2 * keep_second], axis=0)
                sc = lax.dot_general(qs, kw_ref[c, r0:r0 + ATT_KEYS, cs], (((1,), (1,)), ((), ())),
                                     preferred_element_type=F32) + bm_ref[pr]
                if edge_lo is not None or edge_hi is not None:
                    left, right = sc[:, :LANES], sc[:, LANES:]
                    left = left if edge_lo is None else left + edge_lo
                    right = right if edge_hi is None else right + edge_hi
                    sc = jnp.concatenate([left, right], axis=1)
                scs.append(sc)
            ms = [jnp.max(sc, axis=-1, keepdims=True) for sc in scs]
            ps = [jnp.exp2(sc - m) for sc, m in zip(scs, ms)]
            dens = [jnp.sum(p, axis=-1, keepdims=True) for p in ps]
            pvs = [_dot(p.astype(BF16), vw_ref[c, r0:r0 + ATT_KEYS, cs]) for p, cs in zip(ps, lanes)]
            stat_rows = jnp.zeros((ATT_SUB, LANES), F32)
            for pr, (cs, pv, m, den) in enumerate(zip(lanes, pvs, ms, dens)):
                o_ref[c, r0:r0 + ATT_SUB, cs] = jnp.where(first_head, pv[0:ATT_SUB], pv[ATT_SUB:]).astype(BF16)
                for hd, rows in ((2 * pr, slice(0, ATT_SUB)), (2 * pr + 1, slice(ATT_SUB, 2 * ATT_SUB))):
                    both = jnp.where(lane < HEADS_PER_GROUP, m[rows], den[rows])
                    stat_rows = jnp.where((lane == hd) | (lane == HEADS_PER_GROUP + hd), both, stat_rows)
            l_ref[c, r0:r0 + ATT_SUB, :] = stat_rows


def _attention_group(qkv, bm):
    batch, d, _, cls_len, _ = qkv.shape
    tq = min(cls_len, ATT_STEP)
    n_tiles = cls_len // tq
    halo = n_tiles > 1
    n_cls = 1 if halo else min(d, ATT_STEP // tq)
    hb = tq // HALF

    def main(part):
        return pl.BlockSpec((None, n_cls, None, tq, GROUP_W), lambda b, r, i: (b, r, part, i, 0))

    def before(part):
        return pl.BlockSpec((None, 1, None, HALF, GROUP_W),
                            lambda b, r, i: (b, r, part, jnp.maximum(i * hb - 1, 0), 0))

    def after(part):
        return pl.BlockSpec((None, 1, None, HALF, GROUP_W),
                            lambda b, r, i: (b, r, part, jnp.minimum((i + 1) * hb, cls_len // HALF - 1), 0))

    def out(width):
        return pl.BlockSpec((None, n_cls, tq, width), lambda b, r, i: (b, r, i, 0))

    if halo:
        in_specs = [main(0), before(1), main(1), after(1), before(2), main(2), after(2), _resident(bm)]
    else:
        in_specs = [main(0), main(1), main(2), _resident(bm)]
    window = pltpu.VMEM((n_cls, tq + 2 * HALF, GROUP_W), BF16)
    return pl.pallas_call(
        functools.partial(_attn_kernel, n_tiles=n_tiles, halo=halo),
        grid=(batch, d // n_cls, n_tiles),
        in_specs=in_specs,
        out_specs=[out(GROUP_W), out(LANES)],
        out_shape=[jax.ShapeDtypeStruct((batch, d, cls_len, GROUP_W), BF16),
                   jax.ShapeDtypeStruct((batch, d, cls_len, LANES), F32)],
        scratch_shapes=[window, window],
        compiler_params=_params(3),
        name=f"attn_d{d}",
    )(*([qkv] * (len(in_specs) - 1)), bm)


def _token_order(ref, d, slab_ref, s):
    if d == 1:
        return ref[s * PERM:(s + 1) * PERM, :].astype(F32)
    n = PERM // d
    n_slabs = slab_ref.shape[0]
    for r in range(d):
        rows = ref[r, s * n:(s + 1) * n, :].astype(F32)
        for k in range(n_slabs):
            slab_ref[k, pl.ds(s * PERM + r, n, stride=d), :] = rows[:, k * LANES:(k + 1) * LANES]
    return jnp.concatenate([slab_ref[k, s * PERM:(s + 1) * PERM, :] for k in range(n_slabs)], axis=1)


def _ffn_rows(y_ref, parts, g_ref, w1_ref, w3_ref, w2_ref):
    xs = [y_ref[p, :] for p in parts]
    hs = [_rmsnorm_bf16(x, g_ref[...]) for x in xs]
    ys = xs
    for lo, hi in FFN_CHUNKS:
        acts = []
        for h in hs:
            a = _dot(h, w1_ref[:, lo:hi])
            acts.append((a * jax.nn.sigmoid(a) * _dot(h, w3_ref[:, lo:hi])).astype(BF16))
        ys = [y + _dot(act, w2_ref[lo:hi, :]) for y, act in zip(ys, acts)]
    for p, y in zip(parts, ys):
        y_ref[p, :] = y


def _even_out_kernel(x_ref, a_ref, o0_ref, o1_ref, o2_ref, l0_ref, l1_ref, l2_ref, ex_ref,
                     w_ref, fg_ref, w1_ref, w3_ref, w2_ref, y_ref, os1_ref, os2_ref, ls1_ref, ls2_ref):
    lane = lax.broadcasted_iota(jnp.int32, (1, LANES), 1)
    parts = [slice(s * PERM, (s + 1) * PERM) for s in range(x_ref.shape[0] // PERM)]
    blocks = range(len(parts))
    l_groups = tuple(zip((l0_ref, l1_ref, l2_ref), DILATIONS, (None, ls1_ref, ls2_ref)))
    o_groups = tuple(zip((o0_ref, o1_ref, o2_ref), DILATIONS, (None, os1_ref, os2_ref)))
    stats = [[_token_order(l_ref, d, slab, s) for l_ref, d, slab in l_groups] for s in blocks]
    ms = [jnp.maximum(jnp.maximum(st[0], st[1]), st[2]) for st in stats]
    es = [[jnp.exp2(t - m) for t in st] for st, m in zip(stats, ms)]
    dens = [[pltpu.roll(t, LANES - HEADS_PER_GROUP, 1) for t in st] for st in stats]
    invs = [1.0 / (e[0] * dn[0] + e[1] * dn[1] + e[2] * dn[2]) for e, dn in zip(es, dens)]
    wgts = []
    for e, inv in zip(es, invs):
        packed = None
        for g in range(N_GROUPS):
            al = jnp.where(lane < HEADS_PER_GROUP, e[g] * inv, 0.0)
            al = al if g == 0 else pltpu.roll(al, g * HEADS_PER_GROUP, 1)
            packed = al if packed is None else packed + al
        terms, rest = None, packed
        for t in range(WGT_PARTS):
            part = rest.astype(BF16).astype(F32)
            rest = rest - part
            part = part if t == 0 else pltpu.roll(part, t * PACK_W, 1)
            terms = part if terms is None else terms + part
        wgts.append(_dot(terms.astype(BF16), ex_ref[...]))
    outs = []
    for s, wgt in zip(blocks, wgts):
        o = None
        for g, (o_ref, d, slab) in enumerate(o_groups):
            term = wgt[:, g * GROUP_W:(g + 1) * GROUP_W] * _token_order(o_ref, d, slab, s)
            o = term if o is None else o + term
        outs.append(o.astype(BF16))
    for p, o in zip(parts, outs):
        y_ref[p, :] = x_ref[p, :] + _dot(a_ref[p, :], w_ref[0:CONV_A_CH, :]) + _dot(o, w_ref[CONV_A_CH:, :])
    _ffn_rows(y_ref, parts, fg_ref, w1_ref, w3_ref, w2_ref)


def _even_out(x, a, os, ls, ex, w, ffn):
    batch, seq, _ = x.shape
    return pl.pallas_call(
        _even_out_kernel,
        grid=(batch, seq // TM),
        in_specs=[_rows(TM, D_MODEL), _rows(TM, CONV_A_CH)]
                 + [_classes(d, TM, GROUP_W) for d in DILATIONS] + [_classes(d, TM, LANES) for d in DILATIONS]
                 + [_resident(t) for t in (ex, w, *ffn)],
        out_specs=_rows(TM, D_MODEL),
        out_shape=jax.ShapeDtypeStruct((batch, seq, D_MODEL), F32),
        scratch_shapes=[pltpu.VMEM((GROUP_W // LANES, TM, LANES), F32)] * 2 + [pltpu.VMEM((1, TM, LANES), F32)] * 2,
        compiler_params=_params(2),
        name="even_out_ffn",
    )(x, a, *os, *ls, *_arrays((ex, w, *ffn)))


def _odd_kernel(prev_ref, x_ref, next_ref, g_ref, win_ref, cw_ref, wout_ref, fg_ref, w1_ref, w3_ref, w2_ref,
                y_ref, buf_ref, *, n_tiles):
    i = pl.program_id(1)
    tm = x_ref.shape[0]
    hb = SUBLANES
    n_split = buf_ref.shape[0]
    rows = tm // n_split
    parts = [slice(k * rows, (k + 1) * rows) for k in range(n_split)]
    g = g_ref[...]
    xs = [x_ref[p, :] for p in parts]
    hs, keep = [], []
    for k, p in enumerate(parts):
        before = prev_ref[...] if k == 0 else x_ref[p.start - hb:p.start, :]
        after = next_ref[...] if k == n_split - 1 else x_ref[p.stop:p.stop + hb, :]
        halo = jnp.concatenate([before, after], axis=0)
        hs.append(jnp.concatenate([_rmsnorm_bf16(xs[k], g), _rmsnorm_bf16(halo, g)], axis=0))
        keep.append((i > 0 if k == 0 else None, i < n_tiles - 1 if k == n_split - 1 else None))
    gates = [_dot(h[0:rows], win_ref[:, 0:D_MODEL]) for h in hs]
    cus = [_dot(h, win_ref[:, D_MODEL:2 * D_MODEL]) * _dot(h, win_ref[:, 2 * D_MODEL:3 * D_MODEL]) for h in hs]
    for k, (cu, (keep_lo, keep_hi)) in enumerate(zip(cus, keep)):
        lo, hi = cu[rows:rows + hb], cu[rows + hb:]
        buf_ref[k, 0:hb, :] = lo if keep_lo is None else jnp.where(keep_lo, lo, 0.0)
        buf_ref[k, hb:hb + rows, :] = cu[0:rows]
        buf_ref[k, hb + rows:2 * hb + rows, :] = hi if keep_hi is None else jnp.where(keep_hi, hi, 0.0)
    convs = [buf_ref[k, hb - 1:hb - 1 + rows, :] * cw_ref[0:1, :] + buf_ref[k, hb:hb + rows, :] * cw_ref[1:2, :]
             + buf_ref[k, hb + 1:hb + 1 + rows, :] * cw_ref[2:3, :] for k in range(n_split)]
    for p, x, gate, conv in zip(parts, xs, gates, convs):
        y_ref[p, :] = x + _dot((gate * conv).astype(BF16), wout_ref[...])
    _ffn_rows(y_ref, parts, fg_ref, w1_ref, w3_ref, w2_ref)


def _odd(x, g, win, cw, wout, ffn):
    batch, seq, _ = x.shape
    n_tiles = seq // TM
    per = TM // SUBLANES
    main = _rows(TM, D_MODEL)
    prev = pl.BlockSpec((None, SUBLANES, D_MODEL), lambda bi, i: (bi, jnp.maximum(i * per - 1, 0), 0))
    nxt = pl.BlockSpec((None, SUBLANES, D_MODEL), lambda bi, i: (bi, jnp.minimum((i + 1) * per, seq // SUBLANES - 1), 0))
    return pl.pallas_call(
        functools.partial(_odd_kernel, n_tiles=n_tiles),
        grid=(batch, n_tiles),
        in_specs=[prev, main, nxt] + [_resident(t) for t in (g, win, cw, wout, *ffn)],
        out_specs=main,
        out_shape=jax.ShapeDtypeStruct((batch, seq, D_MODEL), F32),
        scratch_shapes=[pltpu.VMEM((ROW_SPLIT, TM // ROW_SPLIT + 2 * SUBLANES, D_MODEL), F32)],
        compiler_params=_params(2),
        name="odd_ffn",
    )(x, x, x, *_arrays((g, win, cw, wout, *ffn)))


def _t5_bucket_np(rel):
    nb = N_BUCKETS // 2
    max_exact = nb // 2
    n = np.abs(rel)
    nf = np.maximum(n, 1).astype(np.float32)
    large = max_exact + (np.log(nf / max_exact) / math.log(MAX_DISTANCE / max_exact) * (nb - max_exact)).astype(np.int32)
    large = np.minimum(large, nb - 1)
    return np.where(rel > 0, nb, 0) + np.where(n < max_exact, n, large)


def _bias_windows(rel_bias):
    n_off = ATT_SUB + ATT_KEYS - 1
    off = np.arange(n_off) - (ATT_SUB - 1) - HALF
    band = np.abs(off) <= HALF
    out = []
    for g, d in enumerate(DILATIONS):
        tab = rel_bias[:, g * HEADS_PER_GROUP:(g + 1) * HEADS_PER_GROUP].astype(F32)
        vec = jnp.take(tab, jnp.asarray(_t5_bucket_np(off * d)), axis=0)
        vec = jnp.where(band[:, None], vec * LOG2E, NEG_INF).T
        flat = jnp.tile(jnp.pad(vec, ((0, 0), (0, 1))), (1, ATT_SUB))[:, :ATT_SUB * n_off]
        skew = flat.reshape(HEADS_PER_GROUP, ATT_SUB, n_off)
        win = skew[:, :, ATT_SUB - 1:ATT_SUB - 1 + ATT_KEYS]
        out.append(win.reshape(N_PAIRS, 2 * ATT_SUB, ATT_KEYS))
    return out


def _head_expand():
    src = np.arange(LANES)
    col = np.arange(N_GROUPS * GROUP_W)
    hit = (src[:, None] % PACK_W == (col // GROUP_W) * HEADS_PER_GROUP + (col % GROUP_W) // HEAD_DIM)[:, :]
    return jnp.asarray(hit & (src[:, None] < WGT_PARTS * PACK_W), BF16)


def _trunk(x, p):
    for layer in range(DEPTH):
        i = layer // 2
        ffn = tuple((p[k], layer) for k in ("ffn_norm", "ffn_w1", "ffn_w3", "ffn_w2"))
        if layer % 2 == 0:
            a, *qkvs = _even_in(x, (p["even_norm"], i), (p["even_w_in"], i), (p["q_gain"], i), (p["k_gain"], i))
            a = _conv_a(a, *((p[k], i) for k in ("conv_a_w", "conv_a_b", "conv_a_ln_g", "conv_a_ln_b")))
            os, ls = zip(*[_attention_group(qkv, bm) for qkv, bm in zip(qkvs, p["bias"])])
            x = _even_out(x, a, os, ls, p["expand"], (p["even_w_out"], i), ffn)
        else:
            x = _odd(x, (p["odd_norm"], i), (p["odd_w_in"], i), (p["conv_c_w"], i), (p["odd_w_out"], i), ffn)
    return x


def _prepare(rel_bias, even_norm, even_w_in, conv_a_w, conv_a_b, conv_a_ln_g, conv_a_ln_b, q_norm, k_norm,
             even_w_out, odd_norm, odd_w_in, conv_c_w, odd_w_out, ffn_norm, ffn_w1, ffn_w3, ffn_w2):
    row = lambda a: a.astype(F32)[:, None, :]
    tile_heads = lambda a: jnp.tile(a.astype(F32), (1, HEADS_PER_GROUP))[:, None, :]
    return dict(
        bias=_bias_windows(rel_bias), expand=_head_expand(),
        even_norm=row(even_norm), even_w_in=even_w_in.astype(BF16),
        conv_a_w=jnp.broadcast_to(conv_a_w.astype(F32)[:, :, None, :], conv_a_w.shape[:2] + (SUBLANES, CONV_A_CH)),
        conv_a_b=row(conv_a_b), conv_a_ln_g=row(conv_a_ln_g), conv_a_ln_b=row(conv_a_ln_b),
        q_gain=tile_heads(q_norm) * (HEAD_DIM ** -0.5 * LOG2E), k_gain=tile_heads(k_norm),
        even_w_out=even_w_out.astype(BF16),
        odd_norm=row(odd_norm), odd_w_in=odd_w_in.astype(BF16), conv_c_w=conv_c_w.astype(F32),
        odd_w_out=odd_w_out.astype(BF16),
        ffn_norm=row(ffn_norm), ffn_w1=ffn_w1.astype(BF16), ffn_w3=ffn_w3.astype(BF16), ffn_w2=ffn_w2.astype(BF16),
    )


def kernel(x_prompt, x_sample, rel_bias, even_norm, even_w_in, conv_a_w, conv_a_b, conv_a_ln_g, conv_a_ln_b, q_norm, k_norm, even_w_out, odd_norm, odd_w_in, conv_c_w, odd_w_out, ffn_norm, ffn_w1, ffn_w3, ffn_w2):
    p = _prepare(rel_bias, even_norm, even_w_in, conv_a_w, conv_a_b, conv_a_ln_g, conv_a_ln_b, q_norm, k_norm,
                 even_w_out, odd_norm, odd_w_in, conv_c_w, odd_w_out, ffn_norm, ffn_w1, ffn_w3, ffn_w2)
    return (_trunk(x_prompt, p), _trunk(x_sample, p))
```

```python
import functools
import math

import numpy as np
import jax
import jax.numpy as jnp
from jax import lax
from jax.experimental import pallas as pl
from jax.experimental.pallas import tpu as pltpu

F32 = jnp.float32
BF16 = jnp.bfloat16

D_MODEL = 1024
DEPTH = 4
HEAD_DIM = 64
WINDOWS = (128, 512, 2048)
DILATIONS = (1, 4, 16)
N_GROUPS = 3
HEADS_PER_GROUP = 8
GROUP_W = HEADS_PER_GROUP * HEAD_DIM
CONV_A_CH = D_MODEL // 2
CONV_A_K = 31
FFN_HIDDEN = 2816
N_BUCKETS = 32
MAX_DISTANCE = 1024
EPS = 1e-6
NEG_INF = -1e30
LOG2E = math.log2(math.e)
HALF = 64
assert all(w // (2 * d) == HALF for w, d in zip(WINDOWS, DILATIONS))

V7X_VMEM_BYTES = 64 * 1024 * 1024
VMEM_LIMIT_BYTES = V7X_VMEM_BYTES * 7 // 8
LANES = 128
SUBLANES = 8
MXU_DIM = 256

TM = 512
ROW_SPLIT = 2
PERM = MXU_DIM
WGT_PARTS = 3
ATT_SUB = 128
ATT_KEYS = ATT_SUB + 2 * HALF
ATT_STEP = 2048
assert ATT_KEYS == 2 * LANES
N_PAIRS = GROUP_W // LANES
CONV_A_TS = 512
CONV_A_RB = 32
CONV_A_HALO = 16
FFN_CHUNKS = ((0, 1536), (1536, FFN_HIDDEN))
PACK_W = N_GROUPS * HEADS_PER_GROUP


def _params(n_axes):
    return pltpu.CompilerParams(dimension_semantics=("parallel",) * n_axes,
                                vmem_limit_bytes=VMEM_LIMIT_BYTES)


def _resident(a):
    if isinstance(a, tuple):
        a, layer = a
        nd = a.ndim
        return pl.BlockSpec((None,) + a.shape[1:], lambda *_: (layer,) + (0,) * (nd - 1), pipeline_mode=pl.Buffered(1))
    nd = a.ndim
    return pl.BlockSpec(a.shape, lambda *_: (0,) * nd, pipeline_mode=pl.Buffered(1))


def _arrays(params):
    return [a[0] if isinstance(a, tuple) else a for a in params]


def _rows(tm, width):
    return pl.BlockSpec((None, tm, width), lambda b, i: (b, i, 0))


def _classes(d, tm, width):
    if d == 1:
        return pl.BlockSpec((None, None, tm, width), lambda b, i: (b, 0, i, 0))
    return pl.BlockSpec((None, d, tm // d, width), lambda b, i: (b, 0, i, 0))


def _qkv_tile(d, tm):
    if d == 1:
        return pl.BlockSpec((None, None, 3, tm, GROUP_W), lambda b, i: (b, 0, 0, i, 0))
    return pl.BlockSpec((None, d, 3, tm // d, GROUP_W), lambda b, i: (b, 0, 0, i, 0))


def _rmsnorm_bf16(x, g):
    ms = jnp.mean(x * x, axis=-1, keepdims=True)
    return (x * lax.rsqrt(ms + EPS) * g).astype(BF16)


def _dot(a, b):
    return jnp.dot(a, b, preferred_element_type=F32)


def _even_in_kernel(x_ref, g_ref, w_ref, qg_ref, kg_ref, a_ref, q0_ref, q1_ref, q2_ref, slab_ref):
    parts = [slice(s * PERM, (s + 1) * PERM) for s in range(x_ref.shape[0] // PERM)]
    x = x_ref[...]
    hf = x * lax.rsqrt(jnp.mean(x * x, axis=-1, keepdims=True) + EPS) * g_ref[...]
    hs = [hf[p].astype(BF16) for p in parts]
    for k in range(D_MODEL // LANES):
        slab_ref[k] = hf[:, k * LANES:(k + 1) * LANES]

    def class_major(s, d):
        n = PERM // d
        slabs = [jnp.concatenate([slab_ref[k, pl.ds(s * PERM + r, n, stride=d), :] for r in range(d)], axis=0)
                 for k in range(D_MODEL // LANES)]
        return jnp.concatenate(slabs, axis=1).astype(BF16)

    def proj(hh, c):
        return _dot(hh, w_ref[:, c * GROUP_W:(c + 1) * GROUP_W])

    vals = [proj(h, 0) for h in hs]
    gates = [proj(h, 1) for h in hs]
    for p, val, gate in zip(parts, vals, gates):
        a_ref[p, :] = val * jax.nn.sigmoid(gate)
    first_head = lax.broadcasted_iota(jnp.int32, (1, LANES), 1) < HEAD_DIM
    for g, (d, out_ref) in enumerate(zip(DILATIONS, (q0_ref, q1_ref, q2_ref))):
        hgs = hs if d == 1 else [class_major(s, d) for s in range(len(parts))]
        for part in range(3):
            ps = [proj(hg, 2 + part * N_GROUPS + g) for hg in hgs]
            if part < 2:
                gain = qg_ref[...] if part == 0 else kg_ref[...]
                mss = []
                for p in ps:
                    cols = []
                    for c in range(0, GROUP_W, LANES):
                        sq = p[:, c:c + LANES] * p[:, c:c + LANES]
                        lo = jnp.sum(jnp.where(first_head, sq, 0.0), axis=-1, keepdims=True)
                        hi = jnp.sum(jnp.where(first_head, 0.0, sq), axis=-1, keepdims=True)
                        cols.append(jnp.where(first_head, lo, hi))
                    mss.append(jnp.concatenate(cols, axis=1))
                ps = [p * lax.rsqrt(ms * (1.0 / HEAD_DIM) + EPS) * gain for p, ms in zip(ps, mss)]
            for s, p in enumerate(ps):
                val = p.astype(BF16)
                if d == 1:
                    out_ref[part, s * PERM:(s + 1) * PERM, :] = val
                else:
                    n = PERM // d
                    for r in range(d):
                        out_ref[r, part, s * n:(s + 1) * n, :] = val[r * n:(r + 1) * n]


def _even_in(x, g, w, qg, kg):
    batch, seq, _ = x.shape
    outs = [jax.ShapeDtypeStruct((batch, seq, CONV_A_CH), F32)]
    outs += [jax.ShapeDtypeStruct((batch, d, 3, seq // d, GROUP_W), BF16) for d in DILATIONS]
    return pl.pallas_call(
        _even_in_kernel,
        grid=(batch, seq // TM),
        in_specs=[_rows(TM, D_MODEL)] + [_resident(a) for a in (g, w, qg, kg)],
        out_specs=[_rows(TM, CONV_A_CH)] + [_qkv_tile(d, TM) for d in DILATIONS],
        out_shape=outs,
        scratch_shapes=[pltpu.VMEM((D_MODEL // LANES, TM, LANES), F32)],
        compiler_params=_params(2),
        name="even_in",
    )(x, *_arrays((g, w, qg, kg)))


def _conv_a_kernel(prev_ref, a_ref, next_ref, wb_ref, b_ref, lg_ref, lb_ref, o_ref, buf_ref, sh_ref, *, n_tiles):
    i = pl.program_id(1)
    ts = a_ref.shape[0]
    h = CONV_A_HALO
    rows = ts + 2 * h
    buf_ref[0:h, :] = jnp.where(i > 0, prev_ref[...], 0.0)
    buf_ref[h:h + ts, :] = a_ref[...]
    buf_ref[h + ts:rows, :] = jnp.where(i < n_tiles - 1, next_ref[...], 0.0)
    n_sh = rows - SUBLANES
    for v in range(SUBLANES):
        sh_ref[v] = buf_ref[v:v + n_sh, :].reshape(n_sh // SUBLANES, SUBLANES, CONV_A_CH)
    off = h - (CONV_A_K - 1) // 2
    nb = CONV_A_RB // SUBLANES
    for r in range(0, ts, CONV_A_RB):
        t0 = r // SUBLANES
        acc = None
        for v in range(SUBLANES):
            us = [u for u in range((CONV_A_K + off) // SUBLANES + 1) if 0 <= SUBLANES * u + v - off < CONV_A_K]
            x = sh_ref[v, t0 + us[0]:t0 + us[-1] + nb]
            for u in us:
                term = x[u - us[0]:u - us[0] + nb] * wb_ref[SUBLANES * u + v - off]
                acc = term if acc is None else acc + term
        acc = acc.reshape(CONV_A_RB, CONV_A_CH) + b_ref[...]
        mu = jnp.mean(acc, axis=-1, keepdims=True)
        xc = acc - mu
        var = jnp.mean(xc * xc, axis=-1, keepdims=True)
        y = xc * lax.rsqrt(var + EPS) * lg_ref[...] + lb_ref[...]
        o_ref[r:r + CONV_A_RB, :] = (y * jax.nn.sigmoid(y)).astype(BF16)


def _conv_a(a, wb, b, lg, lb):
    batch, seq, _ = a.shape
    ts = CONV_A_TS
    n_tiles = seq // ts
    hb = ts // CONV_A_HALO
    rows = ts + 2 * CONV_A_HALO
    main = _rows(ts, CONV_A_CH)
    prev = pl.BlockSpec((None, CONV_A_HALO, CONV_A_CH), lambda bi, i: (bi, jnp.maximum(i * hb - 1, 0), 0))
    nxt = pl.BlockSpec((None, CONV_A_HALO, CONV_A_CH),
                       lambda bi, i: (bi, jnp.minimum((i + 1) * hb, seq // CONV_A_HALO - 1), 0))
    return pl.pallas_call(
        functools.partial(_conv_a_kernel, n_tiles=n_tiles),
        grid=(batch, n_tiles),
        in_specs=[prev, main, nxt] + [_resident(t) for t in (wb, b, lg, lb)],
        out_specs=main,
        out_shape=jax.ShapeDtypeStruct((batch, seq, CONV_A_CH), BF16),
        scratch_shapes=[pltpu.VMEM((rows, CONV_A_CH), F32),
                        pltpu.VMEM((SUBLANES, rows // SUBLANES - 1, SUBLANES, CONV_A_CH), F32)],
        compiler_params=_params(2),
        name="conv_a",
    )(a, a, a, *_arrays((wb, b, lg, lb)))


def _attn_kernel(*refs, n_tiles, halo):
    if halo:
        q_ref, kp_ref, kc_ref, kn_ref, vp_ref, vc_ref, vn_ref, bm_ref, o_ref, l_ref, kw_ref, vw_ref = refs
        edges = ((kw_ref, kp_ref, kn_ref), (vw_ref, vp_ref, vn_ref))
    else:
        q_ref, kc_ref, vc_ref, bm_ref, o_ref, l_ref, kw_ref, vw_ref = refs
        edges = ((kw_ref, None, None), (vw_ref, None, None))
    i = pl.program_id(2)
    n_cls, tq = q_ref.shape[0], q_ref.shape[1]
    lane = lax.broadcasted_iota(jnp.int32, (1, LANES), 1)
    first_head = lane < HEAD_DIM
    keep_first = first_head.astype(BF16)
    keep_second = 1.0 - keep_first
    n_sub = tq // ATT_SUB
    lanes = [slice(pr * LANES, (pr + 1) * LANES) for pr in range(N_PAIRS)]
    for c in range(n_cls):
        for (dst, before, after), cur in zip(edges, (kc_ref, vc_ref)):
            pad = jnp.zeros((HALF, GROUP_W), BF16)
            dst[c, 0:HALF, :] = pad if before is None else before[c]
            dst[c, HALF:HALF + tq, :] = cur[c]
            dst[c, HALF + tq:2 * HALF + tq, :] = pad if after is None else after[c]
        for s in range(n_sub):
            r0 = s * ATT_SUB
            edge_lo = edge_hi = None
            if s == 0:
                edge_lo = jnp.where(lane >= jnp.where(i == 0, HALF, 0), 0.0, NEG_INF)
            if s == n_sub - 1:
                edge_hi = jnp.where(lane < jnp.where(i == n_tiles - 1, LANES - HALF, LANES), 0.0, NEG_INF)
            scs = []
            for pr, cs in enumerate(lanes):
                q2 = q_ref[c, r0:r0 + ATT_SUB, cs]
                qs = jnp.concatenate([q2 * keep_first, q2 * keep_second], axis=0)
                sc = lax.dot_general(qs, kw_ref[c, r0:r0 + ATT_KEYS, cs], (((1,), (1,)), ((), ())),
                                     preferred_element_type=F32) + bm_ref[pr]
                if edge_lo is not None or edge_hi is not None:
                    left, right = sc[:, :LANES], sc[:, LANES:]
                    left = left if edge_lo is None else left + edge_lo
                    right = right if edge_hi is None else right + edge_hi
                    sc = jnp.concatenate([left, right], axis=1)
                scs.append(sc)
            ms = [jnp.max(sc, axis=-1, keepdims=True) for sc in scs]
            ps = [jnp.exp2(sc - m) for sc, m in zip(scs, ms)]
            dens = [jnp.sum(p, axis=-1, keepdims=True) for p in ps]
            pvs = [_dot(p.astype(BF16), vw_ref[c, r0:r0 + ATT_KEYS, cs]) for p, cs in zip(ps, lanes)]
            stat_rows = jnp.zeros((ATT_SUB, LANES), F32)
            for pr, (cs, pv, m, den) in enumerate(zip(lanes, pvs, ms, dens)):
                o_ref[c, r0:r0 + ATT_SUB, cs] = jnp.where(first_head, pv[0:ATT_SUB], pv[ATT_SUB:]).astype(BF16)
                for hd, rows in ((2 * pr, slice(0, ATT_SUB)), (2 * pr + 1, slice(ATT_SUB, 2 * ATT_SUB))):
                    both = jnp.where(lane < HEADS_PER_GROUP, m[rows], den[rows])
                    stat_rows = jnp.where((lane == hd) | (lane == HEADS_PER_GROUP + hd), both, stat_rows)
            l_ref[c, r0:r0 + ATT_SUB, :] = stat_rows


def _attention_group(qkv, bm):
    batch, d, _, cls_len, _ = qkv.shape
    tq = min(cls_len, ATT_STEP)
    n_tiles = cls_len // tq
    halo = n_tiles > 1
    n_cls = 1 if halo else min(d, ATT_STEP // tq)
    hb = tq // HALF

    def main(part):
        return pl.BlockSpec((None, n_cls, None, tq, GROUP_W), lambda b, r, i: (b, r, part, i, 0))

    def before(part):
        return pl.BlockSpec((None, 1, None, HALF, GROUP_W),
                            lambda b, r, i: (b, r, part, jnp.maximum(i * hb - 1, 0), 0))

    def after(part):
        return pl.BlockSpec((None, 1, None, HALF, GROUP_W),
                            lambda b, r, i: (b, r, part, jnp.minimum((i + 1) * hb, cls_len // HALF - 1), 0))

    def out(width):
        return pl.BlockSpec((None, n_cls, tq, width), lambda b, r, i: (b, r, i, 0))

    if halo:
        in_specs = [main(0), before(1), main(1), after(1), before(2), main(2), after(2), _resident(bm)]
    else:
        in_specs = [main(0), main(1), main(2), _resident(bm)]
    window = pltpu.VMEM((n_cls, tq + 2 * HALF, GROUP_W), BF16)
    return pl.pallas_call(
        functools.partial(_attn_kernel, n_tiles=n_tiles, halo=halo),
        grid=(batch, d // n_cls, n_tiles),
        in_specs=in_specs,
        out_specs=[out(GROUP_W), out(LANES)],
        out_shape=[jax.ShapeDtypeStruct((batch, d, cls_len, GROUP_W), BF16),
                   jax.ShapeDtypeStruct((batch, d, cls_len, LANES), F32)],
        scratch_shapes=[window, window],
        compiler_params=_params(3),
        name=f"attn_d{d}",
    )(*([qkv] * (len(in_specs) - 1)), bm)


def _token_order(ref, d, slab_ref, s):
    if d == 1:
        return ref[s * PERM:(s + 1) * PERM, :].astype(F32)
    n = PERM // d
    n_slabs = slab_ref.shape[0]
    for r in range(d):
        rows = ref[r, s * n:(s + 1) * n, :].astype(F32)
        for k in range(n_slabs):
            slab_ref[k, pl.ds(s * PERM + r, n, stride=d), :] = rows[:, k * LANES:(k + 1) * LANES]
    return jnp.concatenate([slab_ref[k, s * PERM:(s + 1) * PERM, :] for k in range(n_slabs)], axis=1)


def _ffn_rows(y_ref, parts, g_ref, w1_ref, w3_ref, w2_ref):
    xs = [y_ref[p, :] for p in parts]
    hs = [_rmsnorm_bf16(x, g_ref[...]) for x in xs]
    ys = xs
    for lo, hi in FFN_CHUNKS:
        acts = []
        for h in hs:
            a = _dot(h, w1_ref[:, lo:hi])
            acts.append((a * jax.nn.sigmoid(a) * _dot(h, w3_ref[:, lo:hi])).astype(BF16))
        ys = [y + _dot(act, w2_ref[lo:hi, :]) for y, act in zip(ys, acts)]
    for p, y in zip(parts, ys):
        y_ref[p, :] = y


def _even_out_kernel(x_ref, a_ref, o0_ref, o1_ref, o2_ref, l0_ref, l1_ref, l2_ref, ex_ref,
                     w_ref, fg_ref, w1_ref, w3_ref, w2_ref, y_ref, os1_ref, os2_ref, ls1_ref, ls2_ref):
    lane = lax.broadcasted_iota(jnp.int32, (1, LANES), 1)
    parts = [slice(s * PERM, (s + 1) * PERM) for s in range(x_ref.shape[0] // PERM)]
    blocks = range(len(parts))
    l_groups = tuple(zip((l0_ref, l1_ref, l2_ref), DILATIONS, (None, ls1_ref, ls2_ref)))
    o_groups = tuple(zip((o0_ref, o1_ref, o2_ref), DILATIONS, (None, os1_ref, os2_ref)))
    stats = [[_token_order(l_ref, d, slab, s) for l_ref, d, slab in l_groups] for s in blocks]
    ms = [jnp.maximum(jnp.maximum(st[0], st[1]), st[2]) for st in stats]
    es = [[jnp.exp2(t - m) for t in st] for st, m in zip(stats, ms)]
    dens = [[pltpu.roll(t, LANES - HEADS_PER_GROUP, 1) for t in st] for st in stats]
    invs = [1.0 / (e[0] * dn[0] + e[1] * dn[1] + e[2] * dn[2]) for e, dn in zip(es, dens)]
    wgts = []
    for e, inv in zip(es, invs):
        packed = None
        for g in range(N_GROUPS):
            al = jnp.where(lane < HEADS_PER_GROUP, e[g] * inv, 0.0)
            al = al if g == 0 else pltpu.roll(al, g * HEADS_PER_GROUP, 1)
            packed = al if packed is None else packed + al
        terms, rest = None, packed
        for t in range(WGT_PARTS):
            part = rest.astype(BF16).astype(F32)
            rest = rest - part
            part = part if t == 0 else pltpu.roll(part, t * PACK_W, 1)
            terms = part if terms is None else terms + part
        wgts.append(_dot(terms.astype(BF16), ex_ref[...]))
    outs = []
    for s, wgt in zip(blocks, wgts):
        o = None
        for g, (o_ref, d, slab) in enumerate(o_groups):
            term = wgt[:, g * GROUP_W:(g + 1) * GROUP_W] * _token_order(o_ref, d, slab, s)
            o = term if o is None else o + term
        outs.append(o.astype(BF16))
    for p, o in zip(parts, outs):
        y_ref[p, :] = x_ref[p, :] + _dot(a_ref[p, :], w_ref[0:CONV_A_CH, :]) + _dot(o, w_ref[CONV_A_CH:, :])
    _ffn_rows(y_ref, parts, fg_ref, w1_ref, w3_ref, w2_ref)


def _even_out(x, a, os, ls, ex, w, ffn):
    batch, seq, _ = x.shape
    return pl.pallas_call(
        _even_out_kernel,
        grid=(batch, seq // TM),
        in_specs=[_rows(TM, D_MODEL), _rows(TM, CONV_A_CH)]
                 + [_classes(d, TM, GROUP_W) for d in DILATIONS] + [_classes(d, TM, LANES) for d in DILATIONS]
                 + [_resident(t) for t in (ex, w, *ffn)],
        out_specs=_rows(TM, D_MODEL),
        out_shape=jax.ShapeDtypeStruct((batch, seq, D_MODEL), F32),
        scratch_shapes=[pltpu.VMEM((GROUP_W // LANES, TM, LANES), F32)] * 2 + [pltpu.VMEM((1, TM, LANES), F32)] * 2,
        compiler_params=_params(2),
        name="even_out_ffn",
    )(x, a, *os, *ls, *_arrays((ex, w, *ffn)))


def _odd_kernel(prev_ref, x_ref, next_ref, g_ref, win_ref, cw_ref, wout_ref, fg_ref, w1_ref, w3_ref, w2_ref,
                y_ref, buf_ref, *, n_tiles):
    i = pl.program_id(1)
    tm = x_ref.shape[0]
    hb = SUBLANES
    n_split = buf_ref.shape[0]
    rows = tm // n_split
    parts = [slice(k * rows, (k + 1) * rows) for k in range(n_split)]
    g = g_ref[...]
    xs = [x_ref[p, :] for p in parts]
    hs, keep = [], []
    for k, p in enumerate(parts):
        before = prev_ref[...] if k == 0 else x_ref[p.start - hb:p.start, :]
        after = next_ref[...] if k == n_split - 1 else x_ref[p.stop:p.stop + hb, :]
        halo = jnp.concatenate([before, after], axis=0)
        hs.append(jnp.concatenate([_rmsnorm_bf16(xs[k], g), _rmsnorm_bf16(halo, g)], axis=0))
        keep.append((i > 0 if k == 0 else None, i < n_tiles - 1 if k == n_split - 1 else None))
    gates = [_dot(h[0:rows], win_ref[:, 0:D_MODEL]) for h in hs]
    cus = [_dot(h, win_ref[:, D_MODEL:2 * D_MODEL]) * _dot(h, win_ref[:, 2 * D_MODEL:3 * D_MODEL]) for h in hs]
    for k, (cu, (keep_lo, keep_hi)) in enumerate(zip(cus, keep)):
        lo, hi = cu[rows:rows + hb], cu[rows + hb:]
        buf_ref[k, 0:hb, :] = lo if keep_lo is None else jnp.where(keep_lo, lo, 0.0)
        buf_ref[k, hb:hb + rows, :] = cu[0:rows]
        buf_ref[k, hb + rows:2 * hb + rows, :] = hi if keep_hi is None else jnp.where(keep_hi, hi, 0.0)
    convs = [buf_ref[k, hb - 1:hb - 1 + rows, :] * cw_ref[0:1, :] + buf_ref[k, hb:hb + rows, :] * cw_ref[1:2, :]
             + buf_ref[k, hb + 1:hb + 1 + rows, :] * cw_ref[2:3, :] for k in range(n_split)]
    for p, x, gate, conv in zip(parts, xs, gates, convs):
        y_ref[p, :] = x + _dot((gate * conv).astype(BF16), wout_ref[...])
    _ffn_rows(y_ref, parts, fg_ref, w1_ref, w3_ref, w2_ref)


def _odd(x, g, win, cw, wout, ffn):
    batch, seq, _ = x.shape
    n_tiles = seq // TM
    per = TM // SUBLANES
    main = _rows(TM, D_MODEL)
    prev = pl.BlockSpec((None, SUBLANES, D_MODEL), lambda bi, i: (bi, jnp.maximum(i * per - 1, 0), 0))
    nxt = pl.BlockSpec((None, SUBLANES, D_MODEL), lambda bi, i: (bi, jnp.minimum((i + 1) * per, seq // SUBLANES - 1), 0))
    return pl.pallas_call(
        functools.partial(_odd_kernel, n_tiles=n_tiles),
        grid=(batch, n_tiles),
        in_specs=[prev, main, nxt] + [_resident(t) for t in (g, win, cw, wout, *ffn)],
        out_specs=main,
        out_shape=jax.ShapeDtypeStruct((batch, seq, D_MODEL), F32),
        scratch_shapes=[pltpu.VMEM((ROW_SPLIT, TM // ROW_SPLIT + 2 * SUBLANES, D_MODEL), F32)],
        compiler_params=_params(2),
        name="odd_ffn",
    )(x, x, x, *_arrays((g, win, cw, wout, *ffn)))


def _t5_bucket_np(rel):
    nb = N_BUCKETS // 2
    max_exact = nb // 2
    n = np.abs(rel)
    nf = np.maximum(n, 1).astype(np.float32)
    large = max_exact + (np.log(nf / max_exact) / math.log(MAX_DISTANCE / max_exact) * (nb - max_exact)).astype(np.int32)
    large = np.minimum(large, nb - 1)
    return np.where(rel > 0, nb, 0) + np.where(n < max_exact, n, large)


def _bias_windows(rel_bias):
    n_off = ATT_SUB + ATT_KEYS - 1
    off = np.arange(n_off) - (ATT_SUB - 1) - HALF
    band = np.abs(off) <= HALF
    out = []
    for g, d in enumerate(DILATIONS):
        tab = rel_bias[:, g * HEADS_PER_GROUP:(g + 1) * HEADS_PER_GROUP].astype(F32)
        vec = jnp.take(tab, jnp.asarray(_t5_bucket_np(off * d)), axis=0)
        vec = jnp.where(band[:, None], vec * LOG2E, NEG_INF).T
        flat = jnp.tile(jnp.pad(vec, ((0, 0), (0, 1))), (1, ATT_SUB))[:, :ATT_SUB * n_off]
        skew = flat.reshape(HEADS_PER_GROUP, ATT_SUB, n_off)
        win = skew[:, :, ATT_SUB - 1:ATT_SUB - 1 + ATT_KEYS]
        out.append(win.reshape(N_PAIRS, 2 * ATT_SUB, ATT_KEYS))
    return out


def _head_expand():
    src = np.arange(LANES)
    col = np.arange(N_GROUPS * GROUP_W)
    hit = (src[:, None] % PACK_W == (col // GROUP_W) * HEADS_PER_GROUP + (col % GROUP_W) // HEAD_DIM)[:, :]
    return jnp.asarray(hit & (src[:, None] < WGT_PARTS * PACK_W), BF16)


def _trunk(x, p):
    for layer in range(DEPTH):
        i = layer // 2
        ffn = tuple((p[k], layer) for k in ("ffn_norm", "ffn_w1", "ffn_w3", "ffn_w2"))
        if layer % 2 == 0:
            a, *qkvs = _even_in(x, (p["even_norm"], i), (p["even_w_in"], i), (p["q_gain"], i), (p["k_gain"], i))
            a = _conv_a(a, *((p[k], i) for k in ("conv_a_w", "conv_a_b", "conv_a_ln_g", "conv_a_ln_b")))
            os, ls = zip(*[_attention_group(qkv, bm) for qkv, bm in zip(qkvs, p["bias"])])
            x = _even_out(x, a, os, ls, p["expand"], (p["even_w_out"], i), ffn)
        else:
            x = _odd(x, (p["odd_norm"], i), (p["odd_w_in"], i), (p["conv_c_w"], i), (p["odd_w_out"], i), ffn)
    return x


def _prepare(rel_bias, even_norm, even_w_in, conv_a_w, conv_a_b, conv_a_ln_g, conv_a_ln_b, q_norm, k_norm,
             even_w_out, odd_norm, odd_w_in, conv_c_w, odd_w_out, ffn_norm, ffn_w1, ffn_w3, ffn_w2):
    row = lambda a: a.astype(F32)[:, None, :]
    tile_heads = lambda a: jnp.tile(a.astype(F32), (1, HEADS_PER_GROUP))[:, None, :]
    return dict(
        bias=_bias_windows(rel_bias), expand=_head_expand(),
        even_norm=row(even_norm), even_w_in=even_w_in.astype(BF16),
        conv_a_w=jnp.broadcast_to(conv_a_w.astype(F32)[:, :, None, :], conv_a_w.shape[:2] + (SUBLANES, CONV_A_CH)),
        conv_a_b=row(conv_a_b), conv_a_ln_g=row(conv_a_ln_g), conv_a_ln_b=row(conv_a_ln_b),
        q_gain=tile_heads(q_norm) * (HEAD_DIM ** -0.5 * LOG2E), k_gain=tile_heads(k_norm),
        even_w_out=even_w_out.astype(BF16),
        odd_norm=row(odd_norm), odd_w_in=odd_w_in.astype(BF16), conv_c_w=conv_c_w.astype(F32),
        odd_w_out=odd_w_out.astype(BF16),
        ffn_norm=row(ffn_norm), ffn_w1=ffn_w1.astype(BF16), ffn_w3=ffn_w3.astype(BF16), ffn_w2=ffn_w2.astype(BF16),
    )


def kernel(x_prompt, x_sample, rel_bias, even_norm, even_w_in, conv_a_w, conv_a_b, conv_a_ln_g, conv_a_ln_b, q_norm, k_norm, even_w_out, odd_norm, odd_w_in, conv_c_w, odd_w_out, ffn_norm, ffn_w1, ffn_w3, ffn_w2):
    p = _prepare(rel_bias, even_norm, even_w_in, conv_a_w, conv_a_b, conv_a_ln_g, conv_a_ln_b, q_norm, k_norm,
                 even_w_out, odd_norm, odd_w_in, conv_c_w, odd_w_out, ffn_norm, ffn_w1, ffn_w3, ffn_w2)
    return (_trunk(x_prompt, p), _trunk(x_sample, p))
```

```python
import functools
import math

import numpy as np
import jax
import jax.numpy as jnp
from jax import lax
from jax.experimental import pallas as pl
from jax.experimental.pallas import tpu as pltpu

F32 = jnp.float32
BF16 = jnp.bfloat16

D_MODEL = 1024
DEPTH = 4
HEAD_DIM = 64
WINDOWS = (128, 512, 2048)
DILATIONS = (1, 4, 16)
N_GROUPS = 3
HEADS_PER_GROUP = 8
GROUP_W = HEADS_PER_GROUP * HEAD_DIM
CONV_A_CH = D_MODEL // 2
CONV_A_K = 31
FFN_HIDDEN = 2816
N_BUCKETS = 32
MAX_DISTANCE = 1024
EPS = 1e-6
NEG_INF = -1e30
LOG2E = math.log2(math.e)
HALF = 64
assert all(w // (2 * d) == HALF for w, d in zip(WINDOWS, DILATIONS))

V7X_VMEM_BYTES = 64 * 1024 * 1024
VMEM_LIMIT_BYTES = V7X_VMEM_BYTES * 7 // 8
LANES = 128
SUBLANES = 8
MXU_DIM = 256

TM = 512
ROW_SPLIT = 2
PERM = MXU_DIM
WGT_PARTS = 3
ATT_SUB = 128
ATT_KEYS = ATT_SUB + 2 * HALF
ATT_STEP = 2048
assert ATT_KEYS == 2 * LANES
N_PAIRS = GROUP_W // LANES
CONV_A_TS = 512
CONV_A_RB = 32
CONV_A_HALO = 16
FFN_CHUNKS = ((0, 1536), (1536, FFN_HIDDEN))
PACK_W = N_GROUPS * HEADS_PER_GROUP


def _params(n_axes):
    return pltpu.CompilerParams(dimension_semantics=("parallel",) * n_axes,
                                vmem_limit_bytes=VMEM_LIMIT_BYTES)


def _resident(a):
    if isinstance(a, tuple):
        a, layer = a
        nd = a.ndim
        return pl.BlockSpec((None,) + a.shape[1:], lambda *_: (layer,) + (0,) * (nd - 1), pipeline_mode=pl.Buffered(1))
    nd = a.ndim
    return pl.BlockSpec(a.shape, lambda *_: (0,) * nd, pipeline_mode=pl.Buffered(1))


def _arrays(params):
    return [a[0] if isinstance(a, tuple) else a for a in params]


def _rows(tm, width):
    return pl.BlockSpec((None, tm, width), lambda b, i: (b, i, 0))


def _classes(d, tm, width):
    if d == 1:
        return pl.BlockSpec((None, None, tm, width), lambda b, i: (b, 0, i, 0))
    return pl.BlockSpec((None, d, tm // d, width), lambda b, i: (b, 0, i, 0))


def _qkv_tile(d, tm):
    if d == 1:
        return pl.BlockSpec((None, None, 3, tm, GROUP_W), lambda b, i: (b, 0, 0, i, 0))
    return pl.BlockSpec((None, d, 3, tm // d, GROUP_W), lambda b, i: (b, 0, 0, i, 0))


def _rmsnorm_bf16(x, g):
    ms = jnp.mean(x * x, axis=-1, keepdims=True)
    return (x * lax.rsqrt(ms + EPS) * g).astype(BF16)


def _dot(a, b):
    return jnp.dot(a, b, preferred_element_type=F32)


def _even_in_kernel(x_ref, g_ref, w_ref, qg_ref, kg_ref, a_ref, q0_ref, q1_ref, q2_ref, slab_ref):
    parts = [slice(s * PERM, (s + 1) * PERM) for s in range(x_ref.shape[0] // PERM)]
    x = x_ref[...]
    hf = x * lax.rsqrt(jnp.mean(x * x, axis=-1, keepdims=True) + EPS) * g_ref[...]
    hs = [hf[p].astype(BF16) for p in parts]
    for k in range(D_MODEL // LANES):
        slab_ref[k] = hf[:, k * LANES:(k + 1) * LANES]

    def class_major(s, d):
        n = PERM // d
        slabs = [jnp.concatenate([slab_ref[k, pl.ds(s * PERM + r, n, stride=d), :] for r in range(d)], axis=0)
                 for k in range(D_MODEL // LANES)]
        return jnp.concatenate(slabs, axis=1).astype(BF16)

    def proj(hh, c):
        return _dot(hh, w_ref[:, c * GROUP_W:(c + 1) * GROUP_W])

    vals = [proj(h, 0) for h in hs]
    gates = [proj(h, 1) for h in hs]
    for p, val, gate in zip(parts, vals, gates):
        a_ref[p, :] = val * jax.nn.sigmoid(gate)
    first_head = lax.broadcasted_iota(jnp.int32, (1, LANES), 1) < HEAD_DIM
    for g, (d, out_ref) in enumerate(zip(DILATIONS, (q0_ref, q1_ref, q2_ref))):
        hgs = hs if d == 1 else [class_major(s, d) for s in range(len(parts))]
        for part in range(3):
            ps = [proj(hg, 2 + part * N_GROUPS + g) for hg in hgs]
            if part < 2:
                gain = qg_ref[...] if part == 0 else kg_ref[...]
                mss = []
                for p in ps:
                    cols = []
                    for c in range(0, GROUP_W, LANES):
                        sq = p[:, c:c + LANES] * p[:, c:c + LANES]
                        lo = jnp.sum(jnp.where(first_head, sq, 0.0), axis=-1, keepdims=True)
                        hi = jnp.sum(jnp.where(first_head, 0.0, sq), axis=-1, keepdims=True)
                        cols.append(jnp.where(first_head, lo, hi))
                    mss.append(jnp.concatenate(cols, axis=1))
                ps = [p * lax.rsqrt(ms * (1.0 / HEAD_DIM) + EPS) * gain for p, ms in zip(ps, mss)]
            for s, p in enumerate(ps):
                val = p.astype(BF16)
                if d == 1:
                    out_ref[part, s * PERM:(s + 1) * PERM, :] = val
                else:
                    n = PERM // d
                    for r in range(d):
                        out_ref[r, part, s * n:(s + 1) * n, :] = val[r * n:(r + 1) * n]


def _even_in(x, g, w, qg, kg):
    batch, seq, _ = x.shape
    outs = [jax.ShapeDtypeStruct((batch, seq, CONV_A_CH), F32)]
    outs += [jax.ShapeDtypeStruct((batch, d, 3, seq // d, GROUP_W), BF16) for d in DILATIONS]
    return pl.pallas_call(
        _even_in_kernel,
        grid=(batch, seq // TM),
        in_specs=[_rows(TM, D_MODEL)] + [_resident(a) for a in (g, w, qg, kg)],
        out_specs=[_rows(TM, CONV_A_CH)] + [_qkv_tile(d, TM) for d in DILATIONS],
        out_shape=outs,
        scratch_shapes=[pltpu.VMEM((D_MODEL // LANES, TM, LANES), F32)],
        compiler_params=_params(2),
        name="even_in",
    )(x, *_arrays((g, w, qg, kg)))


def _conv_a_kernel(prev_ref, a_ref, next_ref, wb_ref, b_ref, lg_ref, lb_ref, o_ref, buf_ref, sh_ref, *, n_tiles):
    i = pl.program_id(1)
    ts = a_ref.shape[0]
    h = CONV_A_HALO
    rows = ts + 2 * h
    buf_ref[0:h, :] = jnp.where(i > 0, prev_ref[...], 0.0)
    buf_ref[h:h + ts, :] = a_ref[...]
    buf_ref[h + ts:rows, :] = jnp.where(i < n_tiles - 1, next_ref[...], 0.0)
    n_sh = rows - SUBLANES
    for v in range(SUBLANES):
        sh_ref[v] = buf_ref[v:v + n_sh, :].reshape(n_sh // SUBLANES, SUBLANES, CONV_A_CH)
    off = h - (CONV_A_K - 1) // 2
    nb = CONV_A_RB // SUBLANES
    for r in range(0, ts, CONV_A_RB):
        t0 = r // SUBLANES
        acc = None
        for v in range(SUBLANES):
            us = [u for u in range((CONV_A_K + off) // SUBLANES + 1) if 0 <= SUBLANES * u + v - off < CONV_A_K]
            x = sh_ref[v, t0 + us[0]:t0 + us[-1] + nb]
            for u in us:
                term = x[u - us[0]:u - us[0] + nb] * wb_ref[SUBLANES * u + v - off]
                acc = term if acc is None else acc + term
        acc = acc.reshape(CONV_A_RB, CONV_A_CH) + b_ref[...]
        mu = jnp.mean(acc, axis=-1, keepdims=True)
        xc = acc - mu
        var = jnp.mean(xc * xc, axis=-1, keepdims=True)
        y = xc * lax.rsqrt(var + EPS) * lg_ref[...] + lb_ref[...]
        o_ref[r:r + CONV_A_RB, :] = (y * jax.nn.sigmoid(y)).astype(BF16)


def _conv_a(a, wb, b, lg, lb):
    batch, seq, _ = a.shape
    ts = CONV_A_TS
    n_tiles = seq // ts
    hb = ts // CONV_A_HALO
    rows = ts + 2 * CONV_A_HALO
    main = _rows(ts, CONV_A_CH)
    prev = pl.BlockSpec((None, CONV_A_HALO, CONV_A_CH), lambda bi, i: (bi, jnp.maximum(i * hb - 1, 0), 0))
    nxt = pl.BlockSpec((None, CONV_A_HALO, CONV_A_CH),
                       lambda bi, i: (bi, jnp.minimum((i + 1) * hb, seq // CONV_A_HALO - 1), 0))
    return pl.pallas_call(
        functools.partial(_conv_a_kernel, n_tiles=n_tiles),
        grid=(batch, n_tiles),
        in_specs=[prev, main, nxt] + [_resident(t) for t in (wb, b, lg, lb)],
        out_specs=main,
        out_shape=jax.ShapeDtypeStruct((batch, seq, CONV_A_CH), BF16),
        scratch_shapes=[pltpu.VMEM((rows, CONV_A_CH), F32),
                        pltpu.VMEM((SUBLANES, rows // SUBLANES - 1, SUBLANES, CONV_A_CH), F32)],
        compiler_params=_params(2),
        name="conv_a",
    )(a, a, a, *_arrays((wb, b, lg, lb)))


def _attn_kernel(*refs, n_tiles, halo):
    if halo:
        q_ref, kp_ref, kc_ref, kn_ref, vp_ref, vc_ref, vn_ref, bm_ref, o_ref, l_ref = refs
    else:
        q_ref, kc_ref, vc_ref, bm_ref, o_ref, l_ref = refs
        kp_ref = kn_ref = vp_ref = vn_ref = None
    i = pl.program_id(2)
    n_cls, tq = q_ref.shape[0], q_ref.shape[1]
    lane = lax.broadcasted_iota(jnp.int32, (1, LANES), 1)
    first_head = lane < HEAD_DIM
    keep_first = first_head.astype(BF16)
    keep_second = 1.0 - keep_first
    n_sub = tq // ATT_SUB
    lanes = [slice(pr * LANES, (pr + 1) * LANES) for pr in range(N_PAIRS)]

    def window(cur_ref, before_ref, after_ref, c, r0, cs):
        lo, hi = r0 - HALF, r0 + ATT_SUB + HALF
        pieces = []
        if lo < 0:
            pieces.append(jnp.zeros((HALF, LANES), BF16) if before_ref is None else before_ref[c, :, cs])
        pieces.append(cur_ref[c, max(lo, 0):min(hi, tq), cs])
        if hi > tq:
            pieces.append(jnp.zeros((HALF, LANES), BF16) if after_ref is None else after_ref[c, :, cs])
        return pieces[0] if len(pieces) == 1 else jnp.concatenate(pieces, axis=0)

    for c in range(n_cls):
        for s in range(n_sub):
            r0 = s * ATT_SUB
            edge_lo = edge_hi = None
            if s == 0:
                edge_lo = jnp.where(lane >= jnp.where(i == 0, HALF, 0), 0.0, NEG_INF)
            if s == n_sub - 1:
                edge_hi = jnp.where(lane < jnp.where(i == n_tiles - 1, LANES - HALF, LANES), 0.0, NEG_INF)
            scs = []
            for pr, cs in enumerate(lanes):
                q2 = q_ref[c, r0:r0 + ATT_SUB, cs]
                qs = jnp.concatenate([q2 * keep_first, q2 * keep_second], axis=0)
                sc = lax.dot_general(qs, window(kc_ref, kp_ref, kn_ref, c, r0, cs), (((1,), (1,)), ((), ())),
                                     preferred_element_type=F32) + bm_ref[pr]
                if edge_lo is not None or edge_hi is not None:
                    left, right = sc[:, :LANES], sc[:, LANES:]
                    left = left if edge_lo is None else left + edge_lo
                    right = right if edge_hi is None else right + edge_hi
                    sc = jnp.concatenate([left, right], axis=1)
                scs.append(sc)
            ms = [jnp.max(sc, axis=-1, keepdims=True) for sc in scs]
            ps = [jnp.exp2(sc - m) for sc, m in zip(scs, ms)]
            dens = [jnp.sum(p, axis=-1, keepdims=True) for p in ps]
            pvs = [_dot(p.astype(BF16), window(vc_ref, vp_ref, vn_ref, c, r0, cs)) for p, cs in zip(ps, lanes)]
            stat_rows = jnp.zeros((ATT_SUB, LANES), F32)
            for pr, (cs, pv, m, den) in enumerate(zip(lanes, pvs, ms, dens)):
                o_ref[c, r0:r0 + ATT_SUB, cs] = jnp.where(first_head, pv[0:ATT_SUB], pv[ATT_SUB:]).astype(BF16)
                for hd, rows in ((2 * pr, slice(0, ATT_SUB)), (2 * pr + 1, slice(ATT_SUB, 2 * ATT_SUB))):
                    both = jnp.where(lane < HEADS_PER_GROUP, m[rows], den[rows])
                    stat_rows = jnp.where((lane == hd) | (lane == HEADS_PER_GROUP + hd), both, stat_rows)
            l_ref[c, r0:r0 + ATT_SUB, :] = stat_rows


def _attention_group(qkv, bm):
    batch, d, _, cls_len, _ = qkv.shape
    tq = min(cls_len, ATT_STEP)
    n_tiles = cls_len // tq
    halo = n_tiles > 1
    n_cls = 1 if halo else min(d, ATT_STEP // tq)
    hb = tq // HALF

    def main(part):
        return pl.BlockSpec((None, n_cls, None, tq, GROUP_W), lambda b, r, i: (b, r, part, i, 0))

    def before(part):
        return pl.BlockSpec((None, 1, None, HALF, GROUP_W),
                            lambda b, r, i: (b, r, part, jnp.maximum(i * hb - 1, 0), 0))

    def after(part):
        return pl.BlockSpec((None, 1, None, HALF, GROUP_W),
                            lambda b, r, i: (b, r, part, jnp.minimum((i + 1) * hb, cls_len // HALF - 1), 0))

    def out(width):
        return pl.BlockSpec((None, n_cls, tq, width), lambda b, r, i: (b, r, i, 0))

    if halo:
        in_specs = [main(0), before(1), main(1), after(1), before(2), main(2), after(2), _resident(bm)]
    else:
        in_specs = [main(0), main(1), main(2), _resident(bm)]
    return pl.pallas_call(
        functools.partial(_attn_kernel, n_tiles=n_tiles, halo=halo),
        grid=(batch, d // n_cls, n_tiles),
        in_specs=in_specs,
        out_specs=[out(GROUP_W), out(LANES)],
        out_shape=[jax.ShapeDtypeStruct((batch, d, cls_len, GROUP_W), BF16),
                   jax.ShapeDtypeStruct((batch, d, cls_len, LANES), F32)],
        compiler_params=_params(3),
        name=f"attn_d{d}",
    )(*([qkv] * (len(in_specs) - 1)), bm)


def _token_order(ref, d, slab_ref, s):
    if d == 1:
        return ref[s * PERM:(s + 1) * PERM, :].astype(F32)
    n = PERM // d
    n_slabs = slab_ref.shape[0]
    for r in range(d):
        rows = ref[r, s * n:(s + 1) * n, :].astype(F32)
        for k in range(n_slabs):
            slab_ref[k, pl.ds(s * PERM + r, n, stride=d), :] = rows[:, k * LANES:(k + 1) * LANES]
    return jnp.concatenate([slab_ref[k, s * PERM:(s + 1) * PERM, :] for k in range(n_slabs)], axis=1)


def _ffn_rows(y_ref, parts, g_ref, w1_ref, w3_ref, w2_ref):
    xs = [y_ref[p, :] for p in parts]
    hs = [_rmsnorm_bf16(x, g_ref[...]) for x in xs]
    ys = xs
    for lo, hi in FFN_CHUNKS:
        acts = []
        for h in hs:
            a = _dot(h, w1_ref[:, lo:hi])
            acts.append((a * jax.nn.sigmoid(a) * _dot(h, w3_ref[:, lo:hi])).astype(BF16))
        ys = [y + _dot(act, w2_ref[lo:hi, :]) for y, act in zip(ys, acts)]
    for p, y in zip(parts, ys):
        y_ref[p, :] = y


def _even_out_kernel(x_ref, a_ref, o0_ref, o1_ref, o2_ref, l0_ref, l1_ref, l2_ref, ex_ref,
                     w_ref, fg_ref, w1_ref, w3_ref, w2_ref, y_ref, os1_ref, os2_ref, ls1_ref, ls2_ref):
    lane = lax.broadcasted_iota(jnp.int32, (1, LANES), 1)
    parts = [slice(s * PERM, (s + 1) * PERM) for s in range(x_ref.shape[0] // PERM)]
    blocks = range(len(parts))
    l_groups = tuple(zip((l0_ref, l1_ref, l2_ref), DILATIONS, (None, ls1_ref, ls2_ref)))
    o_groups = tuple(zip((o0_ref, o1_ref, o2_ref), DILATIONS, (None, os1_ref, os2_ref)))
    stats = [[_token_order(l_ref, d, slab, s) for l_ref, d, slab in l_groups] for s in blocks]
    ms = [jnp.maximum(jnp.maximum(st[0], st[1]), st[2]) for st in stats]
    es = [[jnp.exp2(t - m) for t in st] for st, m in zip(stats, ms)]
    dens = [[pltpu.roll(t, LANES - HEADS_PER_GROUP, 1) for t in st] for st in stats]
    invs = [1.0 / (e[0] * dn[0] + e[1] * dn[1] + e[2] * dn[2]) for e, dn in zip(es, dens)]
    wgts = []
    for e, inv in zip(es, invs):
        packed = None
        for g in range(N_GROUPS):
            al = jnp.where(lane < HEADS_PER_GROUP, e[g] * inv, 0.0)
            al = al if g == 0 else pltpu.roll(al, g * HEADS_PER_GROUP, 1)
            packed = al if packed is None else packed + al
        terms, rest = None, packed
        for t in range(WGT_PARTS):
            part = rest.astype(BF16).astype(F32)
            rest = rest - part
            part = part if t == 0 else pltpu.roll(part, t * PACK_W, 1)
            terms = part if terms is None else terms + part
        wgts.append(_dot(terms.astype(BF16), ex_ref[...]))
    outs = []
    for s, wgt in zip(blocks, wgts):
        o = None
        for g, (o_ref, d, slab) in enumerate(o_groups):
            term = wgt[:, g * GROUP_W:(g + 1) * GROUP_W] * _token_order(o_ref, d, slab, s)
            o = term if o is None else o + term
        outs.append(o.astype(BF16))
    for p, o in zip(parts, outs):
        y_ref[p, :] = x_ref[p, :] + _dot(a_ref[p, :], w_ref[0:CONV_A_CH, :]) + _dot(o, w_ref[CONV_A_CH:, :])
    _ffn_rows(y_ref, parts, fg_ref, w1_ref, w3_ref, w2_ref)


def _even_out(x, a, os, ls, ex, w, ffn):
    batch, seq, _ = x.shape
    return pl.pallas_call(
        _even_out_kernel,
        grid=(batch, seq // TM),
        in_specs=[_rows(TM, D_MODEL), _rows(TM, CONV_A_CH)]
                 + [_classes(d, TM, GROUP_W) for d in DILATIONS] + [_classes(d, TM, LANES) for d in DILATIONS]
                 + [_resident(t) for t in (ex, w, *ffn)],
        out_specs=_rows(TM, D_MODEL),
        out_shape=jax.ShapeDtypeStruct((batch, seq, D_MODEL), F32),
        scratch_shapes=[pltpu.VMEM((GROUP_W // LANES, TM, LANES), F32)] * 2 + [pltpu.VMEM((1, TM, LANES), F32)] * 2,
        compiler_params=_params(2),
        name="even_out_ffn",
    )(x, a, *os, *ls, *_arrays((ex, w, *ffn)))


def _odd_kernel(prev_ref, x_ref, next_ref, g_ref, win_ref, cw_ref, wout_ref, fg_ref, w1_ref, w3_ref, w2_ref,
                y_ref, buf_ref, *, n_tiles):
    i = pl.program_id(1)
    tm = x_ref.shape[0]
    hb = SUBLANES
    n_split = buf_ref.shape[0]
    rows = tm // n_split
    parts = [slice(k * rows, (k + 1) * rows) for k in range(n_split)]
    g = g_ref[...]
    xs = [x_ref[p, :] for p in parts]
    hs, keep = [], []
    for k, p in enumerate(parts):
        before = prev_ref[...] if k == 0 else x_ref[p.start - hb:p.start, :]
        after = next_ref[...] if k == n_split - 1 else x_ref[p.stop:p.stop + hb, :]
        halo = jnp.concatenate([before, after], axis=0)
        hs.append(jnp.concatenate([_rmsnorm_bf16(xs[k], g), _rmsnorm_bf16(halo, g)], axis=0))
        keep.append((i > 0 if k == 0 else None, i < n_tiles - 1 if k == n_split - 1 else None))
    gates = [_dot(h[0:rows], win_ref[:, 0:D_MODEL]) for h in hs]
    cus = [_dot(h, win_ref[:, D_MODEL:2 * D_MODEL]) * _dot(h, win_ref[:, 2 * D_MODEL:3 * D_MODEL]) for h in hs]
    for k, (cu, (keep_lo, keep_hi)) in enumerate(zip(cus, keep)):
        lo, hi = cu[rows:rows + hb], cu[rows + hb:]
        buf_ref[k, 0:hb, :] = lo if keep_lo is None else jnp.where(keep_lo, lo, 0.0)
        buf_ref[k, hb:hb + rows, :] = cu[0:rows]
        buf_ref[k, hb + rows:2 * hb + rows, :] = hi if keep_hi is None else jnp.where(keep_hi, hi, 0.0)
    convs = [buf_ref[k, hb - 1:hb - 1 + rows, :] * cw_ref[0:1, :] + buf_ref[k, hb:hb + rows, :] * cw_ref[1:2, :]
             + buf_ref[k, hb + 1:hb + 1 + rows, :] * cw_ref[2:3, :] for k in range(n_split)]
    for p, x, gate, conv in zip(parts, xs, gates, convs):
        y_ref[p, :] = x + _dot((gate * conv).astype(BF16), wout_ref[...])
    _ffn_rows(y_ref, parts, fg_ref, w1_ref, w3_ref, w2_ref)


def _odd(x, g, win, cw, wout, ffn):
    batch, seq, _ = x.shape
    n_tiles = seq // TM
    per = TM // SUBLANES
    main = _rows(TM, D_MODEL)
    prev = pl.BlockSpec((None, SUBLANES, D_MODEL), lambda bi, i: (bi, jnp.maximum(i * per - 1, 0), 0))
    nxt = pl.BlockSpec((None, SUBLANES, D_MODEL), lambda bi, i: (bi, jnp.minimum((i + 1) * per, seq // SUBLANES - 1), 0))
    return pl.pallas_call(
        functools.partial(_odd_kernel, n_tiles=n_tiles),
        grid=(batch, n_tiles),
        in_specs=[prev, main, nxt] + [_resident(t) for t in (g, win, cw, wout, *ffn)],
        out_specs=main,
        out_shape=jax.ShapeDtypeStruct((batch, seq, D_MODEL), F32),
        scratch_shapes=[pltpu.VMEM((ROW_SPLIT, TM // ROW_SPLIT + 2 * SUBLANES, D_MODEL), F32)],
        compiler_params=_params(2),
        name="odd_ffn",
    )(x, x, x, *_arrays((g, win, cw, wout, *ffn)))


def _t5_bucket_np(rel):
    nb = N_BUCKETS // 2
    max_exact = nb // 2
    n = np.abs(rel)
    nf = np.maximum(n, 1).astype(np.float32)
    large = max_exact + (np.log(nf / max_exact) / math.log(MAX_DISTANCE / max_exact) * (nb - max_exact)).astype(np.int32)
    large = np.minimum(large, nb - 1)
    return np.where(rel > 0, nb, 0) + np.where(n < max_exact, n, large)


def _bias_windows(rel_bias):
    n_off = ATT_SUB + ATT_KEYS - 1
    off = np.arange(n_off) - (ATT_SUB - 1) - HALF
    band = np.abs(off) <= HALF
    out = []
    for g, d in enumerate(DILATIONS):
        tab = rel_bias[:, g * HEADS_PER_GROUP:(g + 1) * HEADS_PER_GROUP].astype(F32)
        vec = jnp.take(tab, jnp.asarray(_t5_bucket_np(off * d)), axis=0)
        vec = jnp.where(band[:, None], vec * LOG2E, NEG_INF).T
        flat = jnp.tile(jnp.pad(vec, ((0, 0), (0, 1))), (1, ATT_SUB))[:, :ATT_SUB * n_off]
        skew = flat.reshape(HEADS_PER_GROUP, ATT_SUB, n_off)
        win = skew[:, :, ATT_SUB - 1:ATT_SUB - 1 + ATT_KEYS]
        out.append(win.reshape(N_PAIRS, 2 * ATT_SUB, ATT_KEYS))
    return out


def _head_expand():
    src = np.arange(LANES)
    col = np.arange(N_GROUPS * GROUP_W)
    hit = (src[:, None] % PACK_W == (col // GROUP_W) * HEADS_PER_GROUP + (col % GROUP_W) // HEAD_DIM)[:, :]
    return jnp.asarray(hit & (src[:, None] < WGT_PARTS * PACK_W), BF16)


def _trunk(x, p):
    for layer in range(DEPTH):
        i = layer // 2
        ffn = tuple((p[k], layer) for k in ("ffn_norm", "ffn_w1", "ffn_w3", "ffn_w2"))
        if layer % 2 == 0:
            a, *qkvs = _even_in(x, (p["even_norm"], i), (p["even_w_in"], i), (p["q_gain"], i), (p["k_gain"], i))
            a = _conv_a(a, *((p[k], i) for k in ("conv_a_w", "conv_a_b", "conv_a_ln_g", "conv_a_ln_b")))
            os, ls = zip(*[_attention_group(qkv, bm) for qkv, bm in zip(qkvs, p["bias"])])
            x = _even_out(x, a, os, ls, p["expand"], (p["even_w_out"], i), ffn)
        else:
            x = _odd(x, (p["odd_norm"], i), (p["odd_w_in"], i), (p["conv_c_w"], i), (p["odd_w_out"], i), ffn)
    return x


def _prepare(rel_bias, even_norm, even_w_in, conv_a_w, conv_a_b, conv_a_ln_g, conv_a_ln_b, q_norm, k_norm,
             even_w_out, odd_norm, odd_w_in, conv_c_w, odd_w_out, ffn_norm, ffn_w1, ffn_w3, ffn_w2):
    row = lambda a: a.astype(F32)[:, None, :]
    tile_heads = lambda a: jnp.tile(a.astype(F32), (1, HEADS_PER_GROUP))[:, None, :]
    return dict(
        bias=_bias_windows(rel_bias), expand=_head_expand(),
        even_norm=row(even_norm), even_w_in=even_w_in.astype(BF16),
        conv_a_w=jnp.broadcast_to(conv_a_w.astype(F32)[:, :, None, :], conv_a_w.shape[:2] + (SUBLANES, CONV_A_CH)),
        conv_a_b=row(conv_a_b), conv_a_ln_g=row(conv_a_ln_g), conv_a_ln_b=row(conv_a_ln_b),
        q_gain=tile_heads(q_norm) * (HEAD_DIM ** -0.5 * LOG2E), k_gain=tile_heads(k_norm),
        even_w_out=even_w_out.astype(BF16),
        odd_norm=row(odd_norm), odd_w_in=odd_w_in.astype(BF16), conv_c_w=conv_c_w.astype(F32),
        odd_w_out=odd_w_out.astype(BF16),
        ffn_norm=row(ffn_norm), ffn_w1=ffn_w1.astype(BF16), ffn_w3=ffn_w3.astype(BF16), ffn_w2=ffn_w2.astype(BF16),
    )


def kernel(x_prompt, x_sample, rel_bias, even_norm, even_w_in, conv_a_w, conv_a_b, conv_a_ln_g, conv_a_ln_b, q_norm, k_norm, even_w_out, odd_norm, odd_w_in, conv_c_w, odd_w_out, ffn_norm, ffn_w1, ffn_w3, ffn_w2):
    p = _prepare(rel_bias, even_norm, even_w_in, conv_a_w, conv_a_b, conv_a_ln_g, conv_a_ln_b, q_norm, k_norm,
                 even_w_out, odd_norm, odd_w_in, conv_c_w, odd_w_out, ffn_norm, ffn_w1, ffn_w3, ffn_w2)
    return (_trunk(x_prompt, p), _trunk(x_sample, p))
```

```python
import functools
import math

import numpy as np
import jax
import jax.numpy as jnp
from jax import lax
from jax.experimental import pallas as pl
from jax.experimental.pallas import tpu as pltpu

F32 = jnp.float32
BF16 = jnp.bfloat16

D_MODEL = 1024
DEPTH = 4
HEAD_DIM = 64
WINDOWS = (128, 512, 2048)
DILATIONS = (1, 4, 16)
N_GROUPS = 3
HEADS_PER_GROUP = 8
GROUP_W = HEADS_PER_GROUP * HEAD_DIM
CONV_A_CH = D_MODEL // 2
CONV_A_K = 31
FFN_HIDDEN = 2816
N_BUCKETS = 32
MAX_DISTANCE = 1024
EPS = 1e-6
NEG_INF = -1e30
LOG2E = math.log2(math.e)
HALF = 64
assert all(w // (2 * d) == HALF for w, d in zip(WINDOWS, DILATIONS))

V7X_VMEM_BYTES = 64 * 1024 * 1024
VMEM_LIMIT_BYTES = V7X_VMEM_BYTES * 7 // 8
LANES = 128
SUBLANES = 8
MXU_DIM = 256

TM = 512
ROW_SPLIT = 2
PERM = MXU_DIM
WGT_PARTS = 3
ATT_SUB = 128
ATT_KEYS = ATT_SUB + 2 * HALF
ATT_STEP = 4096
assert ATT_KEYS == 2 * LANES
N_PAIRS = GROUP_W // LANES
CONV_A_TS = 512
CONV_A_RB = 32
CONV_A_HALO = 16
FFN_CHUNKS = ((0, 1536), (1536, FFN_HIDDEN))
PACK_W = N_GROUPS * HEADS_PER_GROUP


def _params(n_axes):
    return pltpu.CompilerParams(dimension_semantics=("parallel",) * n_axes,
                                vmem_limit_bytes=VMEM_LIMIT_BYTES)


def _resident(a):
    if isinstance(a, tuple):
        a, layer = a
        nd = a.ndim
        return pl.BlockSpec((None,) + a.shape[1:], lambda *_: (layer,) + (0,) * (nd - 1), pipeline_mode=pl.Buffered(1))
    nd = a.ndim
    return pl.BlockSpec(a.shape, lambda *_: (0,) * nd, pipeline_mode=pl.Buffered(1))


def _arrays(params):
    return [a[0] if isinstance(a, tuple) else a for a in params]


def _rows(tm, width):
    return pl.BlockSpec((None, tm, width), lambda b, i: (b, i, 0))


def _classes(d, tm, width):
    if d == 1:
        return pl.BlockSpec((None, None, tm, width), lambda b, i: (b, 0, i, 0))
    return pl.BlockSpec((None, d, tm // d, width), lambda b, i: (b, 0, i, 0))


def _qkv_tile(d, tm):
    if d == 1:
        return pl.BlockSpec((None, None, 3, tm, GROUP_W), lambda b, i: (b, 0, 0, i, 0))
    return pl.BlockSpec((None, d, 3, tm // d, GROUP_W), lambda b, i: (b, 0, 0, i, 0))


def _rmsnorm_bf16(x, g):
    ms = jnp.mean(x * x, axis=-1, keepdims=True)
    return (x * lax.rsqrt(ms + EPS) * g).astype(BF16)


def _dot(a, b):
    return jnp.dot(a, b, preferred_element_type=F32)


def _even_in_kernel(x_ref, g_ref, w_ref, qg_ref, kg_ref, a_ref, q0_ref, q1_ref, q2_ref, slab_ref):
    parts = [slice(s * PERM, (s + 1) * PERM) for s in range(x_ref.shape[0] // PERM)]
    x = x_ref[...]
    hf = x * lax.rsqrt(jnp.mean(x * x, axis=-1, keepdims=True) + EPS) * g_ref[...]
    hs = [hf[p].astype(BF16) for p in parts]
    for k in range(D_MODEL // LANES):
        slab_ref[k] = hf[:, k * LANES:(k + 1) * LANES]

    def class_major(s, d):
        n = PERM // d
        slabs = [jnp.concatenate([slab_ref[k, pl.ds(s * PERM + r, n, stride=d), :] for r in range(d)], axis=0)
                 for k in range(D_MODEL // LANES)]
        return jnp.concatenate(slabs, axis=1).astype(BF16)

    def proj(hh, c):
        return _dot(hh, w_ref[:, c * GROUP_W:(c + 1) * GROUP_W])

    vals = [proj(h, 0) for h in hs]
    gates = [proj(h, 1) for h in hs]
    for p, val, gate in zip(parts, vals, gates):
        a_ref[p, :] = val * jax.nn.sigmoid(gate)
    first_head = lax.broadcasted_iota(jnp.int32, (1, LANES), 1) < HEAD_DIM
    for g, (d, out_ref) in enumerate(zip(DILATIONS, (q0_ref, q1_ref, q2_ref))):
        hgs = hs if d == 1 else [class_major(s, d) for s in range(len(parts))]
        for part in range(3):
            ps = [proj(hg, 2 + part * N_GROUPS + g) for hg in hgs]
            if part < 2:
                gain = qg_ref[...] if part == 0 else kg_ref[...]
                mss = []
                for p in ps:
                    cols = []
                    for c in range(0, GROUP_W, LANES):
                        sq = p[:, c:c + LANES] * p[:, c:c + LANES]
                        lo = jnp.sum(jnp.where(first_head, sq, 0.0), axis=-1, keepdims=True)
                        hi = jnp.sum(jnp.where(first_head, 0.0, sq), axis=-1, keepdims=True)
                        cols.append(jnp.where(first_head, lo, hi))
                    mss.append(jnp.concatenate(cols, axis=1))
                ps = [p * lax.rsqrt(ms * (1.0 / HEAD_DIM) + EPS) * gain for p, ms in zip(ps, mss)]
            for s, p in enumerate(ps):
                val = p.astype(BF16)
                if d == 1:
                    out_ref[part, s * PERM:(s + 1) * PERM, :] = val
                else:
                    n = PERM // d
                    for r in range(d):
                        out_ref[r, part, s * n:(s + 1) * n, :] = val[r * n:(r + 1) * n]


def _even_in(x, g, w, qg, kg):
    batch, seq, _ = x.shape
    outs = [jax.ShapeDtypeStruct((batch, seq, CONV_A_CH), F32)]
    outs += [jax.ShapeDtypeStruct((batch, d, 3, seq // d, GROUP_W), BF16) for d in DILATIONS]
    return pl.pallas_call(
        _even_in_kernel,
        grid=(batch, seq // TM),
        in_specs=[_rows(TM, D_MODEL)] + [_resident(a) for a in (g, w, qg, kg)],
        out_specs=[_rows(TM, CONV_A_CH)] + [_qkv_tile(d, TM) for d in DILATIONS],
        out_shape=outs,
        scratch_shapes=[pltpu.VMEM((D_MODEL // LANES, TM, LANES), F32)],
        compiler_params=_params(2),
        name="even_in",
    )(x, *_arrays((g, w, qg, kg)))


def _conv_a_kernel(prev_ref, a_ref, next_ref, wb_ref, b_ref, lg_ref, lb_ref, o_ref, sh_ref, *, n_tiles):
    i = pl.program_id(1)
    ts = a_ref.shape[0]
    h = CONV_A_HALO
    rows = ts + 2 * h
    buf = jnp.concatenate([jnp.where(i > 0, prev_ref[...], 0.0), a_ref[...],
                           jnp.where(i < n_tiles - 1, next_ref[...], 0.0)], axis=0)
    n_sh = rows - SUBLANES
    for v in range(SUBLANES):
        sh_ref[v] = buf[v:v + n_sh, :].reshape(n_sh // SUBLANES, SUBLANES, CONV_A_CH)
    off = h - (CONV_A_K - 1) // 2
    nb = CONV_A_RB // SUBLANES
    for r in range(0, ts, CONV_A_RB):
        t0 = r // SUBLANES
        acc = None
        for v in range(SUBLANES):
            us = [u for u in range((CONV_A_K + off) // SUBLANES + 1) if 0 <= SUBLANES * u + v - off < CONV_A_K]
            x = sh_ref[v, t0 + us[0]:t0 + us[-1] + nb]
            for u in us:
                term = x[u - us[0]:u - us[0] + nb] * wb_ref[SUBLANES * u + v - off]
                acc = term if acc is None else acc + term
        acc = acc.reshape(CONV_A_RB, CONV_A_CH) + b_ref[...]
        mu = jnp.mean(acc, axis=-1, keepdims=True)
        xc = acc - mu
        var = jnp.mean(xc * xc, axis=-1, keepdims=True)
        y = xc * lax.rsqrt(var + EPS) * lg_ref[...] + lb_ref[...]
        o_ref[r:r + CONV_A_RB, :] = (y * jax.nn.sigmoid(y)).astype(BF16)


def _conv_a(a, wb, b, lg, lb):
    batch, seq, _ = a.shape
    ts = CONV_A_TS
    n_tiles = seq // ts
    hb = ts // CONV_A_HALO
    rows = ts + 2 * CONV_A_HALO
    main = _rows(ts, CONV_A_CH)
    prev = pl.BlockSpec((None, CONV_A_HALO, CONV_A_CH), lambda bi, i: (bi, jnp.maximum(i * hb - 1, 0), 0))
    nxt = pl.BlockSpec((None, CONV_A_HALO, CONV_A_CH),
                       lambda bi, i: (bi, jnp.minimum((i + 1) * hb, seq // CONV_A_HALO - 1), 0))
    return pl.pallas_call(
        functools.partial(_conv_a_kernel, n_tiles=n_tiles),
        grid=(batch, n_tiles),
        in_specs=[prev, main, nxt] + [_resident(t) for t in (wb, b, lg, lb)],
        out_specs=main,
        out_shape=jax.ShapeDtypeStruct((batch, seq, CONV_A_CH), BF16),
        scratch_shapes=[pltpu.VMEM((SUBLANES, rows // SUBLANES - 1, SUBLANES, CONV_A_CH), F32)],
        compiler_params=_params(2),
        name="conv_a",
    )(a, a, a, *_arrays((wb, b, lg, lb)))


def _attn_kernel(*refs, n_tiles, halo):
    if halo:
        q_ref, kp_ref, kc_ref, kn_ref, vp_ref, vc_ref, vn_ref, bm_ref, o_ref, l_ref = refs
    else:
        q_ref, kc_ref, vc_ref, bm_ref, o_ref, l_ref = refs
        kp_ref = kn_ref = vp_ref = vn_ref = None
    i = pl.program_id(2)
    n_cls, tq = q_ref.shape[0], q_ref.shape[1]
    lane = lax.broadcasted_iota(jnp.int32, (1, LANES), 1)
    first_head = lane < HEAD_DIM
    keep_first = first_head.astype(BF16)
    keep_second = 1.0 - keep_first
    n_sub = tq // ATT_SUB
    lanes = [slice(pr * LANES, (pr + 1) * LANES) for pr in range(N_PAIRS)]

    def window(cur_ref, before_ref, after_ref, c, r0, cs):
        lo, hi = r0 - HALF, r0 + ATT_SUB + HALF
        pieces = []
        if lo < 0:
            pieces.append(jnp.zeros((HALF, LANES), BF16) if before_ref is None else before_ref[c, :, cs])
        pieces.append(cur_ref[c, max(lo, 0):min(hi, tq), cs])
        if hi > tq:
            pieces.append(jnp.zeros((HALF, LANES), BF16) if after_ref is None else after_ref[c, :, cs])
        return pieces[0] if len(pieces) == 1 else jnp.concatenate(pieces, axis=0)

    for c in range(n_cls):
        for s in range(n_sub):
            r0 = s * ATT_SUB
            edge_lo = edge_hi = None
            if s == 0:
                edge_lo = jnp.where(lane >= jnp.where(i == 0, HALF, 0), 0.0, NEG_INF)
            if s == n_sub - 1:
                edge_hi = jnp.where(lane < jnp.where(i == n_tiles - 1, LANES - HALF, LANES), 0.0, NEG_INF)
            scs = []
            for pr, cs in enumerate(lanes):
                q2 = q_ref[c, r0:r0 + ATT_SUB, cs]
                qs = jnp.concatenate([q2 * keep_first, q2 * keep_second], axis=0)
                sc = lax.dot_general(qs, window(kc_ref, kp_ref, kn_ref, c, r0, cs), (((1,), (1,)), ((), ())),
                                     preferred_element_type=F32) + bm_ref[pr]
                if edge_lo is not None or edge_hi is not None:
                    left, right = sc[:, :LANES], sc[:, LANES:]
                    left = left if edge_lo is None else left + edge_lo
                    right = right if edge_hi is None else right + edge_hi
                    sc = jnp.concatenate([left, right], axis=1)
                scs.append(sc)
            ms = [jnp.max(sc, axis=-1, keepdims=True) for sc in scs]
            ps = [jnp.exp2(sc - m) for sc, m in zip(scs, ms)]
            dens = [jnp.sum(p, axis=-1, keepdims=True) for p in ps]
            pvs = [_dot(p.astype(BF16), window(vc_ref, vp_ref, vn_ref, c, r0, cs)) for p, cs in zip(ps, lanes)]
            stat_rows = jnp.zeros((ATT_SUB, LANES), F32)
            for pr, (cs, pv, m, den) in enumerate(zip(lanes, pvs, ms, dens)):
                o_ref[c, r0:r0 + ATT_SUB, cs] = jnp.where(first_head, pv[0:ATT_SUB], pv[ATT_SUB:]).astype(BF16)
                for hd, rows in ((2 * pr, slice(0, ATT_SUB)), (2 * pr + 1, slice(ATT_SUB, 2 * ATT_SUB))):
                    both = jnp.where(lane < HEADS_PER_GROUP, m[rows], den[rows])
                    stat_rows = jnp.where((lane == hd) | (lane == HEADS_PER_GROUP + hd), both, stat_rows)
            l_ref[c, r0:r0 + ATT_SUB, :] = stat_rows


def _attention_group(qkv, bm):
    batch, d, _, cls_len, _ = qkv.shape
    tq = min(cls_len, ATT_STEP)
    n_tiles = cls_len // tq
    halo = n_tiles > 1
    n_cls = 1 if halo else min(d, ATT_STEP // tq)
    hb = tq // HALF

    def main(part):
        return pl.BlockSpec((None, n_cls, None, tq, GROUP_W), lambda b, r, i: (b, r, part, i, 0))

    def before(part):
        return pl.BlockSpec((None, 1, None, HALF, GROUP_W),
                            lambda b, r, i: (b, r, part, jnp.maximum(i * hb - 1, 0), 0))

    def after(part):
        return pl.BlockSpec((None, 1, None, HALF, GROUP_W),
                            lambda b, r, i: (b, r, part, jnp.minimum((i + 1) * hb, cls_len // HALF - 1), 0))

    def out(width):
        return pl.BlockSpec((None, n_cls, tq, width), lambda b, r, i: (b, r, i, 0))

    if halo:
        in_specs = [main(0), before(1), main(1), after(1), before(2), main(2), after(2), _resident(bm)]
    else:
        in_specs = [main(0), main(1), main(2), _resident(bm)]
    return pl.pallas_call(
        functools.partial(_attn_kernel, n_tiles=n_tiles, halo=halo),
        grid=(batch, d // n_cls, n_tiles),
        in_specs=in_specs,
        out_specs=[out(GROUP_W), out(LANES)],
        out_shape=[jax.ShapeDtypeStruct((batch, d, cls_len, GROUP_W), BF16),
                   jax.ShapeDtypeStruct((batch, d, cls_len, LANES), F32)],
        compiler_params=_params(3),
        name=f"attn_d{d}",
    )(*([qkv] * (len(in_specs) - 1)), bm)


def _token_order(ref, d, slab_ref, s):
    if d == 1:
        return ref[s * PERM:(s + 1) * PERM, :].astype(F32)
    n = PERM // d
    n_slabs = slab_ref.shape[0]
    for r in range(d):
        rows = ref[r, s * n:(s + 1) * n, :].astype(F32)
        for k in range(n_slabs):
            slab_ref[k, pl.ds(s * PERM + r, n, stride=d), :] = rows[:, k * LANES:(k + 1) * LANES]
    return jnp.concatenate([slab_ref[k, s * PERM:(s + 1) * PERM, :] for k in range(n_slabs)], axis=1)


def _ffn_rows(y_ref, parts, g_ref, w1_ref, w3_ref, w2_ref):
    xs = [y_ref[p, :] for p in parts]
    hs = [_rmsnorm_bf16(x, g_ref[...]) for x in xs]
    ys = xs
    for lo, hi in FFN_CHUNKS:
        acts = []
        for h in hs:
            a = _dot(h, w1_ref[:, lo:hi])
            acts.append((a * jax.nn.sigmoid(a) * _dot(h, w3_ref[:, lo:hi])).astype(BF16))
        ys = [y + _dot(act, w2_ref[lo:hi, :]) for y, act in zip(ys, acts)]
    for p, y in zip(parts, ys):
        y_ref[p, :] = y


def _even_out_kernel(x_ref, a_ref, o0_ref, o1_ref, o2_ref, l0_ref, l1_ref, l2_ref, ex_ref,
                     w_ref, fg_ref, w1_ref, w3_ref, w2_ref, y_ref, os1_ref, os2_ref, ls1_ref, ls2_ref):
    lane = lax.broadcasted_iota(jnp.int32, (1, LANES), 1)
    parts = [slice(s * PERM, (s + 1) * PERM) for s in range(x_ref.shape[0] // PERM)]
    blocks = range(len(parts))
    l_groups = tuple(zip((l0_ref, l1_ref, l2_ref), DILATIONS, (None, ls1_ref, ls2_ref)))
    o_groups = tuple(zip((o0_ref, o1_ref, o2_ref), DILATIONS, (None, os1_ref, os2_ref)))
    stats = [[_token_order(l_ref, d, slab, s) for l_ref, d, slab in l_groups] for s in blocks]
    ms = [jnp.maximum(jnp.maximum(st[0], st[1]), st[2]) for st in stats]
    es = [[jnp.exp2(t - m) for t in st] for st, m in zip(stats, ms)]
    dens = [[pltpu.roll(t, LANES - HEADS_PER_GROUP, 1) for t in st] for st in stats]
    invs = [1.0 / (e[0] * dn[0] + e[1] * dn[1] + e[2] * dn[2]) for e, dn in zip(es, dens)]
    wgts = []
    for e, inv in zip(es, invs):
        packed = None
        for g in range(N_GROUPS):
            al = jnp.where(lane < HEADS_PER_GROUP, e[g] * inv, 0.0)
            al = al if g == 0 else pltpu.roll(al, g * HEADS_PER_GROUP, 1)
            packed = al if packed is None else packed + al
        terms, rest = None, packed
        for t in range(WGT_PARTS):
            part = rest.astype(BF16).astype(F32)
            rest = rest - part
            part = part if t == 0 else pltpu.roll(part, t * PACK_W, 1)
            terms = part if terms is None else terms + part
        wgts.append(_dot(terms.astype(BF16), ex_ref[...]))
    outs = []
    for s, wgt in zip(blocks, wgts):
        o = None
        for g, (o_ref, d, slab) in enumerate(o_groups):
            term = wgt[:, g * GROUP_W:(g + 1) * GROUP_W] * _token_order(o_ref, d, slab, s)
            o = term if o is None else o + term
        outs.append(o.astype(BF16))
    for p, o in zip(parts, outs):
        y_ref[p, :] = x_ref[p, :] + _dot(a_ref[p, :], w_ref[0:CONV_A_CH, :]) + _dot(o, w_ref[CONV_A_CH:, :])
    _ffn_rows(y_ref, parts, fg_ref, w1_ref, w3_ref, w2_ref)


def _even_out(x, a, os, ls, ex, w, ffn):
    batch, seq, _ = x.shape
    return pl.pallas_call(
        _even_out_kernel,
        grid=(batch, seq // TM),
        in_specs=[_rows(TM, D_MODEL), _rows(TM, CONV_A_CH)]
                 + [_classes(d, TM, GROUP_W) for d in DILATIONS] + [_classes(d, TM, LANES) for d in DILATIONS]
                 + [_resident(t) for t in (ex, w, *ffn)],
        out_specs=_rows(TM, D_MODEL),
        out_shape=jax.ShapeDtypeStruct((batch, seq, D_MODEL), F32),
        scratch_shapes=[pltpu.VMEM((GROUP_W // LANES, TM, LANES), F32)] * 2 + [pltpu.VMEM((1, TM, LANES), F32)] * 2,
        compiler_params=_params(2),
        name="even_out_ffn",
    )(x, a, *os, *ls, *_arrays((ex, w, *ffn)))


def _odd_kernel(prev_ref, x_ref, next_ref, g_ref, win_ref, cw_ref, wout_ref, fg_ref, w1_ref, w3_ref, w2_ref,
                y_ref, buf_ref, *, n_tiles):
    i = pl.program_id(1)
    tm = x_ref.shape[0]
    hb = SUBLANES
    n_split = buf_ref.shape[0]
    rows = tm // n_split
    parts = [slice(k * rows, (k + 1) * rows) for k in range(n_split)]
    g = g_ref[...]
    xs = [x_ref[p, :] for p in parts]
    hs, keep = [], []
    for k, p in enumerate(parts):
        before = prev_ref[...] if k == 0 else x_ref[p.start - hb:p.start, :]
        after = next_ref[...] if k == n_split - 1 else x_ref[p.stop:p.stop + hb, :]
        halo = jnp.concatenate([before, after], axis=0)
        hs.append(jnp.concatenate([_rmsnorm_bf16(xs[k], g), _rmsnorm_bf16(halo, g)], axis=0))
        keep.append((i > 0 if k == 0 else None, i < n_tiles - 1 if k == n_split - 1 else None))
    gates = [_dot(h[0:rows], win_ref[:, 0:D_MODEL]) for h in hs]
    cus = [_dot(h, win_ref[:, D_MODEL:2 * D_MODEL]) * _dot(h, win_ref[:, 2 * D_MODEL:3 * D_MODEL]) for h in hs]
    for k, (cu, (keep_lo, keep_hi)) in enumerate(zip(cus, keep)):
        lo, hi = cu[rows:rows + hb], cu[rows + hb:]
        buf_ref[k, 0:hb, :] = lo if keep_lo is None else jnp.where(keep_lo, lo, 0.0)
        buf_ref[k, hb:hb + rows, :] = cu[0:rows]
        buf_ref[k, hb + rows:2 * hb + rows, :] = hi if keep_hi is None else jnp.where(keep_hi, hi, 0.0)
    convs = [buf_ref[k, hb - 1:hb - 1 + rows, :] * cw_ref[0:1, :] + buf_ref[k, hb:hb + rows, :] * cw_ref[1:2, :]
             + buf_ref[k, hb + 1:hb + 1 + rows, :] * cw_ref[2:3, :] for k in range(n_split)]
    for p, x, gate, conv in zip(parts, xs, gates, convs):
        y_ref[p, :] = x + _dot((gate * conv).astype(BF16), wout_ref[...])
    _ffn_rows(y_ref, parts, fg_ref, w1_ref, w3_ref, w2_ref)


def _odd(x, g, win, cw, wout, ffn):
    batch, seq, _ = x.shape
    n_tiles = seq // TM
    per = TM // SUBLANES
    main = _rows(TM, D_MODEL)
    prev = pl.BlockSpec((None, SUBLANES, D_MODEL), lambda bi, i: (bi, jnp.maximum(i * per - 1, 0), 0))
    nxt = pl.BlockSpec((None, SUBLANES, D_MODEL), lambda bi, i: (bi, jnp.minimum((i + 1) * per, seq // SUBLANES - 1), 0))
    return pl.pallas_call(
        functools.partial(_odd_kernel, n_tiles=n_tiles),
        grid=(batch, n_tiles),
        in_specs=[prev, main, nxt] + [_resident(t) for t in (g, win, cw, wout, *ffn)],
        out_specs=main,
        out_shape=jax.ShapeDtypeStruct((batch, seq, D_MODEL), F32),
        scratch_shapes=[pltpu.VMEM((ROW_SPLIT, TM // ROW_SPLIT + 2 * SUBLANES, D_MODEL), F32)],
        compiler_params=_params(2),
        name="odd_ffn",
    )(x, x, x, *_arrays((g, win, cw, wout, *ffn)))


def _t5_bucket_np(rel):
    nb = N_BUCKETS // 2
    max_exact = nb // 2
    n = np.abs(rel)
    nf = np.maximum(n, 1).astype(np.float32)
    large = max_exact + (np.log(nf / max_exact) / math.log(MAX_DISTANCE / max_exact) * (nb - max_exact)).astype(np.int32)
    large = np.minimum(large, nb - 1)
    return np.where(rel > 0, nb, 0) + np.where(n < max_exact, n, large)


def _bias_windows(rel_bias):
    n_off = ATT_SUB + ATT_KEYS - 1
    off = np.arange(n_off) - (ATT_SUB - 1) - HALF
    band = np.abs(off) <= HALF
    out = []
    for g, d in enumerate(DILATIONS):
        tab = rel_bias[:, g * HEADS_PER_GROUP:(g + 1) * HEADS_PER_GROUP].astype(F32)
        vec = jnp.take(tab, jnp.asarray(_t5_bucket_np(off * d)), axis=0)
        vec = jnp.where(band[:, None], vec * LOG2E, NEG_INF).T
        flat = jnp.tile(jnp.pad(vec, ((0, 0), (0, 1))), (1, ATT_SUB))[:, :ATT_SUB * n_off]
        skew = flat.reshape(HEADS_PER_GROUP, ATT_SUB, n_off)
        win = skew[:, :, ATT_SUB - 1:ATT_SUB - 1 + ATT_KEYS]
        out.append(win.reshape(N_PAIRS, 2 * ATT_SUB, ATT_KEYS))
    return out


def _head_expand():
    src = np.arange(LANES)
    col = np.arange(N_GROUPS * GROUP_W)
    hit = (src[:, None] % PACK_W == (col // GROUP_W) * HEADS_PER_GROUP + (col % GROUP_W) // HEAD_DIM)[:, :]
    return jnp.asarray(hit & (src[:, None] < WGT_PARTS * PACK_W), BF16)


def _trunk(x, p):
    for layer in range(DEPTH):
        i = layer // 2
        ffn = tuple((p[k], layer) for k in ("ffn_norm", "ffn_w1", "ffn_w3", "ffn_w2"))
        if layer % 2 == 0:
            a, *qkvs = _even_in(x, (p["even_norm"], i), (p["even_w_in"], i), (p["q_gain"], i), (p["k_gain"], i))
            a = _conv_a(a, *((p[k], i) for k in ("conv_a_w", "conv_a_b", "conv_a_ln_g", "conv_a_ln_b")))
            os, ls = zip(*[_attention_group(qkv, bm) for qkv, bm in zip(qkvs, p["bias"])])
            x = _even_out(x, a, os, ls, p["expand"], (p["even_w_out"], i), ffn)
        else:
            x = _odd(x, (p["odd_norm"], i), (p["odd_w_in"], i), (p["conv_c_w"], i), (p["odd_w_out"], i), ffn)
    return x


def _prepare(rel_bias, even_norm, even_w_in, conv_a_w, conv_a_b, conv_a_ln_g, conv_a_ln_b, q_norm, k_norm,
             even_w_out, odd_norm, odd_w_in, conv_c_w, odd_w_out, ffn_norm, ffn_w1, ffn_w3, ffn_w2):
    row = lambda a: a.astype(F32)[:, None, :]
    tile_heads = lambda a: jnp.tile(a.astype(F32), (1, HEADS_PER_GROUP))[:, None, :]
    return dict(
        bias=_bias_windows(rel_bias), expand=_head_expand(),
        even_norm=row(even_norm), even_w_in=even_w_in.astype(BF16),
        conv_a_w=jnp.broadcast_to(conv_a_w.astype(F32)[:, :, None, :], conv_a_w.shape[:2] + (SUBLANES, CONV_A_CH)),
        conv_a_b=row(conv_a_b), conv_a_ln_g=row(conv_a_ln_g), conv_a_ln_b=row(conv_a_ln_b),
        q_gain=tile_heads(q_norm) * (HEAD_DIM ** -0.5 * LOG2E), k_gain=tile_heads(k_norm),
        even_w_out=even_w_out.astype(BF16),
        odd_norm=row(odd_norm), odd_w_in=odd_w_in.astype(BF16), conv_c_w=conv_c_w.astype(F32),
        odd_w_out=odd_w_out.astype(BF16),
        ffn_norm=row(ffn_norm), ffn_w1=ffn_w1.astype(BF16), ffn_w3=ffn_w3.astype(BF16), ffn_w2=ffn_w2.astype(BF16),
    )


def kernel(x_prompt, x_sample, rel_bias, even_norm, even_w_in, conv_a_w, conv_a_b, conv_a_ln_g, conv_a_ln_b, q_norm, k_norm, even_w_out, odd_norm, odd_w_in, conv_c_w, odd_w_out, ffn_norm, ffn_w1, ffn_w3, ffn_w2):
    p = _prepare(rel_bias, even_norm, even_w_in, conv_a_w, conv_a_b, conv_a_ln_g, conv_a_ln_b, q_norm, k_norm,
                 even_w_out, odd_norm, odd_w_in, conv_c_w, odd_w_out, ffn_norm, ffn_w1, ffn_w3, ffn_w2)
    return (_trunk(x_prompt, p), _trunk(x_sample, p))
```
